```python
import jax, jax.numpy as jnp
from jax import lax
import numpy as np

D_MODEL = 1024
BATCH = 2
SEQ = 16384
DEPTH = 2

HEAD_DIM = 64
MIX_WIDTH = D_MODEL
MOBA_HEADS = 6
MLSTM_HEADS = 4
SWA_Q_HEADS = 6
SWA_KV_HEADS = 2
MOBA_WIDTH = MOBA_HEADS * HEAD_DIM
MLSTM_WIDTH = MLSTM_HEADS * HEAD_DIM
SWA_Q_WIDTH = SWA_Q_HEADS * HEAD_DIM
SWA_KV_WIDTH = SWA_KV_HEADS * HEAD_DIM
MOBA_BLOCK = 256
MOBA_TOPK = 3
MOBA_Q_CHUNK = 64
MLSTM_CHUNK = 64
CONV_WIDTH = 4
SWA_WINDOW = 128
ROPE_THETA = 10000.0
D_FF = 4 * D_MODEL
NORM_EPS = 1e-6
IN_SPLITS = (MOBA_WIDTH, MOBA_WIDTH, MOBA_WIDTH, SWA_Q_WIDTH, SWA_KV_WIDTH, SWA_KV_WIDTH, 2 * MLSTM_WIDTH, MLSTM_WIDTH, MLSTM_WIDTH, MLSTM_HEADS, MLSTM_HEADS)
IN_WIDTH = sum(IN_SPLITS)

kernel_name = 'hybrid_moba_mlstm_swa_block'


def rms_norm(x, g):
    xf = x.astype(jnp.float32)
    y = xf * lax.rsqrt(jnp.mean(xf * xf, axis=-1, keepdims=True) + NORM_EPS)
    return (y * g.astype(jnp.float32)).astype(x.dtype)


def rope_tables(seq):
    inv = ROPE_THETA ** (-jnp.arange(0, HEAD_DIM, 2, dtype=jnp.float32) / HEAD_DIM)
    ang = jnp.arange(seq, dtype=jnp.float32)[:, None] * inv[None, :]
    return jnp.cos(ang), jnp.sin(ang)


def apply_rope(x, cos, sin):
    c = cos[:, None, :].astype(x.dtype)
    s = sin[:, None, :].astype(x.dtype)
    x1, x2 = jnp.split(x, 2, axis=-1)
    return jnp.concatenate([x1 * c - x2 * s, x2 * c + x1 * s], axis=-1)


def causal_depthwise_conv(x, w, b):
    C = x.shape[-1]
    y = lax.conv_general_dilated(x, w[:, None, :].astype(x.dtype), (1,), [(w.shape[0] - 1, 0)],
                                 dimension_numbers=('NWC', 'WIO', 'NWC'), feature_group_count=C)
    return y + b.astype(x.dtype)


def moba_attention(q, k, v):
    B, H, S, D = q.shape
    L = MOBA_BLOCK
    n_blocks = -(-S // L)
    s_pad = n_blocks * L
    if s_pad != S:
        padw = ((0, 0), (0, 0), (0, s_pad - S), (0, 0))
        q, k, v = jnp.pad(q, padw), jnp.pad(k, padw), jnp.pad(v, padw)
    k_blk = k.reshape(B, H, n_blocks, L, D)
    v_blk = v.reshape(B, H, n_blocks, L, D)
    k_mean = jnp.mean(k_blk.astype(jnp.float32), axis=3)
    top = min(MOBA_TOPK, n_blocks)
    scale = D ** -0.5
    Qc = MOBA_Q_CHUNK
    b_idx = jnp.arange(B)[:, None, None, None]
    h_idx = jnp.arange(H)[None, :, None, None]

    def one_chunk(c):
        start = c * Qc
        blk = start // L
        qc = lax.dynamic_slice_in_dim(q, start, Qc, axis=2)
        gate = jnp.einsum('bhqd,bhnd->bhqn', qc.astype(jnp.float32), k_mean)
        gate = jnp.where(jnp.arange(n_blocks) < blk, gate, -jnp.inf)
        _, sel = lax.top_k(gate, top)
        sel_ok = jnp.arange(top) < blk
        k_sel = k_blk[b_idx, h_idx, sel]
        v_sel = v_blk[b_idx, h_idx, sel]
        s_sel = jnp.einsum('bhqd,bhqrld->bhqrl', qc, k_sel).astype(jnp.float32) * scale
        s_sel = jnp.where(sel_ok[:, None], s_sel, -jnp.inf).reshape(B, H, Qc, top * L)
        k_own = lax.dynamic_slice_in_dim(k, blk * L, L, axis=2)
        v_own = lax.dynamic_slice_in_dim(v, blk * L, L, axis=2)
        s_own = jnp.einsum('bhqd,bhld->bhql', qc, k_own).astype(jnp.float32) * scale
        q_pos = start + jnp.arange(Qc)
        k_pos = blk * L + jnp.arange(L)
        s_own = jnp.where(k_pos[None, :] <= q_pos[:, None], s_own, -jnp.inf)
        p = jax.nn.softmax(jnp.concatenate([s_sel, s_own], axis=-1), axis=-1).astype(v.dtype)
        p_sel = p[..., :top * L].reshape(B, H, Qc, top, L)
        p_own = p[..., top * L:]
        return (jnp.einsum('bhqrl,bhqrld->bhqd', p_sel, v_sel)
                + jnp.einsum('bhql,bhld->bhqd', p_own, v_own))

    outs = lax.map(one_chunk, jnp.arange(s_pad // Qc))
    out = jnp.moveaxis(outs, 0, 2).reshape(B, H, s_pad, D)
    return out[:, :, :S]


def mlstm_chunkwise(q, k, v, i_pre, f_pre):
    B, H, S, D = q.shape
    L = MLSTM_CHUNK
    nc = S // L
    k = k * (D ** -0.5)
    log_f = jax.nn.log_sigmoid(f_pre)

    def chunks(a):
        return jnp.moveaxis(a.reshape(B, H, nc, L, *a.shape[3:]), 2, 0)

    causal = jnp.tril(jnp.ones((L, L), dtype=bool))

    def step(carry, inp):
        C, n, m = carry
        qc, kc, vc, ic, lfc = inp
        b = jnp.cumsum(lfc, axis=-1)
        a = b[..., -1]
        d = b[..., :, None] - b[..., None, :] + ic[..., None, :]
        d = jnp.where(causal, d, -jnp.inf)
        inter = b + m[..., None]
        m_t = jnp.maximum(inter, jnp.max(d, axis=-1))
        w_intra = jnp.exp(d - m_t[..., None])
        w_inter = jnp.exp(inter - m_t)
        qk = jnp.einsum('bhtd,bhsd->bhts', qc, kc) * w_intra
        num = (w_inter[..., None] * jnp.einsum('bhtd,bhde->bhte', qc, C)
               + jnp.einsum('bhts,bhse->bhte', qk, vc))
        den = w_inter * jnp.einsum('bhtd,bhd->bht', qc, n) + jnp.sum(qk, axis=-1)
        h = num / jnp.maximum(jnp.abs(den), jnp.exp(-m_t))[..., None]
        g = a[..., None] - b + ic
        m_new = jnp.maximum(a + m, jnp.max(g, axis=-1))
        w_s = jnp.exp(g - m_new[..., None])
        decay = jnp.exp(a + m - m_new)
        C_new = decay[..., None, None] * C + jnp.einsum('bhsd,bhse->bhde', kc * w_s[..., None], vc)
        n_new = decay[..., None] * n + jnp.einsum('bhs,bhsd->bhd', w_s, kc)
        return (C_new, n_new, m_new), h

    init = (jnp.zeros((B, H, D, D), jnp.float32), jnp.zeros((B, H, D), jnp.float32),
            jnp.zeros((B, H), jnp.float32))
    _, h = lax.scan(step, init, (chunks(q), chunks(k), chunks(v), chunks(i_pre), chunks(log_f)))
    return jnp.moveaxis(h, 0, 2).reshape(B, H, S, D)


def swa_attention(q, k, v, sinks):
    B, S, Hq, D = q.shape
    Hkv = k.shape[2]
    G = Hq // Hkv
    W = SWA_WINDOW
    nb = S // W
    qb = q.reshape(B, nb, W, Hkv, G, D)

    def band(a):
        ab = a.reshape(B, nb, W, Hkv, D)
        prev = jnp.pad(ab, ((0, 0), (1, 0), (0, 0), (0, 0), (0, 0)))[:, :-1]
        return jnp.concatenate([prev, ab], axis=2)

    kb, vb = band(k), band(v)
    s = jnp.einsum('bnqkgd,bnjkd->bnkgqj', qb, kb).astype(jnp.float32) * (D ** -0.5)
    qi = jnp.arange(W)[:, None] + W
    kj = jnp.arange(2 * W)[None, :]
    in_window = (qi - kj >= 0) & (qi - kj < W)
    exists = (jnp.arange(nb)[:, None, None] > 0) | (kj[None] >= W)
    mask = in_window[None] & exists
    s = jnp.where(mask[None, :, None, None], s, -jnp.inf)
    sink = jnp.broadcast_to(sinks.astype(jnp.float32).reshape(1, 1, Hkv, G, 1, 1), s.shape[:-1] + (1,))
    p = jax.nn.softmax(jnp.concatenate([s, sink], axis=-1), axis=-1)[..., :-1].astype(v.dtype)
    o = jnp.einsum('bnkgqj,bnjkd->bnqkgd', p, vb)
    return o.reshape(B, S, Hq, D)


def hybrid_layer(x, cos, sin, ln1, w_in, conv_w, conv_b, igate_b, fgate_b, mlstm_norm,
                 moba_q_norm, moba_k_norm, swa_q_norm, swa_k_norm, swa_sinks, w_out,
                 ln2, w_up, w_down):
    B, S, _ = x.shape
    hn = rms_norm(x, ln1)
    proj = hn @ w_in
    idx = np.cumsum(IN_SPLITS)[:-1].tolist()
    mq, mk, mv, sq, sk, sv, xqk, xv, xo, xi, xf = jnp.split(proj, idx, axis=-1)

    def heads(a):
        return a.reshape(B, S, -1, HEAD_DIM)

    def bhsd(a):
        return a.transpose(0, 2, 1, 3)

    mq = apply_rope(rms_norm(heads(mq), moba_q_norm), cos, sin)
    mk = apply_rope(rms_norm(heads(mk), moba_k_norm), cos, sin)
    y_moba = bhsd(moba_attention(bhsd(mq), bhsd(mk), bhsd(heads(mv)))).reshape(B, S, MOBA_WIDTH)

    qk = jax.nn.silu(causal_depthwise_conv(xqk, conv_w, conv_b))
    lq, lk = jnp.split(qk, 2, axis=-1)
    i_pre = (xi + igate_b).astype(jnp.float32).transpose(0, 2, 1)
    f_pre = (xf + fgate_b).astype(jnp.float32).transpose(0, 2, 1)
    h = mlstm_chunkwise(bhsd(heads(lq)).astype(jnp.float32), bhsd(heads(lk)).astype(jnp.float32),
                        bhsd(heads(xv)).astype(jnp.float32), i_pre, f_pre)
    h = bhsd(h).astype(x.dtype) * heads(jax.nn.sigmoid(xo))
    y_mlstm = rms_norm(h, mlstm_norm).reshape(B, S, MLSTM_WIDTH)

    sq = apply_rope(rms_norm(heads(sq), swa_q_norm), cos, sin)
    sk = apply_rope(rms_norm(heads(sk), swa_k_norm), cos, sin)
    y_swa = swa_attention(sq, sk, heads(sv), swa_sinks).reshape(B, S, SWA_Q_WIDTH)

    x = x + jnp.concatenate([y_moba, y_mlstm, y_swa], axis=-1) @ w_out
    u = rms_norm(x, ln2) @ w_up
    return x + jnp.square(jax.nn.relu(u)) @ w_down


def setup_inputs(seed: int = 0) -> dict:
    key = jax.random.key(seed)
    ks = jax.random.split(key, 17)
    f32 = jnp.float32

    def nrm(k, shape, scale):
        return jax.random.normal(k, shape, f32) * scale

    return {
        'x': nrm(ks[0], (BATCH, SEQ, D_MODEL), 1.0),
        'ln1': 1.0 + nrm(ks[1], (DEPTH, D_MODEL), 0.02),
        'w_in': nrm(ks[2], (DEPTH, D_MODEL, IN_WIDTH), D_MODEL ** -0.5),
        'conv_w': nrm(ks[3], (DEPTH, CONV_WIDTH, 2 * MLSTM_WIDTH), CONV_WIDTH ** -0.5),
        'conv_b': nrm(ks[4], (DEPTH, 2 * MLSTM_WIDTH), 0.02),
        'igate_b': nrm(ks[5], (DEPTH, MLSTM_HEADS), 0.1),
        'fgate_b': jnp.linspace(3.0, 6.0, MLSTM_HEADS, dtype=f32)[None] + nrm(ks[6], (DEPTH, MLSTM_HEADS), 0.1),
        'mlstm_norm': 1.0 + nrm(ks[7], (DEPTH, MLSTM_HEADS, HEAD_DIM), 0.02),
        'moba_q_norm': 1.0 + nrm(ks[8], (DEPTH, HEAD_DIM), 0.02),
        'moba_k_norm': 1.0 + nrm(ks[9], (DEPTH, HEAD_DIM), 0.02),
        'swa_q_norm': 1.0 + nrm(ks[10], (DEPTH, HEAD_DIM), 0.02),
        'swa_k_norm': 1.0 + nrm(ks[11], (DEPTH, HEAD_DIM), 0.02),
        'swa_sinks': nrm(ks[12], (DEPTH, SWA_Q_HEADS), 1.0),
        'w_out': nrm(ks[13], (DEPTH, MIX_WIDTH, D_MODEL), MIX_WIDTH ** -0.5),
        'ln2': 1.0 + nrm(ks[14], (DEPTH, D_MODEL), 0.02),
        'w_up': nrm(ks[15], (DEPTH, D_MODEL, D_FF), D_MODEL ** -0.5),
        'w_down': nrm(ks[16], (DEPTH, D_FF, D_MODEL), D_FF ** -0.5),
    }


def reference(x, ln1, w_in, conv_w, conv_b, igate_b, fgate_b, mlstm_norm, moba_q_norm,
              moba_k_norm, swa_q_norm, swa_k_norm, swa_sinks, w_out, ln2, w_up, w_down):
    cos, sin = rope_tables(x.shape[1])
    for l in range(DEPTH):
        x = hybrid_layer(x, cos, sin, ln1[l], w_in[l], conv_w[l], conv_b[l], igate_b[l], fgate_b[l],
                         mlstm_norm[l], moba_q_norm[l], moba_k_norm[l], swa_q_norm[l], swa_k_norm[l],
                         swa_sinks[l], w_out[l], ln2[l], w_up[l], w_down[l])
    return x
```

```python
import functools

import jax
import jax.numpy as jnp
from jax import lax
from jax.experimental import pallas as pl
from jax.experimental.pallas import tpu as pltpu

F32 = jnp.float32
BF16 = jnp.bfloat16
NEG_INF = float("-inf")

D_MODEL = 1024
HEAD_DIM = 64
MOBA_HEADS = 6
MLSTM_HEADS = 4
SWA_Q_HEADS = 6
SWA_KV_HEADS = 2
SWA_GROUP = SWA_Q_HEADS // SWA_KV_HEADS
MOBA_WIDTH = MOBA_HEADS * HEAD_DIM
MLSTM_WIDTH = MLSTM_HEADS * HEAD_DIM
SWA_Q_WIDTH = SWA_Q_HEADS * HEAD_DIM
SWA_KV_WIDTH = SWA_KV_HEADS * HEAD_DIM
MOBA_BLOCK = 256
MOBA_TOPK = 3
MLSTM_CHUNK = 64
CONV_WIDTH = 4
SWA_WINDOW = 128
ROPE_THETA = 10000.0
D_FF = 4 * D_MODEL
NORM_EPS = 1e-6
SM_SCALE = HEAD_DIM ** -0.5

LANES_V7X = 128
VMEM_BYTES_V7X = 64 * 1024 * 1024
HEAD_PAIR = 2 * HEAD_DIM
assert HEAD_PAIR == LANES_V7X

NAT_WIDTH = MOBA_WIDTH + SWA_KV_WIDTH + 2 * MLSTM_WIDTH
KN_WIDTH = MOBA_WIDTH + SWA_KV_WIDTH
QT_ROWS = MOBA_WIDTH + SWA_Q_WIDTH
TR_ROWS = QT_ROWS + MOBA_WIDTH + SWA_KV_WIDTH + 2 * MLSTM_WIDTH + 2 * MLSTM_HEADS
NUM_GATES = 2 * MLSTM_HEADS

IN_PROJ_ROWS = 512
OUT_MLP_ROWS = 512
MLP_FF_CHUNK = 1024
SWA_ROWS = 512
MLSTM_ROWS = 256
CONV_HALO = 16


def _dot(a, b):
    return jnp.dot(a, b, preferred_element_type=F32)


def _dot_nt(a, b):
    return lax.dot_general(a, b, (((1,), (1,)), ((), ())), preferred_element_type=F32)


def _split_bf16(v):
    hi = v.astype(BF16)
    lo = (v - hi.astype(F32)).astype(BF16)
    return hi, lo


def _vmem_limit(nbytes):
    return int(min(nbytes, VMEM_BYTES_V7X - 4 * 1024 * 1024))


def _const_spec(shape):
    nd = len(shape)
    return pl.BlockSpec(shape, lambda *_: (0,) * nd, pipeline_mode=pl.Buffered(1))


def _in_proj_kernel(x_ref, ln1_ref, wnat_ref, wtr_ref, bd_ref, gk_ref, cosn_ref, sinn_ref,
                    gq_ref, cost_ref, sint_ref, gbias_ref,
                    mk_ref, kmean_ref, sk_ref, xqk_ref, mqt_ref, sqt_ref, mvt_ref, svt_ref,
                    xvt_ref, xot_ref, gt_ref):
    tm = x_ref.shape[0]
    x = x_ref[...]
    ms = jnp.mean(x * x, axis=-1, keepdims=True)
    hn = (x * lax.rsqrt(ms + NORM_EPS) * ln1_ref[...]).astype(BF16)
    nat = _dot(hn, wnat_ref[...])
    tr = _dot_nt(wtr_ref[...], hn)

    kk = nat[:, :KN_WIDTH]
    hi, lo = _split_bf16(kk * kk)
    bd = bd_ref[...]
    bw = bd.shape[0]
    msk = jnp.concatenate(
        [_dot(hi[:, c:c + bw], bd) + _dot(lo[:, c:c + bw], bd) for c in range(0, KN_WIDTH, bw)],
        axis=1)
    kn = kk * lax.rsqrt(msk + NORM_EPS) * gk_ref[...]
    reps = KN_WIDTH // LANES_V7X
    cosn = jnp.concatenate([cosn_ref[...]] * reps, axis=1)
    sinn = jnp.concatenate([sinn_ref[...]] * reps, axis=1)
    lane = lax.broadcasted_iota(jnp.int32, kn.shape, 1)
    first_half = (lane % HEAD_DIM) < (HEAD_DIM // 2)
    swapped = jnp.where(first_half,
                        pltpu.roll(kn, KN_WIDTH - HEAD_DIM // 2, 1),
                        pltpu.roll(kn, HEAD_DIM // 2, 1))
    kr = kn * cosn + swapped * sinn
    mk = kr[:, :MOBA_WIDTH]
    mk_ref[...] = mk.astype(BF16)
    nblk = tm // MOBA_BLOCK
    kmean_ref[0] = jnp.concatenate(
        [jnp.mean(mk[c * MOBA_BLOCK:(c + 1) * MOBA_BLOCK], axis=0, keepdims=True) for c in range(nblk)],
        axis=0)
    sk_ref[...] = kr[:, MOBA_WIDTH:].astype(BF16)
    xqk_ref[...] = nat[:, KN_WIDTH:].astype(BF16)

    cost = cost_ref[...]
    sint = sint_ref[...]
    gq = jnp.concatenate([gq_ref[...]] * (tm // LANES_V7X), axis=1)
    half = HEAD_DIM // 2
    for h in range(QT_ROWS // HEAD_DIM):
        blk = tr[h * HEAD_DIM:(h + 1) * HEAD_DIM]
        is_swa = h >= MOBA_HEADS
        gain = gq[HEAD_DIM:] if is_swa else gq[:HEAD_DIM]
        msq = jnp.mean(blk * blk, axis=0, keepdims=True)
        qn = blk * lax.rsqrt(msq + NORM_EPS) * gain
        x1, x2 = qn[:half], qn[half:]
        rot = jnp.concatenate([x1 * cost - x2 * sint, x2 * cost + x1 * sint], axis=0).astype(BF16)
        if is_swa:
            r0 = (h - MOBA_HEADS) * HEAD_DIM
            sqt_ref[r0:r0 + HEAD_DIM, :] = rot
        else:
            mqt_ref[h * HEAD_DIM:(h + 1) * HEAD_DIM, :] = rot
    r = QT_ROWS
    mv = tr[r:r + MOBA_WIDTH].astype(BF16)
    for c in range(nblk):
        mvt_ref[c] = mv[:, c * MOBA_BLOCK:(c + 1) * MOBA_BLOCK]
    r += MOBA_WIDTH
    svt_ref[...] = tr[r:r + SWA_KV_WIDTH].astype(BF16)
    r += SWA_KV_WIDTH
    xvt_ref[...] = tr[r:r + MLSTM_WIDTH].astype(BF16)
    r += MLSTM_WIDTH
    xot_ref[...] = tr[r:r + MLSTM_WIDTH].astype(BF16)
    r += MLSTM_WIDTH
    gbias = jnp.concatenate([gbias_ref[...]] * (tm // LANES_V7X), axis=1)
    gt_ref[...] = tr[r:r + NUM_GATES] + gbias


def _in_proj(xf, ln1, wnat, wtr, bd, gk, cosn, sinn, gq, cost, sint, gbias, seq):
    t = xf.shape[0]
    tm = IN_PROJ_ROWS
    steps = t // tm
    seq_steps = seq // tm
    nblk = tm // MOBA_BLOCK
    row = lambda w: pl.BlockSpec((tm, w), lambda i: (i, 0))
    col = lambda r: pl.BlockSpec((r, tm), lambda i: (0, i))
    in_specs = [
        row(D_MODEL),
        _const_spec((1, D_MODEL)),
        _const_spec(wnat.shape),
        _const_spec(wtr.shape),
        _const_spec(bd.shape),
        _const_spec(gk.shape),
        pl.BlockSpec((tm, LANES_V7X), lambda i: (i % seq_steps, 0)),
        pl.BlockSpec((tm, LANES_V7X), lambda i: (i % seq_steps, 0)),
        _const_spec(gq.shape),
        pl.BlockSpec((HEAD_DIM // 2, tm), lambda i: (0, i % seq_steps)),
        pl.BlockSpec((HEAD_DIM // 2, tm), lambda i: (0, i % seq_steps)),
        _const_spec(gbias.shape),
    ]
    out_shape = [
        jax.ShapeDtypeStruct((t, MOBA_WIDTH), BF16),
        jax.ShapeDtypeStruct((steps, nblk, MOBA_WIDTH), F32),
        jax.ShapeDtypeStruct((t, SWA_KV_WIDTH), BF16),
        jax.ShapeDtypeStruct((t, 2 * MLSTM_WIDTH), BF16),
        jax.ShapeDtypeStruct((MOBA_WIDTH, t), BF16),
        jax.ShapeDtypeStruct((SWA_Q_WIDTH, t), BF16),
        jax.ShapeDtypeStruct((t // MOBA_BLOCK, MOBA_WIDTH, MOBA_BLOCK), BF16),
        jax.ShapeDtypeStruct((SWA_KV_WIDTH, t), BF16),
        jax.ShapeDtypeStruct((MLSTM_WIDTH, t), BF16),
        jax.ShapeDtypeStruct((MLSTM_WIDTH, t), BF16),
        jax.ShapeDtypeStruct((NUM_GATES, t), F32),
    ]
    out_specs = [
        row(MOBA_WIDTH),
        pl.BlockSpec((1, nblk, MOBA_WIDTH), lambda i: (i, 0, 0)),
        row(SWA_KV_WIDTH),
        row(2 * MLSTM_WIDTH),
        col(MOBA_WIDTH),
        col(SWA_Q_WIDTH),
        pl.BlockSpec((nblk, MOBA_WIDTH, MOBA_BLOCK), lambda i: (i, 0, 0)),
        col(SWA_KV_WIDTH),
        col(MLSTM_WIDTH),
        col(MLSTM_WIDTH),
        col(NUM_GATES),
    ]
    weights = 2 * (wnat.size + wtr.size)
    tiles = tm * (2 * 4 * D_MODEL + 3 * 4 * (NAT_WIDTH + TR_ROWS) + 2 * 2 * (NAT_WIDTH + TR_ROWS))
    return pl.pallas_call(
        _in_proj_kernel,
        grid=(steps,),
        in_specs=in_specs,
        out_specs=out_specs,
        out_shape=out_shape,
        compiler_params=pltpu.CompilerParams(
            dimension_semantics=("arbitrary",),
            vmem_limit_bytes=_vmem_limit(2 * weights + tiles + (8 << 20))),
        name="in_proj",
    )(xf, ln1, wnat, wtr, bd, gk, cosn, sinn, gq, cost, sint, gbias)


def _moba_kernel(qt_ref, k_ref, vt_ref, km_ref, o_ref, sel_ref):
    nb = k_ref.shape[0]
    lq = qt_ref.shape[1]
    i = pl.program_id(2)
    qt = qt_ref[...]
    row = lax.broadcasted_iota(jnp.int32, qt.shape, 0)
    km_hi, km_lo = _split_bf16(km_ref[...])
    blk_id = lax.broadcasted_iota(jnp.int32, (nb, lq), 0)
    zero = jnp.zeros_like(qt)
    qh = []
    for hh in range(2):
        q_one = jnp.where((row >= hh * HEAD_DIM) & (row < (hh + 1) * HEAD_DIM), qt, zero)
        qh.append(q_one)
        gate = _dot(km_hi, q_one) + _dot(km_lo, q_one)
        gate = jnp.where(blk_id < i, gate, NEG_INF)
        sel = jnp.zeros((nb, lq), dtype=jnp.bool_)
        for _ in range(MOBA_TOPK):
            mx = jnp.max(gate, axis=0, keepdims=True)
            is_mx = (gate == mx) & (mx > NEG_INF)
            idx = jnp.min(jnp.where(is_mx, blk_id, nb), axis=0, keepdims=True)
            pick = blk_id == idx
            sel = sel | pick
            gate = jnp.where(pick, NEG_INF, gate)
        sel_ref[hh] = jnp.where(sel, 0.0, NEG_INF).astype(F32)

    kd = k_ref[i]
    vd = vt_ref[i]
    kpos = lax.broadcasted_iota(jnp.int32, (lq, lq), 0)
    qpos = lax.broadcasted_iota(jnp.int32, (lq, lq), 1)
    causal = kpos <= qpos
    carry = []
    for hh in range(2):
        st = jnp.where(causal, _dot(kd, qh[hh]), NEG_INF)
        m = jnp.max(st, axis=0, keepdims=True)
        p = jnp.exp(st - m)
        l = jnp.sum(p, axis=0, keepdims=True)
        acc = _dot(vd[hh * HEAD_DIM:(hh + 1) * HEAD_DIM, :], p.astype(BF16))
        carry += [m, l, acc]

    def body(j, carry):
        kj = k_ref[j]
        vj = vt_ref[j]
        out = []
        for hh in range(2):
            m, l, acc = carry[3 * hh:3 * hh + 3]
            st = _dot(kj, qh[hh]) + sel_ref[hh, pl.ds(j, 1), :]
            m_new = jnp.maximum(m, jnp.max(st, axis=0, keepdims=True))
            p = jnp.exp(st - m_new)
            corr = jnp.exp(m - m_new)
            l = corr * l + jnp.sum(p, axis=0, keepdims=True)
            acc = corr * acc + _dot(vj[hh * HEAD_DIM:(hh + 1) * HEAD_DIM, :], p.astype(BF16))
            out += [m_new, l, acc]
        return tuple(out)

    carry = lax.fori_loop(0, i, body, tuple(carry))
    ot = jnp.concatenate([carry[2] / carry[1], carry[5] / carry[4]], axis=0)
    o_ref[...] = ot.T.astype(BF16)


def _moba(mqt, mk, mvt, kmean, batch, seq):
    t = mk.shape[0]
    nb = seq // MOBA_BLOCK
    lq = MOBA_BLOCK
    pairs = MOBA_WIDTH // HEAD_PAIR
    k3 = mk.reshape(t // MOBA_BLOCK, MOBA_BLOCK, MOBA_WIDTH)
    kv_bytes = 2 * 2 * 2 * nb * MOBA_BLOCK * HEAD_PAIR
    return pl.pallas_call(
        _moba_kernel,
        grid=(batch, pairs, nb),
        in_specs=[
            pl.BlockSpec((HEAD_PAIR, lq), lambda b, p, i: (p, b * nb + i)),
            pl.BlockSpec((nb, MOBA_BLOCK, HEAD_PAIR), lambda b, p, i: (b, 0, p)),
            pl.BlockSpec((nb, HEAD_PAIR, MOBA_BLOCK), lambda b, p, i: (b, p, 0)),
            pl.BlockSpec((nb, HEAD_PAIR), lambda b, p, i: (b, p)),
        ],
        out_specs=pl.BlockSpec((lq, HEAD_PAIR), lambda b, p, i: (b * nb + i, p)),
        out_shape=jax.ShapeDtypeStruct((t, MOBA_WIDTH), BF16),
        scratch_shapes=[pltpu.VMEM((2, nb, lq), F32)],
        compiler_params=pltpu.CompilerParams(
            dimension_semantics=("arbitrary", "arbitrary", "arbitrary"),
            vmem_limit_bytes=_vmem_limit(kv_bytes + (16 << 20))),
        name="moba",
    )(mqt, k3, mvt, kmean)


def _swa_kernel(sink_ref, qt_ref, k_ref, kh_ref, vt_ref, vth_ref, o_ref):
    tq = k_ref.shape[0]
    w = SWA_WINDOW
    i = pl.program_id(1)
    ncol = SWA_GROUP * w
    kpos = lax.broadcasted_iota(jnp.int32, (2 * w, ncol), 0)
    col = lax.broadcasted_iota(jnp.int32, (2 * w, ncol), 1)
    diff = w + (col % w) - kpos
    in_window = (diff >= 0) & (diff < w)
    zeros = jnp.zeros((HEAD_DIM, ncol), BF16)
    for r in range(tq // w):
        if r == 0:
            kcat = jnp.concatenate([kh_ref[...], k_ref[0:w, :]], axis=0)
            vcat = jnp.concatenate([vth_ref[...], vt_ref[:, 0:w]], axis=1)
            mask = in_window & ((kpos >= w) | (i > 0))
        else:
            kcat = k_ref[(r - 1) * w:(r + 1) * w, :]
            vcat = vt_ref[:, (r - 1) * w:(r + 1) * w]
            mask = in_window
        pieces = []
        for g in range(SWA_KV_HEADS):
            qs = jnp.concatenate(
                [qt_ref[(SWA_GROUP * g + a) * HEAD_DIM:(SWA_GROUP * g + a + 1) * HEAD_DIM, r * w:(r + 1) * w]
                 for a in range(SWA_GROUP)], axis=1)
            qpad = jnp.concatenate([qs, zeros] if g == 0 else [zeros, qs], axis=0)
            st = jnp.where(mask, _dot(kcat, qpad), NEG_INF)
            sink = jnp.concatenate(
                [jnp.full((1, w), sink_ref[SWA_GROUP * g + a], F32) for a in range(SWA_GROUP)], axis=1)
            m = jnp.maximum(jnp.max(st, axis=0, keepdims=True), sink)
            p = jnp.exp(st - m)
            l = jnp.sum(p, axis=0, keepdims=True) + jnp.exp(sink - m)
            ot = _dot(vcat[g * HEAD_DIM:(g + 1) * HEAD_DIM, :], p.astype(BF16)) / l
            pieces += [ot[:, a * w:(a + 1) * w] for a in range(SWA_GROUP)]
        o_ref[r * w:(r + 1) * w, :] = jnp.concatenate(pieces, axis=0).T.astype(BF16)


def _swa(sinks, sqt, sk, svt, batch, seq):
    t = sk.shape[0]
    tq = SWA_ROWS
    w = SWA_WINDOW
    steps = seq // tq
    halo = lambda b, i: jnp.maximum((b * seq + i * tq) // w - 1, 0)
    return pl.pallas_call(
        _swa_kernel,
        grid=(batch, steps),
        in_specs=[
            pl.BlockSpec(memory_space=pltpu.SMEM),
            pl.BlockSpec((SWA_Q_WIDTH, tq), lambda b, i: (0, b * steps + i)),
            pl.BlockSpec((tq, SWA_KV_WIDTH), lambda b, i: (b * steps + i, 0)),
            pl.BlockSpec((w, SWA_KV_WIDTH), lambda b, i: (halo(b, i), 0)),
            pl.BlockSpec((SWA_KV_WIDTH, tq), lambda b, i: (0, b * steps + i)),
            pl.BlockSpec((SWA_KV_WIDTH, w), lambda b, i: (0, halo(b, i))),
        ],
        out_specs=pl.BlockSpec((tq, SWA_Q_WIDTH), lambda b, i: (b * steps + i, 0)),
        out_shape=jax.ShapeDtypeStruct((t, SWA_Q_WIDTH), BF16),
        compiler_params=pltpu.CompilerParams(dimension_semantics=("arbitrary", "arbitrary")),
        name="swa",
    )(sinks, sqt, sk, sk, svt, svt)


def _log_sigmoid(v):
    return jnp.minimum(v, 0.0) - jnp.log1p(jnp.exp(-jnp.abs(v)))


def _mlstm_kernel(xqk_ref, halo_ref, xvt_ref, xot_ref, gt_ref, gc_ref, convw_ref, convb_ref,
                  gain_ref, cum_ref, cumt_ref, o_ref, ext_ref, c_ref, n_ref, m_ref):
    tt = xqk_ref.shape[0]
    lc = MLSTM_CHUNK
    i = pl.program_id(1)

    @pl.when(i == 0)
    def _():
        c_ref[...] = jnp.zeros_like(c_ref)
        n_ref[...] = jnp.zeros_like(n_ref)
        m_ref[...] = jnp.zeros_like(m_ref)

    hist = halo_ref[...].astype(F32)
    ext_ref[0:CONV_HALO, :] = jnp.where(i > 0, hist, jnp.zeros_like(hist))
    ext_ref[CONV_HALO:CONV_HALO + tt, :] = xqk_ref[...].astype(F32)
    convw = convw_ref[...]
    conv = jnp.zeros((tt, 2 * MLSTM_WIDTH), F32) + convb_ref[...]
    for j in range(CONV_WIDTH):
        start = CONV_HALO - (CONV_WIDTH - 1) + j
        conv = conv + convw[j:j + 1, :] * ext_ref[start:start + tt, :]
    qk = conv * jax.nn.sigmoid(conv)
    q_all = qk[:, :MLSTM_WIDTH]
    k_all = qk[:, MLSTM_WIDTH:] * SM_SCALE

    g_row = gt_ref[...]
    g_col = gc_ref[...]
    cum = cum_ref[...]
    b_row = jnp.dot(_log_sigmoid(g_row), cum, preferred_element_type=F32,
                    precision=lax.Precision.HIGHEST)
    b_col = jnp.dot(cumt_ref[...], _log_sigmoid(g_col), preferred_element_type=F32,
                    precision=lax.Precision.HIGHEST)
    u_col = b_col[:, MLSTM_HEADS:] - g_col[:, :MLSTM_HEADS]

    lane = lax.broadcasted_iota(jnp.int32, (lc, HEAD_PAIR), 1)
    s_idx = lax.broadcasted_iota(jnp.int32, (lc, lc), 0)
    t_idx = lax.broadcasted_iota(jnp.int32, (lc, lc), 1)
    causal = s_idx <= t_idx
    ht_rows = []
    for h in range(MLSTM_HEADS):
        p, hh = divmod(h, 2)
        in_head = (lane >= hh * HEAD_DIM) & (lane < (hh + 1) * HEAD_DIM)
        c_state = c_ref[h]
        n_state = n_ref[h]
        m_state = m_ref[h][0:1, 0:lc]
        ht_chunks = []
        for c in range(tt // lc):
            sl = slice(c * lc, (c + 1) * lc)
            q_pair = q_all[sl, p * HEAD_PAIR:(p + 1) * HEAD_PAIR]
            qm = jnp.where(in_head, q_pair, 0.0).astype(BF16)
            kp = k_all[sl, p * HEAD_PAIR:(p + 1) * HEAD_PAIR].astype(BF16)
            vt = xvt_ref[h * HEAD_DIM:(h + 1) * HEAD_DIM, sl]
            br = b_row[MLSTM_HEADS + h:MLSTM_HEADS + h + 1, sl]
            ir = g_row[h:h + 1, sl]
            uc = u_col[sl, h:h + 1]
            a = br[:, lc - 1:lc]
            dt = jnp.where(causal, br - uc, NEG_INF)
            inter = br + m_state
            m_t = jnp.maximum(inter, jnp.max(dt, axis=0, keepdims=True))
            w_intra = jnp.exp(dt - m_t)
            w_inter = jnp.exp(inter - m_t)
            qkt = _dot_nt(kp, qm) * w_intra
            num = w_inter * _dot_nt(c_state.astype(BF16), qm) + _dot(vt, qkt.astype(BF16))
            nq = _dot_nt(n_state.astype(BF16), qm)[0:1, :]
            den = w_inter * nq + jnp.sum(qkt, axis=0, keepdims=True)
            ht_chunks.append(num / jnp.maximum(jnp.abs(den), jnp.exp(-m_t)))
            g_end = a - br + ir
            m_new = jnp.maximum(a + m_state, jnp.max(g_end, axis=1, keepdims=True))
            w_s = jnp.exp(g_end - m_new)
            decay = jnp.exp(a + m_state - m_new)
            vtw = (vt.astype(F32) * w_s).astype(BF16)
            decay_col = decay[:, 0:1]
            c_state = decay_col * c_state + jnp.where(in_head, _dot(vtw, kp), 0.0)
            w_rows = jnp.broadcast_to(w_s, (8, lc)).astype(BF16)
            n_state = decay_col * n_state + jnp.where(in_head[0:8], _dot(w_rows, kp), 0.0)
            m_state = m_new + jnp.zeros_like(m_state)
        c_ref[h] = c_state
        n_ref[h] = n_state
        m_ref[h] = jnp.broadcast_to(m_state[:, 0:1], m_ref.shape[1:])
        ht_rows.append(jnp.concatenate(ht_chunks, axis=1))

    gain = jnp.concatenate([gain_ref[...]] * (tt // LANES_V7X), axis=1)
    outs = []
    for h in range(MLSTM_HEADS):
        rows = slice(h * HEAD_DIM, (h + 1) * HEAD_DIM)
        hg = ht_rows[h] * jax.nn.sigmoid(xot_ref[rows, :].astype(F32))
        msq = jnp.mean(hg * hg, axis=0, keepdims=True)
        outs.append(hg * lax.rsqrt(msq + NORM_EPS) * gain[rows])
    o_ref[...] = jnp.concatenate(outs, axis=0).T.astype(BF16)


def _mlstm(xqk, xvt, xot, gt, gc, convw, convb, gain, cum, batch, seq):
    t = xqk.shape[0]
    tt = MLSTM_ROWS
    steps = seq // tt
    col = lambda r: pl.BlockSpec((r, tt), lambda b, i: (0, b * steps + i))
    return pl.pallas_call(
        _mlstm_kernel,
        grid=(batch, steps),
        in_specs=[
            pl.BlockSpec((tt, 2 * MLSTM_WIDTH), lambda b, i: (b * steps + i, 0)),
            pl.BlockSpec((CONV_HALO, 2 * MLSTM_WIDTH),
                         lambda b, i: (jnp.maximum((b * seq + i * tt) // CONV_HALO - 1, 0), 0)),
            col(MLSTM_WIDTH),
            col(MLSTM_WIDTH),
            col(NUM_GATES),
            pl.BlockSpec((tt, NUM_GATES), lambda b, i: (b * steps + i, 0)),
            _const_spec(convw.shape),
            _const_spec(convb.shape),
            _const_spec(gain.shape),
            _const_spec(cum.shape),
            _const_spec(cum.shape),
        ],
        out_specs=pl.BlockSpec((tt, MLSTM_WIDTH), lambda b, i: (b * steps + i, 0)),
        out_shape=jax.ShapeDtypeStruct((t, MLSTM_WIDTH), BF16),
        scratch_shapes=[
            pltpu.VMEM((CONV_HALO + tt, 2 * MLSTM_WIDTH), F32),
            pltpu.VMEM((MLSTM_HEADS, HEAD_DIM, HEAD_PAIR), F32),
            pltpu.VMEM((MLSTM_HEADS, 8, HEAD_PAIR), F32),
            pltpu.VMEM((MLSTM_HEADS, 8, HEAD_PAIR), F32),
        ],
        compiler_params=pltpu.CompilerParams(dimension_semantics=("arbitrary", "arbitrary")),
        name="mlstm",
    )(xqk, xqk, xvt, xot, gt, gc, convw, convb, gain, cum, cum.T)


def _out_mlp_kernel(x_ref, ym_ref, yl_ref, ys_ref, wom_ref, wol_ref, wos_ref, ln2_ref,
                    wup_ref, wdn_ref, o_ref):
    x1 = (x_ref[...] + _dot(ym_ref[...], wom_ref[...]) + _dot(yl_ref[...], wol_ref[...])
          + _dot(ys_ref[...], wos_ref[...]))
    ms = jnp.mean(x1 * x1, axis=-1, keepdims=True)
    hn = (x1 * lax.rsqrt(ms + NORM_EPS) * ln2_ref[...]).astype(BF16)
    o_ref[...] = x1
    for c in range(0, D_FF, MLP_FF_CHUNK):
        u = _dot(hn, wup_ref[:, c:c + MLP_FF_CHUNK])
        act = jnp.square(jnp.maximum(u, 0.0)).astype(BF16)
        o_ref[...] += _dot(act, wdn_ref[c:c + MLP_FF_CHUNK, :])


def _out_mlp(xf, ym, yl, ys, wom, wol, wos, ln2, wup, wdn):
    t = xf.shape[0]
    tm = OUT_MLP_ROWS
    row = lambda w: pl.BlockSpec((tm, w), lambda i: (i, 0))
    weights = 2 * (wom.size + wol.size + wos.size + wup.size + wdn.size)
    tiles = tm * (4 * 4 * D_MODEL + 2 * 2 * D_MODEL + 4 * 4 * D_MODEL + 6 * MLP_FF_CHUNK)
    return pl.pallas_call(
        _out_mlp_kernel,
        grid=(t // tm,),
        in_specs=[
            row(D_MODEL), row(MOBA_WIDTH), row(MLSTM_WIDTH), row(SWA_Q_WIDTH),
            _const_spec(wom.shape), _const_spec(wol.shape), _const_spec(wos.shape),
            _const_spec(ln2.shape), _const_spec(wup.shape), _const_spec(wdn.shape),
        ],
        out_specs=row(D_MODEL),
        out_shape=jax.ShapeDtypeStruct((t, D_MODEL), F32),
        compiler_params=pltpu.CompilerParams(
            dimension_semantics=("arbitrary",),
            vmem_limit_bytes=_vmem_limit(weights + tiles + (8 << 20))),
        name="out_mlp",
    )(xf, ym, yl, ys, wom, wol, wos, ln2, wup, wdn)


def _rope_tables(seq):
    inv = ROPE_THETA ** (-jnp.arange(0, HEAD_DIM, 2, dtype=F32) / HEAD_DIM)
    ang = jnp.arange(seq, dtype=F32)[:, None] * inv[None, :]
    cos, sin = jnp.cos(ang), jnp.sin(ang)
    cosn = jnp.concatenate([cos, cos, cos, cos], axis=1)
    sinn = jnp.concatenate([-sin, sin, -sin, sin], axis=1)
    return cosn, sinn, cos.T, sin.T


def _layer(xf, tables, consts, batch, seq, ln1, w_in, conv_w, conv_b, igate_b, fgate_b, mlstm_norm,
           moba_q_norm, moba_k_norm, swa_q_norm, swa_k_norm, swa_sinks, w_out, ln2, w_up, w_down):
    cosn, sinn, cost, sint = tables
    bd, cum = consts
    o = 0
    cols = {}
    for name, width in (("mq", MOBA_WIDTH), ("mk", MOBA_WIDTH), ("mv", MOBA_WIDTH), ("sq", SWA_Q_WIDTH),
                        ("sk", SWA_KV_WIDTH), ("sv", SWA_KV_WIDTH), ("xqk", 2 * MLSTM_WIDTH),
                        ("xv", MLSTM_WIDTH), ("xo", MLSTM_WIDTH), ("xi", MLSTM_HEADS), ("xf", MLSTM_HEADS)):
        cols[name] = w_in[:, o:o + width]
        o += width
    wnat = jnp.concatenate([cols["mk"], cols["sk"], cols["xqk"]], axis=1).astype(BF16)
    wtr = jnp.concatenate([cols["mq"], cols["sq"], cols["mv"], cols["sv"], cols["xv"], cols["xo"],
                           cols["xi"], cols["xf"]], axis=1).T.astype(BF16)
    gk = jnp.concatenate([jnp.tile(moba_k_norm, MOBA_HEADS), jnp.tile(swa_k_norm, SWA_KV_HEADS)])[None, :]
    gq = jnp.broadcast_to(
        (jnp.concatenate([moba_q_norm, swa_q_norm]) * SM_SCALE)[:, None], (2 * HEAD_DIM, LANES_V7X))
    gbias = jnp.broadcast_to(jnp.concatenate([igate_b, fgate_b])[:, None], (NUM_GATES, LANES_V7X))

    (mk, kmean, sk, xqk, mqt, sqt, mvt, svt, xvt, xot, gt) = _in_proj(
        xf, ln1[None, :], wnat, wtr, bd, gk, cosn, sinn, gq, cost, sint, gbias, seq)

    kmean = kmean.reshape(-1, MOBA_WIDTH)
    ym = _moba(mqt, mk, mvt, kmean, batch, seq)
    ys = _swa(swa_sinks, sqt, sk, svt, batch, seq)
    gain = jnp.broadcast_to(mlstm_norm.reshape(MLSTM_WIDTH, 1), (MLSTM_WIDTH, LANES_V7X))
    yl = _mlstm(xqk, xvt, xot, gt, gt.T, conv_w, conv_b[None, :], gain, cum, batch, seq)

    wo = w_out.astype(BF16)
    return _out_mlp(xf, ym, yl, ys, wo[:MOBA_WIDTH], wo[MOBA_WIDTH:MOBA_WIDTH + MLSTM_WIDTH],
                    wo[MOBA_WIDTH + MLSTM_WIDTH:], ln2[None, :], w_up.astype(BF16), w_down.astype(BF16))


def kernel(x, ln1, w_in, conv_w, conv_b, igate_b, fgate_b, mlstm_norm, moba_q_norm, moba_k_norm,
           swa_q_norm, swa_k_norm, swa_sinks, w_out, ln2, w_up, w_down):
    batch, seq, d = x.shape
    assert d == D_MODEL and seq % max(IN_PROJ_ROWS, SWA_ROWS, MLSTM_ROWS, MOBA_BLOCK) == 0
    depth = ln1.shape[0]
    tables = _rope_tables(seq)
    bw = 2 * LANES_V7X
    ids = jnp.arange(bw) // HEAD_DIM
    bd = jnp.where(ids[:, None] == ids[None, :], 1.0 / HEAD_DIM, 0.0).astype(BF16)
    tids = jnp.arange(MLSTM_ROWS)
    cum = ((tids[:, None] // MLSTM_CHUNK == tids[None, :] // MLSTM_CHUNK)
           & (tids[:, None] <= tids[None, :])).astype(F32)
    xf = x.reshape(batch * seq, d)
    for l in range(depth):
        xf = _layer(xf, tables, (bd, cum), batch, seq, ln1[l], w_in[l], conv_w[l], conv_b[l], igate_b[l],
                    fgate_b[l], mlstm_norm[l], moba_q_norm[l], moba_k_norm[l], swa_q_norm[l],
                    swa_k_norm[l], swa_sinks[l], w_out[l], ln2[l], w_up[l], w_down[l])
    return xf.reshape(batch, seq, d)
```

```python
import functools

import jax
import jax.numpy as jnp
from jax import lax
from jax.experimental import pallas as pl
from jax.experimental.pallas import tpu as pltpu

F32 = jnp.float32
BF16 = jnp.bfloat16
NEG_INF = float("-inf")

D_MODEL = 1024
HEAD_DIM = 64
MOBA_HEADS = 6
MLSTM_HEADS = 4
SWA_Q_HEADS = 6
SWA_KV_HEADS = 2
SWA_GROUP = SWA_Q_HEADS // SWA_KV_HEADS
MOBA_WIDTH = MOBA_HEADS * HEAD_DIM
MLSTM_WIDTH = MLSTM_HEADS * HEAD_DIM
SWA_Q_WIDTH = SWA_Q_HEADS * HEAD_DIM
SWA_KV_WIDTH = SWA_KV_HEADS * HEAD_DIM
MOBA_BLOCK = 256
MOBA_TOPK = 3
MLSTM_CHUNK = 64
CONV_WIDTH = 4
SWA_WINDOW = 128
ROPE_THETA = 10000.0
D_FF = 4 * D_MODEL
NORM_EPS = 1e-6
SM_SCALE = HEAD_DIM ** -0.5
LOG2E = 1.4426950408889634
MOBA_V_ROWS = HEAD_DIM + 16

LANES_V7X = 128
VMEM_BYTES_V7X = 64 * 1024 * 1024
HEAD_PAIR = 2 * HEAD_DIM
assert HEAD_PAIR == LANES_V7X

NAT_WIDTH = MOBA_WIDTH + SWA_KV_WIDTH + 2 * MLSTM_WIDTH
KN_WIDTH = MOBA_WIDTH + SWA_KV_WIDTH
QT_ROWS = MOBA_WIDTH + SWA_Q_WIDTH
TR_ROWS = QT_ROWS + MOBA_WIDTH + SWA_KV_WIDTH + 2 * MLSTM_WIDTH + 2 * MLSTM_HEADS
NUM_GATES = 2 * MLSTM_HEADS

IN_PROJ_ROWS = 512
OUT_MLP_ROWS = 512
MLP_FF_CHUNK = 1024
SWA_ROWS = 512
MLSTM_ROWS = 256
MOBA_GROUP = 2
CONV_HALO = 16


def _dot(a, b):
    return jnp.dot(a, b, preferred_element_type=F32)


def _dot_nt(a, b):
    return lax.dot_general(a, b, (((1,), (1,)), ((), ())), preferred_element_type=F32)


def _split_bf16(v):
    hi = v.astype(BF16)
    lo = (v - hi.astype(F32)).astype(BF16)
    return hi, lo


def _vmem_limit(nbytes):
    return int(min(nbytes, VMEM_BYTES_V7X - 4 * 1024 * 1024))


def _const_spec(shape):
    nd = len(shape)
    return pl.BlockSpec(shape, lambda *_: (0,) * nd, pipeline_mode=pl.Buffered(1))


def _in_proj_kernel(x_ref, ln1_ref, wnat_ref, wtr_ref, bd_ref, gk_ref, cosn_ref, sinn_ref,
                    gq_ref, cost_ref, sint_ref, gbias_ref,
                    mk_ref, kmean_ref, sk_ref, xqk_ref, mqt_ref, sqt_ref, mvt_ref, svt_ref,
                    xvt_ref, xot_ref, gt_ref):
    tm = x_ref.shape[0]
    x = x_ref[...]
    ms = jnp.mean(x * x, axis=-1, keepdims=True)
    hn = (x * lax.rsqrt(ms + NORM_EPS) * ln1_ref[...]).astype(BF16)
    nat = _dot(hn, wnat_ref[...])
    tr = _dot_nt(wtr_ref[...], hn)

    kk = nat[:, :KN_WIDTH]
    hi, lo = _split_bf16(kk * kk)
    bd = bd_ref[...]
    bw = bd.shape[0]
    msk = jnp.concatenate(
        [_dot(hi[:, c:c + bw], bd) + _dot(lo[:, c:c + bw], bd) for c in range(0, KN_WIDTH, bw)],
        axis=1)
    kn = kk * lax.rsqrt(msk + NORM_EPS) * gk_ref[...]
    reps = KN_WIDTH // LANES_V7X
    cosn = jnp.concatenate([cosn_ref[...]] * reps, axis=1)
    sinn = jnp.concatenate([sinn_ref[...]] * reps, axis=1)
    lane = lax.broadcasted_iota(jnp.int32, kn.shape, 1)
    first_half = (lane % HEAD_DIM) < (HEAD_DIM // 2)
    swapped = jnp.where(first_half,
                        pltpu.roll(kn, KN_WIDTH - HEAD_DIM // 2, 1),
                        pltpu.roll(kn, HEAD_DIM // 2, 1))
    kr = kn * cosn + swapped * sinn
    mk = kr[:, :MOBA_WIDTH]
    mk_ref[...] = mk.astype(BF16)
    nblk = tm // MOBA_BLOCK
    kmean_ref[0] = jnp.concatenate(
        [jnp.mean(mk[c * MOBA_BLOCK:(c + 1) * MOBA_BLOCK], axis=0, keepdims=True) for c in range(nblk)],
        axis=0)
    sk_ref[...] = kr[:, MOBA_WIDTH:].astype(BF16)
    xqk_ref[...] = nat[:, KN_WIDTH:].astype(BF16)

    cost = cost_ref[...]
    sint = sint_ref[...]
    gq = jnp.concatenate([gq_ref[...]] * (tm // LANES_V7X), axis=1)
    half = HEAD_DIM // 2
    for h in range(QT_ROWS // HEAD_DIM):
        blk = tr[h * HEAD_DIM:(h + 1) * HEAD_DIM]
        is_swa = h >= MOBA_HEADS
        gain = gq[HEAD_DIM:] if is_swa else gq[:HEAD_DIM]
        msq = jnp.mean(blk * blk, axis=0, keepdims=True)
        qn = blk * lax.rsqrt(msq + NORM_EPS) * gain
        x1, x2 = qn[:half], qn[half:]
        rot = jnp.concatenate([x1 * cost - x2 * sint, x2 * cost + x1 * sint], axis=0).astype(BF16)
        if is_swa:
            r0 = (h - MOBA_HEADS) * HEAD_DIM
            sqt_ref[r0:r0 + HEAD_DIM, :] = rot
        else:
            mqt_ref[h * HEAD_DIM:(h + 1) * HEAD_DIM, :] = rot
    r = QT_ROWS
    mv = tr[r:r + MOBA_WIDTH].astype(BF16)
    ones = jnp.ones((MOBA_V_ROWS - HEAD_DIM, MOBA_BLOCK), BF16)
    for c in range(nblk):
        for h in range(MOBA_HEADS):
            r0 = h * MOBA_V_ROWS
            mvt_ref[c, r0:r0 + HEAD_DIM, :] = mv[h * HEAD_DIM:(h + 1) * HEAD_DIM,
                                                 c * MOBA_BLOCK:(c + 1) * MOBA_BLOCK]
            mvt_ref[c, r0 + HEAD_DIM:r0 + MOBA_V_ROWS, :] = ones
    r += MOBA_WIDTH
    svt_ref[...] = tr[r:r + SWA_KV_WIDTH].astype(BF16)
    r += SWA_KV_WIDTH
    xvt_ref[...] = tr[r:r + MLSTM_WIDTH].astype(BF16)
    r += MLSTM_WIDTH
    xot_ref[...] = tr[r:r + MLSTM_WIDTH].astype(BF16)
    r += MLSTM_WIDTH
    gbias = jnp.concatenate([gbias_ref[...]] * (tm // LANES_V7X), axis=1)
    gt_ref[...] = tr[r:r + NUM_GATES] + gbias


def _in_proj(xf, ln1, wnat, wtr, bd, gk, cosn, sinn, gq, cost, sint, gbias, seq):
    t = xf.shape[0]
    tm = IN_PROJ_ROWS
    steps = t // tm
    seq_steps = seq // tm
    nblk = tm // MOBA_BLOCK
    row = lambda w: pl.BlockSpec((tm, w), lambda i: (i, 0))
    col = lambda r: pl.BlockSpec((r, tm), lambda i: (0, i))
    in_specs = [
        row(D_MODEL),
        _const_spec((1, D_MODEL)),
        _const_spec(wnat.shape),
        _const_spec(wtr.shape),
        _const_spec(bd.shape),
        _const_spec(gk.shape),
        pl.BlockSpec((tm, LANES_V7X), lambda i: (i % seq_steps, 0)),
        pl.BlockSpec((tm, LANES_V7X), lambda i: (i % seq_steps, 0)),
        _const_spec(gq.shape),
        pl.BlockSpec((HEAD_DIM // 2, tm), lambda i: (0, i % seq_steps)),
        pl.BlockSpec((HEAD_DIM // 2, tm), lambda i: (0, i % seq_steps)),
        _const_spec(gbias.shape),
    ]
    out_shape = [
        jax.ShapeDtypeStruct((t, MOBA_WIDTH), BF16),
        jax.ShapeDtypeStruct((steps, nblk, MOBA_WIDTH), F32),
        jax.ShapeDtypeStruct((t, SWA_KV_WIDTH), BF16),
        jax.ShapeDtypeStruct((t, 2 * MLSTM_WIDTH), BF16),
        jax.ShapeDtypeStruct((MOBA_WIDTH, t), BF16),
        jax.ShapeDtypeStruct((SWA_Q_WIDTH, t), BF16),
        jax.ShapeDtypeStruct((t // MOBA_BLOCK, MOBA_HEADS * MOBA_V_ROWS, MOBA_BLOCK), BF16),
        jax.ShapeDtypeStruct((SWA_KV_WIDTH, t), BF16),
        jax.ShapeDtypeStruct((MLSTM_WIDTH, t), BF16),
        jax.ShapeDtypeStruct((MLSTM_WIDTH, t), BF16),
        jax.ShapeDtypeStruct((NUM_GATES, t), F32),
    ]
    out_specs = [
        row(MOBA_WIDTH),
        pl.BlockSpec((1, nblk, MOBA_WIDTH), lambda i: (i, 0, 0)),
        row(SWA_KV_WIDTH),
        row(2 * MLSTM_WIDTH),
        col(MOBA_WIDTH),
        col(SWA_Q_WIDTH),
        pl.BlockSpec((nblk, MOBA_HEADS * MOBA_V_ROWS, MOBA_BLOCK), lambda i: (i, 0, 0)),
        col(SWA_KV_WIDTH),
        col(MLSTM_WIDTH),
        col(MLSTM_WIDTH),
        col(NUM_GATES),
    ]
    weights = 2 * (wnat.size + wtr.size)
    tiles = tm * (2 * 4 * D_MODEL + 3 * 4 * (NAT_WIDTH + TR_ROWS) + 2 * 2 * (NAT_WIDTH + TR_ROWS))
    return pl.pallas_call(
        _in_proj_kernel,
        grid=(steps,),
        in_specs=in_specs,
        out_specs=out_specs,
        out_shape=out_shape,
        compiler_params=pltpu.CompilerParams(
            dimension_semantics=("arbitrary",),
            vmem_limit_bytes=_vmem_limit(2 * weights + tiles + (8 << 20))),
        name="in_proj",
    )(xf, ln1, wnat, wtr, bd, gk, cosn, sinn, gq, cost, sint, gbias)


def _moba_kernel(qt_ref, k_ref, vt_ref, km_ref, o_ref, sel_ref, s_ref):
    nb = k_ref.shape[0]
    lq = qt_ref.shape[1]
    i = pl.program_id(2)
    qt = qt_ref[...]
    row = lax.broadcasted_iota(jnp.int32, qt.shape, 0)
    km_hi, km_lo = _split_bf16(km_ref[...])
    blk_id = lax.broadcasted_iota(jnp.int32, (nb, lq), 0)
    zero = jnp.zeros_like(qt)
    qh = []
    for hh in range(2):
        q_one = jnp.where((row >= hh * HEAD_DIM) & (row < (hh + 1) * HEAD_DIM), qt, zero)
        qh.append(q_one)
        gate = _dot(km_hi, q_one) + _dot(km_lo, q_one)
        gate = jnp.where(blk_id < i, gate, NEG_INF)
        sel = jnp.zeros((nb, lq), dtype=jnp.bool_)
        for _ in range(MOBA_TOPK):
            mx = jnp.max(gate, axis=0, keepdims=True)
            is_mx = (gate == mx) & (mx > NEG_INF)
            idx = jnp.min(jnp.where(is_mx, blk_id, nb), axis=0, keepdims=True)
            pick = blk_id == idx
            sel = sel | pick
            gate = jnp.where(pick, NEG_INF, gate)
        sel_ref[hh] = jnp.where(sel, 0.0, NEG_INF).astype(F32)

    kd = k_ref[i]
    vd = vt_ref[i]
    kpos = lax.broadcasted_iota(jnp.int32, (lq, lq), 0)
    qpos = lax.broadcasted_iota(jnp.int32, (lq, lq), 1)
    causal = kpos <= qpos

    def weighted_values(st, mu, v_aug):
        pv = _dot(v_aug, jnp.exp2(st - mu).astype(BF16))
        return pv[HEAD_DIM:HEAD_DIM + 1], pv[:HEAD_DIM]

    carry = []
    for hh in range(2):
        st = jnp.where(causal, _dot(kd, qh[hh]), NEG_INF)
        mu = jnp.max(st, axis=0, keepdims=True)
        carry += [mu, *weighted_values(st, mu, vd[hh * MOBA_V_ROWS:(hh + 1) * MOBA_V_ROWS, :])]

    units = 2 * MOBA_GROUP

    def stage_a(blk0, slot0):
        mus = []
        for u in range(MOBA_GROUP):
            kj = k_ref[jnp.minimum(blk0 + u, nb - 1)]
            for hh in range(2):
                st = _dot(kj, qh[hh])
                s_ref[slot0 + 2 * u + hh] = st
                mus.append(jnp.max(st, axis=0, keepdims=True))
        return mus

    def stage_b(blk0, slot0, mus, state):
        state = list(state)
        for u in range(MOBA_GROUP):
            j = blk0 + u
            vj = vt_ref[j]
            for hh in range(2):
                mu = mus[2 * u + hh]
                ls, pv = weighted_values(s_ref[slot0 + 2 * u + hh], mu,
                                         vj[hh * MOBA_V_ROWS:(hh + 1) * MOBA_V_ROWS, :])
                m, l, acc = state[3 * hh:3 * hh + 3]
                mu_sel = mu + sel_ref[hh, pl.ds(j, 1), :]
                m_new = jnp.maximum(m, mu_sel)
                f = jnp.exp2(mu_sel - m_new)
                c = jnp.exp2(m - m_new)
                state[3 * hh:3 * hh + 3] = [m_new, c * l + f * ls, c * acc + f * pv]
        return state

    def body(t, loop_carry):
        mus0, state = loop_carry[:units], loop_carry[units:]
        blk = t * (2 * MOBA_GROUP)
        mus1 = stage_a(blk + MOBA_GROUP, units)
        state = stage_b(blk, 0, mus0, state)
        mus0 = stage_a(blk + 2 * MOBA_GROUP, 0)
        state = stage_b(blk + MOBA_GROUP, units, mus1, state)
        return (*mus0, *state)

    trips = lax.shift_right_logical(i + (2 * MOBA_GROUP - 1), MOBA_GROUP.bit_length())
    carry = lax.fori_loop(0, trips, body, (*stage_a(0, 0), *carry))[units:]
    ot = jnp.concatenate([carry[2] / carry[1], carry[5] / carry[4]], axis=0)
    o_ref[...] = ot.T.astype(BF16)


def _moba(mqt, mk, mvt, kmean, batch, seq):
    t = mk.shape[0]
    nb = seq // MOBA_BLOCK
    lq = MOBA_BLOCK
    pairs = MOBA_WIDTH // HEAD_PAIR
    k3 = mk.reshape(t // MOBA_BLOCK, MOBA_BLOCK, MOBA_WIDTH)
    assert nb % (2 * MOBA_GROUP) == 0
    kv_bytes = 2 * 2 * nb * MOBA_BLOCK * (HEAD_PAIR + 2 * MOBA_V_ROWS)
    return pl.pallas_call(
        _moba_kernel,
        grid=(batch, pairs, nb),
        in_specs=[
            pl.BlockSpec((HEAD_PAIR, lq), lambda b, p, i: (p, b * nb + i)),
            pl.BlockSpec((nb, MOBA_BLOCK, HEAD_PAIR), lambda b, p, i: (b, 0, p)),
            pl.BlockSpec((nb, 2 * MOBA_V_ROWS, MOBA_BLOCK), lambda b, p, i: (b, p, 0)),
            pl.BlockSpec((nb, HEAD_PAIR), lambda b, p, i: (b, p)),
        ],
        out_specs=pl.BlockSpec((lq, HEAD_PAIR), lambda b, p, i: (b * nb + i, p)),
        out_shape=jax.ShapeDtypeStruct((t, MOBA_WIDTH), BF16),
        scratch_shapes=[pltpu.VMEM((2, nb, lq), F32), pltpu.VMEM((4 * MOBA_GROUP, lq, lq), F32)],
        compiler_params=pltpu.CompilerParams(
            dimension_semantics=("arbitrary", "arbitrary", "arbitrary"),
            vmem_limit_bytes=_vmem_limit(kv_bytes + (16 << 20))),
        name="moba",
    )(mqt, k3, mvt, kmean)


def _swa_kernel(sink_ref, qt_ref, k_ref, kh_ref, vt_ref, vth_ref, o_ref):
    tq = k_ref.shape[0]
    w = SWA_WINDOW
    i = pl.program_id(1)
    ncol = SWA_GROUP * w
    kpos = lax.broadcasted_iota(jnp.int32, (2 * w, ncol), 0)
    col = lax.broadcasted_iota(jnp.int32, (2 * w, ncol), 1)
    diff = w + (col % w) - kpos
    in_window = (diff >= 0) & (diff < w)
    zeros = jnp.zeros((HEAD_DIM, ncol), BF16)
    for r in range(tq // w):
        if r == 0:
            kcat = jnp.concatenate([kh_ref[...], k_ref[0:w, :]], axis=0)
            vcat = jnp.concatenate([vth_ref[...], vt_ref[:, 0:w]], axis=1)
            mask = in_window & ((kpos >= w) | (i > 0))
        else:
            kcat = k_ref[(r - 1) * w:(r + 1) * w, :]
            vcat = vt_ref[:, (r - 1) * w:(r + 1) * w]
            mask = in_window
        pieces = []
        for g in range(SWA_KV_HEADS):
            qs = jnp.concatenate(
                [qt_ref[(SWA_GROUP * g + a) * HEAD_DIM:(SWA_GROUP * g + a + 1) * HEAD_DIM, r * w:(r + 1) * w]
                 for a in range(SWA_GROUP)], axis=1)
            qpad = jnp.concatenate([qs, zeros] if g == 0 else [zeros, qs], axis=0)
            st = jnp.where(mask, _dot(kcat, qpad), NEG_INF)
            sink = jnp.concatenate(
                [jnp.full((1, w), sink_ref[SWA_GROUP * g + a], F32) for a in range(SWA_GROUP)], axis=1)
            m = jnp.maximum(jnp.max(st, axis=0, keepdims=True), sink)
            p = jnp.exp(st - m)
            l = jnp.sum(p, axis=0, keepdims=True) + jnp.exp(sink - m)
            ot = _dot(vcat[g * HEAD_DIM:(g + 1) * HEAD_DIM, :], p.astype(BF16)) / l
            pieces += [ot[:, a * w:(a + 1) * w] for a in range(SWA_GROUP)]
        o_ref[r * w:(r + 1) * w, :] = jnp.concatenate(pieces, axis=0).T.astype(BF16)


def _swa(sinks, sqt, sk, svt, batch, seq):
    t = sk.shape[0]
    tq = SWA_ROWS
    w = SWA_WINDOW
    steps = seq // tq
    halo = lambda b, i: jnp.maximum((b * seq + i * tq) // w - 1, 0)
    return pl.pallas_call(
        _swa_kernel,
        grid=(batch, steps),
        in_specs=[
            pl.BlockSpec(memory_space=pltpu.SMEM),
            pl.BlockSpec((SWA_Q_WIDTH, tq), lambda b, i: (0, b * steps + i)),
            pl.BlockSpec((tq, SWA_KV_WIDTH), lambda b, i: (b * steps + i, 0)),
            pl.BlockSpec((w, SWA_KV_WIDTH), lambda b, i: (halo(b, i), 0)),
            pl.BlockSpec((SWA_KV_WIDTH, tq), lambda b, i: (0, b * steps + i)),
            pl.BlockSpec((SWA_KV_WIDTH, w), lambda b, i: (0, halo(b, i))),
        ],
        out_specs=pl.BlockSpec((tq, SWA_Q_WIDTH), lambda b, i: (b * steps + i, 0)),
        out_shape=jax.ShapeDtypeStruct((t, SWA_Q_WIDTH), BF16),
        compiler_params=pltpu.CompilerParams(dimension_semantics=("arbitrary", "arbitrary")),
        name="swa",
    )(sinks, sqt, sk, sk, svt, svt)


def _log_sigmoid(v):
    return jnp.minimum(v, 0.0) - jnp.log1p(jnp.exp(-jnp.abs(v)))


def _mlstm_kernel(xqk_ref, halo_ref, xvt_ref, xot_ref, gt_ref, gc_ref, convw_ref, convb_ref,
                  gain_ref, cum_ref, cumt_ref, o_ref, ext_ref, c_ref, n_ref, m_ref):
    tt = xqk_ref.shape[0]
    lc = MLSTM_CHUNK
    i = pl.program_id(1)

    @pl.when(i == 0)
    def _():
        c_ref[...] = jnp.zeros_like(c_ref)
        n_ref[...] = jnp.zeros_like(n_ref)
        m_ref[...] = jnp.zeros_like(m_ref)

    hist = halo_ref[...].astype(F32)
    ext_ref[0:CONV_HALO, :] = jnp.where(i > 0, hist, jnp.zeros_like(hist))
    ext_ref[CONV_HALO:CONV_HALO + tt, :] = xqk_ref[...].astype(F32)
    convw = convw_ref[...]
    conv = jnp.zeros((tt, 2 * MLSTM_WIDTH), F32) + convb_ref[...]
    for j in range(CONV_WIDTH):
        start = CONV_HALO - (CONV_WIDTH - 1) + j
        conv = conv + convw[j:j + 1, :] * ext_ref[start:start + tt, :]
    qk = conv * jax.nn.sigmoid(conv)
    q_all = qk[:, :MLSTM_WIDTH]
    k_all = qk[:, MLSTM_WIDTH:] * SM_SCALE

    g_row = gt_ref[...]
    g_col = gc_ref[...]
    cum = cum_ref[...]
    b_row = jnp.dot(_log_sigmoid(g_row), cum, preferred_element_type=F32,
                    precision=lax.Precision.HIGHEST)
    b_col = jnp.dot(cumt_ref[...], _log_sigmoid(g_col), preferred_element_type=F32,
                    precision=lax.Precision.HIGHEST)
    u_col = b_col[:, MLSTM_HEADS:] - g_col[:, :MLSTM_HEADS]

    lane = lax.broadcasted_iota(jnp.int32, (lc, HEAD_PAIR), 1)
    s_idx = lax.broadcasted_iota(jnp.int32, (lc, lc), 0)
    t_idx = lax.broadcasted_iota(jnp.int32, (lc, lc), 1)
    causal = s_idx <= t_idx
    ht_rows = []
    for h in range(MLSTM_HEADS):
        p, hh = divmod(h, 2)
        in_head = (lane >= hh * HEAD_DIM) & (lane < (hh + 1) * HEAD_DIM)
        c_state = c_ref[h]
        n_state = n_ref[h]
        m_state = m_ref[h][0:1, 0:lc]
        ht_chunks = []
        for c in range(tt // lc):
            sl = slice(c * lc, (c + 1) * lc)
            q_pair = q_all[sl, p * HEAD_PAIR:(p + 1) * HEAD_PAIR]
            qm = jnp.where(in_head, q_pair, 0.0).astype(BF16)
            kp = k_all[sl, p * HEAD_PAIR:(p + 1) * HEAD_PAIR].astype(BF16)
            vt = xvt_ref[h * HEAD_DIM:(h + 1) * HEAD_DIM, sl]
            br = b_row[MLSTM_HEADS + h:MLSTM_HEADS + h + 1, sl]
            ir = g_row[h:h + 1, sl]
            uc = u_col[sl, h:h + 1]
            a = br[:, lc - 1:lc]
            dt = jnp.where(causal, br - uc, NEG_INF)
            inter = br + m_state
            m_t = jnp.maximum(inter, jnp.max(dt, axis=0, keepdims=True))
            w_intra = jnp.exp(dt - m_t)
            w_inter = jnp.exp(inter - m_t)
            qkt = _dot_nt(kp, qm) * w_intra
            num = w_inter * _dot_nt(c_state.astype(BF16), qm) + _dot(vt, qkt.astype(BF16))
            nq = _dot_nt(n_state.astype(BF16), qm)[0:1, :]
            den = w_inter * nq + jnp.sum(qkt, axis=0, keepdims=True)
            ht_chunks.append(num / jnp.maximum(jnp.abs(den), jnp.exp(-m_t)))
            g_end = a - br + ir
            m_new = jnp.maximum(a + m_state, jnp.max(g_end, axis=1, keepdims=True))
            w_s = jnp.exp(g_end - m_new)
            decay = jnp.exp(a + m_state - m_new)
            vtw = (vt.astype(F32) * w_s).astype(BF16)
            decay_col = decay[:, 0:1]
            c_state = decay_col * c_state + jnp.where(in_head, _dot(vtw, kp), 0.0)
            w_rows = jnp.broadcast_to(w_s, (8, lc)).astype(BF16)
            n_state = decay_col * n_state + jnp.where(in_head[0:8], _dot(w_rows, kp), 0.0)
            m_state = m_new + jnp.zeros_like(m_state)
        c_ref[h] = c_state
        n_ref[h] = n_state
        m_ref[h] = jnp.broadcast_to(m_state[:, 0:1], m_ref.shape[1:])
        ht_rows.append(jnp.concatenate(ht_chunks, axis=1))

    gain = jnp.concatenate([gain_ref[...]] * (tt // LANES_V7X), axis=1)
    outs = []
    for h in range(MLSTM_HEADS):
        rows = slice(h * HEAD_DIM, (h + 1) * HEAD_DIM)
        hg = ht_rows[h] * jax.nn.sigmoid(xot_ref[rows, :].astype(F32))
        msq = jnp.mean(hg * hg, axis=0, keepdims=True)
        outs.append(hg * lax.rsqrt(msq + NORM_EPS) * gain[rows])
    o_ref[...] = jnp.concatenate(outs, axis=0).T.astype(BF16)


def _mlstm(xqk, xvt, xot, gt, gc, convw, convb, gain, cum, batch, seq):
    t = xqk.shape[0]
    tt = MLSTM_ROWS
    steps = seq // tt
    col = lambda r: pl.BlockSpec((r, tt), lambda b, i: (0, b * steps + i))
    return pl.pallas_call(
        _mlstm_kernel,
        grid=(batch, steps),
        in_specs=[
            pl.BlockSpec((tt, 2 * MLSTM_WIDTH), lambda b, i: (b * steps + i, 0)),
            pl.BlockSpec((CONV_HALO, 2 * MLSTM_WIDTH),
                         lambda b, i: (jnp.maximum((b * seq + i * tt) // CONV_HALO - 1, 0), 0)),
            col(MLSTM_WIDTH),
            col(MLSTM_WIDTH),
            col(NUM_GATES),
            pl.BlockSpec((tt, NUM_GATES), lambda b, i: (b * steps + i, 0)),
            _const_spec(convw.shape),
            _const_spec(convb.shape),
            _const_spec(gain.shape),
            _const_spec(cum.shape),
            _const_spec(cum.shape),
        ],
        out_specs=pl.BlockSpec((tt, MLSTM_WIDTH), lambda b, i: (b * steps + i, 0)),
        out_shape=jax.ShapeDtypeStruct((t, MLSTM_WIDTH), BF16),
        scratch_shapes=[
            pltpu.VMEM((CONV_HALO + tt, 2 * MLSTM_WIDTH), F32),
            pltpu.VMEM((MLSTM_HEADS, HEAD_DIM, HEAD_PAIR), F32),
            pltpu.VMEM((MLSTM_HEADS, 8, HEAD_PAIR), F32),
            pltpu.VMEM((MLSTM_HEADS, 8, HEAD_PAIR), F32),
        ],
        compiler_params=pltpu.CompilerParams(dimension_semantics=("arbitrary", "arbitrary")),
        name="mlstm",
    )(xqk, xqk, xvt, xot, gt, gc, convw, convb, gain, cum, cum.T)


def _out_mlp_kernel(x_ref, ym_ref, yl_ref, ys_ref, wom_ref, wol_ref, wos_ref, ln2_ref,
                    wup_ref, wdn_ref, o_ref):
    x1 = (x_ref[...] + _dot(ym_ref[...], wom_ref[...]) + _dot(yl_ref[...], wol_ref[...])
          + _dot(ys_ref[...], wos_ref[...]))
    ms = jnp.mean(x1 * x1, axis=-1, keepdims=True)
    hn = (x1 * lax.rsqrt(ms + NORM_EPS) * ln2_ref[...]).astype(BF16)
    o_ref[...] = x1
    for c in range(0, D_FF, MLP_FF_CHUNK):
        u = _dot(hn, wup_ref[:, c:c + MLP_FF_CHUNK])
        act = jnp.square(jnp.maximum(u, 0.0)).astype(BF16)
        o_ref[...] += _dot(act, wdn_ref[c:c + MLP_FF_CHUNK, :])


def _out_mlp(xf, ym, yl, ys, wom, wol, wos, ln2, wup, wdn):
    t = xf.shape[0]
    tm = OUT_MLP_ROWS
    row = lambda w: pl.BlockSpec((tm, w), lambda i: (i, 0))
    weights = 2 * (wom.size + wol.size + wos.size + wup.size + wdn.size)
    tiles = tm * (4 * 4 * D_MODEL + 2 * 2 * D_MODEL + 4 * 4 * D_MODEL + 6 * MLP_FF_CHUNK)
    return pl.pallas_call(
        _out_mlp_kernel,
        grid=(t // tm,),
        in_specs=[
            row(D_MODEL), row(MOBA_WIDTH), row(MLSTM_WIDTH), row(SWA_Q_WIDTH),
            _const_spec(wom.shape), _const_spec(wol.shape), _const_spec(wos.shape),
            _const_spec(ln2.shape), _const_spec(wup.shape), _const_spec(wdn.shape),
        ],
        out_specs=row(D_MODEL),
        out_shape=jax.ShapeDtypeStruct((t, D_MODEL), F32),
        compiler_params=pltpu.CompilerParams(
            dimension_semantics=("arbitrary",),
            vmem_limit_bytes=_vmem_limit(weights + tiles + (8 << 20))),
        name="out_mlp",
    )(xf, ym, yl, ys, wom, wol, wos, ln2, wup, wdn)


def _rope_tables(seq):
    inv = ROPE_THETA ** (-jnp.arange(0, HEAD_DIM, 2, dtype=F32) / HEAD_DIM)
    ang = jnp.arange(seq, dtype=F32)[:, None] * inv[None, :]
    cos, sin = jnp.cos(ang), jnp.sin(ang)
    cosn = jnp.concatenate([cos, cos, cos, cos], axis=1)
    sinn = jnp.concatenate([-sin, sin, -sin, sin], axis=1)
    return cosn, sinn, cos.T, sin.T


def _layer(xf, tables, consts, batch, seq, ln1, w_in, conv_w, conv_b, igate_b, fgate_b, mlstm_norm,
           moba_q_norm, moba_k_norm, swa_q_norm, swa_k_norm, swa_sinks, w_out, ln2, w_up, w_down):
    cosn, sinn, cost, sint = tables
    bd, cum = consts
    o = 0
    cols = {}
    for name, width in (("mq", MOBA_WIDTH), ("mk", MOBA_WIDTH), ("mv", MOBA_WIDTH), ("sq", SWA_Q_WIDTH),
                        ("sk", SWA_KV_WIDTH), ("sv", SWA_KV_WIDTH), ("xqk", 2 * MLSTM_WIDTH),
                        ("xv", MLSTM_WIDTH), ("xo", MLSTM_WIDTH), ("xi", MLSTM_HEADS), ("xf", MLSTM_HEADS)):
        cols[name] = w_in[:, o:o + width]
        o += width
    wnat = jnp.concatenate([cols["mk"], cols["sk"], cols["xqk"]], axis=1).astype(BF16)
    wtr = jnp.concatenate([cols["mq"], cols["sq"], cols["mv"], cols["sv"], cols["xv"], cols["xo"],
                           cols["xi"], cols["xf"]], axis=1).T.astype(BF16)
    gk = jnp.concatenate([jnp.tile(moba_k_norm, MOBA_HEADS), jnp.tile(swa_k_norm, SWA_KV_HEADS)])[None, :]
    gq = jnp.broadcast_to(
        jnp.concatenate([moba_q_norm * (SM_SCALE * LOG2E), swa_q_norm * SM_SCALE])[:, None],
        (2 * HEAD_DIM, LANES_V7X))
    gbias = jnp.broadcast_to(jnp.concatenate([igate_b, fgate_b])[:, None], (NUM_GATES, LANES_V7X))

    (mk, kmean, sk, xqk, mqt, sqt, mvt, svt, xvt, xot, gt) = _in_proj(
        xf, ln1[None, :], wnat, wtr, bd, gk, cosn, sinn, gq, cost, sint, gbias, seq)

    kmean = kmean.reshape(-1, MOBA_WIDTH)
    ym = _moba(mqt, mk, mvt, kmean, batch, seq)
    ys = _swa(swa_sinks, sqt, sk, svt, batch, seq)
    gain = jnp.broadcast_to(mlstm_norm.reshape(MLSTM_WIDTH, 1), (MLSTM_WIDTH, LANES_V7X))
    yl = _mlstm(xqk, xvt, xot, gt, gt.T, conv_w, conv_b[None, :], gain, cum, batch, seq)

    wo = w_out.astype(BF16)
    return _out_mlp(xf, ym, yl, ys, wo[:MOBA_WIDTH], wo[MOBA_WIDTH:MOBA_WIDTH + MLSTM_WIDTH],
                    wo[MOBA_WIDTH + MLSTM_WIDTH:], ln2[None, :], w_up.astype(BF16), w_down.astype(BF16))


def kernel(x, ln1, w_in, conv_w, conv_b, igate_b, fgate_b, mlstm_norm, moba_q_norm, moba_k_norm,
           swa_q_norm, swa_k_norm, swa_sinks, w_out, ln2, w_up, w_down):
    batch, seq, d = x.shape
    assert d == D_MODEL and seq % max(IN_PROJ_ROWS, SWA_ROWS, MLSTM_ROWS, MOBA_BLOCK) == 0
    depth = ln1.shape[0]
    tables = _rope_tables(seq)
    bw = 2 * LANES_V7X
    ids = jnp.arange(bw) // HEAD_DIM
    bd = jnp.where(ids[:, None] == ids[None, :], 1.0 / HEAD_DIM, 0.0).astype(BF16)
    tids = jnp.arange(MLSTM_ROWS)
    cum = ((tids[:, None] // MLSTM_CHUNK == tids[None, :] // MLSTM_CHUNK)
           & (tids[:, None] <= tids[None, :])).astype(F32)
    xf = x.reshape(batch * seq, d)
    for l in range(depth):
        xf = _layer(xf, tables, (bd, cum), batch, seq, ln1[l], w_in[l], conv_w[l], conv_b[l], igate_b[l],
                    fgate_b[l], mlstm_norm[l], moba_q_norm[l], moba_k_norm[l], swa_q_norm[l],
                    swa_k_norm[l], swa_sinks[l], w_out[l], ln2[l], w_up[l], w_down[l])
    return xf.reshape(batch, seq, d)
```

```python
import functools

import jax
import jax.numpy as jnp
from jax import lax
from jax.experimental import pallas as pl
from jax.experimental.pallas import tpu as pltpu

F32 = jnp.float32
BF16 = jnp.bfloat16
NEG_INF = float("-inf")

D_MODEL = 1024
HEAD_DIM = 64
MOBA_HEADS = 6
MLSTM_HEADS = 4
SWA_Q_HEADS = 6
SWA_KV_HEADS = 2
SWA_GROUP = SWA_Q_HEADS // SWA_KV_HEADS
MOBA_WIDTH = MOBA_HEADS * HEAD_DIM
MLSTM_WIDTH = MLSTM_HEADS * HEAD_DIM
SWA_Q_WIDTH = SWA_Q_HEADS * HEAD_DIM
SWA_KV_WIDTH = SWA_KV_HEADS * HEAD_DIM
MOBA_BLOCK = 256
MOBA_TOPK = 3
MLSTM_CHUNK = 64
CONV_WIDTH = 4
SWA_WINDOW = 128
ROPE_THETA = 10000.0
D_FF = 4 * D_MODEL
NORM_EPS = 1e-6
SM_SCALE = HEAD_DIM ** -0.5
LOG2E = 1.4426950408889634
MOBA_V_ROWS = HEAD_DIM + 16
MOBA_SAFE_LOG2 = 60.0

LANES_V7X = 128
VMEM_BYTES_V7X = 64 * 1024 * 1024
HEAD_PAIR = 2 * HEAD_DIM
assert HEAD_PAIR == LANES_V7X

NAT_WIDTH = MOBA_WIDTH + SWA_KV_WIDTH + 2 * MLSTM_WIDTH
KN_WIDTH = MOBA_WIDTH + SWA_KV_WIDTH
QT_ROWS = MOBA_WIDTH + SWA_Q_WIDTH
TR_ROWS = QT_ROWS + MOBA_WIDTH + SWA_KV_WIDTH + 2 * MLSTM_WIDTH + 2 * MLSTM_HEADS
NUM_GATES = 2 * MLSTM_HEADS

IN_PROJ_ROWS = 512
OUT_MLP_ROWS = 512
MLP_FF_CHUNK = 1024
SWA_ROWS = 512
MLSTM_ROWS = 256
MOBA_GROUP = 2
CONV_HALO = 16


def _dot(a, b):
    return jnp.dot(a, b, preferred_element_type=F32)


def _dot_nt(a, b):
    return lax.dot_general(a, b, (((1,), (1,)), ((), ())), preferred_element_type=F32)


def _split_bf16(v):
    hi = v.astype(BF16)
    lo = (v - hi.astype(F32)).astype(BF16)
    return hi, lo


def _vmem_limit(nbytes):
    return int(min(nbytes, VMEM_BYTES_V7X - 4 * 1024 * 1024))


def _const_spec(shape):
    nd = len(shape)
    return pl.BlockSpec(shape, lambda *_: (0,) * nd, pipeline_mode=pl.Buffered(1))


def _in_proj_kernel(x_ref, ln1_ref, wnat_ref, wtr_ref, bd_ref, gk_ref, cosn_ref, sinn_ref,
                    gq_ref, cost_ref, sint_ref, gbias_ref,
                    mk_ref, kmean_ref, sk_ref, xqk_ref, mqt_ref, sqt_ref, mvt_ref, svt_ref,
                    xvt_ref, xot_ref, gt_ref):
    tm = x_ref.shape[0]
    x = x_ref[...]
    ms = jnp.mean(x * x, axis=-1, keepdims=True)
    hn = (x * lax.rsqrt(ms + NORM_EPS) * ln1_ref[...]).astype(BF16)
    nat = _dot(hn, wnat_ref[...])
    tr = _dot_nt(wtr_ref[...], hn)

    kk = nat[:, :KN_WIDTH]
    hi, lo = _split_bf16(kk * kk)
    bd = bd_ref[...]
    bw = bd.shape[0]
    msk = jnp.concatenate(
        [_dot(hi[:, c:c + bw], bd) + _dot(lo[:, c:c + bw], bd) for c in range(0, KN_WIDTH, bw)],
        axis=1)
    kn = kk * lax.rsqrt(msk + NORM_EPS) * gk_ref[...]
    reps = KN_WIDTH // LANES_V7X
    cosn = jnp.concatenate([cosn_ref[...]] * reps, axis=1)
    sinn = jnp.concatenate([sinn_ref[...]] * reps, axis=1)
    lane = lax.broadcasted_iota(jnp.int32, kn.shape, 1)
    first_half = (lane % HEAD_DIM) < (HEAD_DIM // 2)
    swapped = jnp.where(first_half,
                        pltpu.roll(kn, KN_WIDTH - HEAD_DIM // 2, 1),
                        pltpu.roll(kn, HEAD_DIM // 2, 1))
    kr = kn * cosn + swapped * sinn
    mk = kr[:, :MOBA_WIDTH]
    mk_ref[...] = mk.astype(BF16)
    nblk = tm // MOBA_BLOCK
    kmean_ref[0] = jnp.concatenate(
        [jnp.mean(mk[c * MOBA_BLOCK:(c + 1) * MOBA_BLOCK], axis=0, keepdims=True) for c in range(nblk)],
        axis=0)
    sk_ref[...] = kr[:, MOBA_WIDTH:].astype(BF16)
    xqk_ref[...] = nat[:, KN_WIDTH:].astype(BF16)

    cost = cost_ref[...]
    sint = sint_ref[...]
    gq = jnp.concatenate([gq_ref[...]] * (tm // LANES_V7X), axis=1)
    half = HEAD_DIM // 2
    for h in range(QT_ROWS // HEAD_DIM):
        blk = tr[h * HEAD_DIM:(h + 1) * HEAD_DIM]
        is_swa = h >= MOBA_HEADS
        gain = gq[HEAD_DIM:] if is_swa else gq[:HEAD_DIM]
        msq = jnp.mean(blk * blk, axis=0, keepdims=True)
        qn = blk * lax.rsqrt(msq + NORM_EPS) * gain
        x1, x2 = qn[:half], qn[half:]
        rot = jnp.concatenate([x1 * cost - x2 * sint, x2 * cost + x1 * sint], axis=0).astype(BF16)
        if is_swa:
            r0 = (h - MOBA_HEADS) * HEAD_DIM
            sqt_ref[r0:r0 + HEAD_DIM, :] = rot
        else:
            mqt_ref[h * HEAD_DIM:(h + 1) * HEAD_DIM, :] = rot
    r = QT_ROWS
    mv = tr[r:r + MOBA_WIDTH].astype(BF16)
    ones = jnp.ones((MOBA_V_ROWS - HEAD_DIM, MOBA_BLOCK), BF16)
    for c in range(nblk):
        for h in range(MOBA_HEADS):
            r0 = h * MOBA_V_ROWS
            mvt_ref[c, r0:r0 + HEAD_DIM, :] = mv[h * HEAD_DIM:(h + 1) * HEAD_DIM,
                                                 c * MOBA_BLOCK:(c + 1) * MOBA_BLOCK]
            mvt_ref[c, r0 + HEAD_DIM:r0 + MOBA_V_ROWS, :] = ones
    r += MOBA_WIDTH
    svt_ref[...] = tr[r:r + SWA_KV_WIDTH].astype(BF16)
    r += SWA_KV_WIDTH
    xvt_ref[...] = tr[r:r + MLSTM_WIDTH].astype(BF16)
    r += MLSTM_WIDTH
    xot_ref[...] = tr[r:r + MLSTM_WIDTH].astype(BF16)
    r += MLSTM_WIDTH
    gbias = jnp.concatenate([gbias_ref[...]] * (tm // LANES_V7X), axis=1)
    gt_ref[...] = tr[r:r + NUM_GATES] + gbias


def _in_proj(xf, ln1, wnat, wtr, bd, gk, cosn, sinn, gq, cost, sint, gbias, seq):
    t = xf.shape[0]
    tm = IN_PROJ_ROWS
    steps = t // tm
    seq_steps = seq // tm
    nblk = tm // MOBA_BLOCK
    row = lambda w: pl.BlockSpec((tm, w), lambda i: (i, 0))
    col = lambda r: pl.BlockSpec((r, tm), lambda i: (0, i))
    in_specs = [
        row(D_MODEL),
        _const_spec((1, D_MODEL)),
        _const_spec(wnat.shape),
        _const_spec(wtr.shape),
        _const_spec(bd.shape),
        _const_spec(gk.shape),
        pl.BlockSpec((tm, LANES_V7X), lambda i: (i % seq_steps, 0)),
        pl.BlockSpec((tm, LANES_V7X), lambda i: (i % seq_steps, 0)),
        _const_spec(gq.shape),
        pl.BlockSpec((HEAD_DIM // 2, tm), lambda i: (0, i % seq_steps)),
        pl.BlockSpec((HEAD_DIM // 2, tm), lambda i: (0, i % seq_steps)),
        _const_spec(gbias.shape),
    ]
    out_shape = [
        jax.ShapeDtypeStruct((t, MOBA_WIDTH), BF16),
        jax.ShapeDtypeStruct((steps, nblk, MOBA_WIDTH), F32),
        jax.ShapeDtypeStruct((t, SWA_KV_WIDTH), BF16),
        jax.ShapeDtypeStruct((t, 2 * MLSTM_WIDTH), BF16),
        jax.ShapeDtypeStruct((MOBA_WIDTH, t), BF16),
        jax.ShapeDtypeStruct((SWA_Q_WIDTH, t), BF16),
        jax.ShapeDtypeStruct((t // MOBA_BLOCK, MOBA_HEADS * MOBA_V_ROWS, MOBA_BLOCK), BF16),
        jax.ShapeDtypeStruct((SWA_KV_WIDTH, t), BF16),
        jax.ShapeDtypeStruct((MLSTM_WIDTH, t), BF16),
        jax.ShapeDtypeStruct((MLSTM_WIDTH, t), BF16),
        jax.ShapeDtypeStruct((NUM_GATES, t), F32),
    ]
    out_specs = [
        row(MOBA_WIDTH),
        pl.BlockSpec((1, nblk, MOBA_WIDTH), lambda i: (i, 0, 0)),
        row(SWA_KV_WIDTH),
        row(2 * MLSTM_WIDTH),
        col(MOBA_WIDTH),
        col(SWA_Q_WIDTH),
        pl.BlockSpec((nblk, MOBA_HEADS * MOBA_V_ROWS, MOBA_BLOCK), lambda i: (i, 0, 0)),
        col(SWA_KV_WIDTH),
        col(MLSTM_WIDTH),
        col(MLSTM_WIDTH),
        col(NUM_GATES),
    ]
    weights = 2 * (wnat.size + wtr.size)
    tiles = tm * (2 * 4 * D_MODEL + 3 * 4 * (NAT_WIDTH + TR_ROWS) + 2 * 2 * (NAT_WIDTH + TR_ROWS))
    return pl.pallas_call(
        _in_proj_kernel,
        grid=(steps,),
        in_specs=in_specs,
        out_specs=out_specs,
        out_shape=out_shape,
        compiler_params=pltpu.CompilerParams(
            dimension_semantics=("arbitrary",),
            vmem_limit_bytes=_vmem_limit(2 * weights + tiles + (8 << 20))),
        name="in_proj",
    )(xf, ln1, wnat, wtr, bd, gk, cosn, sinn, gq, cost, sint, gbias)


def _moba_kernel(bounded_ref, qt_ref, k_ref, vt_ref, km_ref, o_ref, sel_ref, s_ref, p_ref):
    nb = k_ref.shape[0]
    lq = qt_ref.shape[1]
    i = pl.program_id(2)
    qt = qt_ref[...]
    row = lax.broadcasted_iota(jnp.int32, qt.shape, 0)
    km_hi, km_lo = _split_bf16(km_ref[...])
    blk_id = lax.broadcasted_iota(jnp.int32, (nb, lq), 0)
    zero = jnp.zeros_like(qt)
    qh = []
    for hh in range(2):
        q_one = jnp.where((row >= hh * HEAD_DIM) & (row < (hh + 1) * HEAD_DIM), qt, zero)
        qh.append(q_one)
        gate = _dot(km_hi, q_one) + _dot(km_lo, q_one)
        gate = jnp.where(blk_id < i, gate, NEG_INF)
        sel = jnp.zeros((nb, lq), dtype=jnp.bool_)
        for _ in range(MOBA_TOPK):
            mx = jnp.max(gate, axis=0, keepdims=True)
            is_mx = (gate == mx) & (mx > NEG_INF)
            idx = jnp.min(jnp.where(is_mx, blk_id, nb), axis=0, keepdims=True)
            pick = blk_id == idx
            sel = sel | pick
            gate = jnp.where(pick, NEG_INF, gate)
        sel_ref[hh] = jnp.where(sel, 0.0, NEG_INF).astype(F32)

    kd = k_ref[i]
    vd = vt_ref[i]
    kpos = lax.broadcasted_iota(jnp.int32, (lq, lq), 0)
    qpos = lax.broadcasted_iota(jnp.int32, (lq, lq), 1)
    causal = kpos <= qpos

    def weighted_values(st, mu, v_aug):
        pv = _dot(v_aug, jnp.exp2(st - mu).astype(BF16))
        return pv[HEAD_DIM:HEAD_DIM + 1], pv[:HEAD_DIM]

    def finish(num0, den0, num1, den1):
        ot = jnp.concatenate([num0 / den0, num1 / den1], axis=0)
        o_ref[...] = ot.T.astype(BF16)

    trips = lax.shift_right_logical(i + (2 * MOBA_GROUP - 1), MOBA_GROUP.bit_length())

    @pl.when(bounded_ref[0] != 0)
    def _():
        acc = []
        for hh in range(2):
            p = jnp.exp2(jnp.where(causal, _dot(kd, qh[hh]), NEG_INF)).astype(BF16)
            acc.append(_dot(vd[hh * MOBA_V_ROWS:(hh + 1) * MOBA_V_ROWS, :], p))

        def stage_p(blk0, slot0):
            kg = k_ref[pl.ds(jnp.minimum(blk0, nb - MOBA_GROUP), MOBA_GROUP)].reshape(MOBA_GROUP * lq, HEAD_PAIR)
            for hh in range(2):
                p = jnp.exp2(_dot(kg, qh[hh])).astype(BF16)
                for u in range(MOBA_GROUP):
                    p_ref[slot0 + 2 * u + hh] = p[u * lq:(u + 1) * lq]

        def stage_v(blk0, slot0, acc):
            acc = list(acc)
            for u in range(MOBA_GROUP):
                j = blk0 + u
                vj = vt_ref[j]
                for hh in range(2):
                    pv = _dot(vj[hh * MOBA_V_ROWS:(hh + 1) * MOBA_V_ROWS, :], p_ref[slot0 + 2 * u + hh])
                    acc[hh] = acc[hh] + jnp.exp2(sel_ref[hh, pl.ds(j, 1), :]) * pv
            return acc

        def body(t, acc):
            blk = t * (2 * MOBA_GROUP)
            stage_p(blk + MOBA_GROUP, 2 * MOBA_GROUP)
            acc = stage_v(blk, 0, acc)
            stage_p(blk + 2 * MOBA_GROUP, 0)
            return tuple(stage_v(blk + MOBA_GROUP, 2 * MOBA_GROUP, acc))

        stage_p(0, 0)
        acc = lax.fori_loop(0, trips, body, tuple(acc))
        finish(acc[0][:HEAD_DIM], acc[0][HEAD_DIM:HEAD_DIM + 1],
               acc[1][:HEAD_DIM], acc[1][HEAD_DIM:HEAD_DIM + 1])

    units = 2 * MOBA_GROUP

    def stage_a(blk0, slot0):
        mus = []
        for u in range(MOBA_GROUP):
            kj = k_ref[jnp.minimum(blk0 + u, nb - 1)]
            for hh in range(2):
                st = _dot(kj, qh[hh])
                s_ref[slot0 + 2 * u + hh] = st
                mus.append(jnp.max(st, axis=0, keepdims=True))
        return mus

    def stage_b(blk0, slot0, mus, state):
        state = list(state)
        for u in range(MOBA_GROUP):
            j = blk0 + u
            vj = vt_ref[j]
            for hh in range(2):
                mu = mus[2 * u + hh]
                ls, pv = weighted_values(s_ref[slot0 + 2 * u + hh], mu,
                                         vj[hh * MOBA_V_ROWS:(hh + 1) * MOBA_V_ROWS, :])
                m, l, acc = state[3 * hh:3 * hh + 3]
                mu_sel = mu + sel_ref[hh, pl.ds(j, 1), :]
                m_new = jnp.maximum(m, mu_sel)
                f = jnp.exp2(mu_sel - m_new)
                c = jnp.exp2(m - m_new)
                state[3 * hh:3 * hh + 3] = [m_new, c * l + f * ls, c * acc + f * pv]
        return state

    def body(t, loop_carry):
        mus0, state = loop_carry[:units], loop_carry[units:]
        blk = t * (2 * MOBA_GROUP)
        mus1 = stage_a(blk + MOBA_GROUP, units)
        state = stage_b(blk, 0, mus0, state)
        mus0 = stage_a(blk + 2 * MOBA_GROUP, 0)
        state = stage_b(blk + MOBA_GROUP, units, mus1, state)
        return (*mus0, *state)

    @pl.when(bounded_ref[0] == 0)
    def _():
        state = []
        for hh in range(2):
            st = jnp.where(causal, _dot(kd, qh[hh]), NEG_INF)
            mu = jnp.max(st, axis=0, keepdims=True)
            state += [mu, *weighted_values(st, mu, vd[hh * MOBA_V_ROWS:(hh + 1) * MOBA_V_ROWS, :])]
        state = lax.fori_loop(0, trips, body, (*stage_a(0, 0), *state))[units:]
        finish(state[2], state[1], state[5], state[4])


def _moba(bounded, mqt, mk, mvt, kmean, batch, seq):
    t = mk.shape[0]
    nb = seq // MOBA_BLOCK
    lq = MOBA_BLOCK
    pairs = MOBA_WIDTH // HEAD_PAIR
    k3 = mk.reshape(t // MOBA_BLOCK, MOBA_BLOCK, MOBA_WIDTH)
    assert nb % (2 * MOBA_GROUP) == 0
    kv_bytes = 2 * 2 * nb * MOBA_BLOCK * (HEAD_PAIR + 2 * MOBA_V_ROWS)
    return pl.pallas_call(
        _moba_kernel,
        grid=(batch, pairs, nb),
        in_specs=[
            pl.BlockSpec(memory_space=pltpu.SMEM),
            pl.BlockSpec((HEAD_PAIR, lq), lambda b, p, i: (p, b * nb + i)),
            pl.BlockSpec((nb, MOBA_BLOCK, HEAD_PAIR), lambda b, p, i: (b, 0, p)),
            pl.BlockSpec((nb, 2 * MOBA_V_ROWS, MOBA_BLOCK), lambda b, p, i: (b, p, 0)),
            pl.BlockSpec((nb, HEAD_PAIR), lambda b, p, i: (b, p)),
        ],
        out_specs=pl.BlockSpec((lq, HEAD_PAIR), lambda b, p, i: (b * nb + i, p)),
        out_shape=jax.ShapeDtypeStruct((t, MOBA_WIDTH), BF16),
        scratch_shapes=[pltpu.VMEM((2, nb, lq), F32), pltpu.VMEM((4 * MOBA_GROUP, lq, lq), F32),
                        pltpu.VMEM((4 * MOBA_GROUP, lq, lq), BF16)],
        compiler_params=pltpu.CompilerParams(
            dimension_semantics=("arbitrary", "arbitrary", "arbitrary"),
            vmem_limit_bytes=_vmem_limit(kv_bytes + (16 << 20))),
        name="moba",
    )(bounded, mqt, k3, mvt, kmean)


def _swa_kernel(sink_ref, qt_ref, k_ref, kh_ref, vt_ref, vth_ref, o_ref):
    tq = k_ref.shape[0]
    w = SWA_WINDOW
    i = pl.program_id(1)
    ncol = SWA_GROUP * w
    kpos = lax.broadcasted_iota(jnp.int32, (2 * w, ncol), 0)
    col = lax.broadcasted_iota(jnp.int32, (2 * w, ncol), 1)
    diff = w + (col % w) - kpos
    in_window = (diff >= 0) & (diff < w)
    zeros = jnp.zeros((HEAD_DIM, ncol), BF16)
    for r in range(tq // w):
        if r == 0:
            kcat = jnp.concatenate([kh_ref[...], k_ref[0:w, :]], axis=0)
            vcat = jnp.concatenate([vth_ref[...], vt_ref[:, 0:w]], axis=1)
            mask = in_window & ((kpos >= w) | (i > 0))
        else:
            kcat = k_ref[(r - 1) * w:(r + 1) * w, :]
            vcat = vt_ref[:, (r - 1) * w:(r + 1) * w]
            mask = in_window
        pieces = []
        for g in range(SWA_KV_HEADS):
            qs = jnp.concatenate(
                [qt_ref[(SWA_GROUP * g + a) * HEAD_DIM:(SWA_GROUP * g + a + 1) * HEAD_DIM, r * w:(r + 1) * w]
                 for a in range(SWA_GROUP)], axis=1)
            qpad = jnp.concatenate([qs, zeros] if g == 0 else [zeros, qs], axis=0)
            st = jnp.where(mask, _dot(kcat, qpad), NEG_INF)
            sink = jnp.concatenate(
                [jnp.full((1, w), sink_ref[SWA_GROUP * g + a], F32) for a in range(SWA_GROUP)], axis=1)
            m = jnp.maximum(jnp.max(st, axis=0, keepdims=True), sink)
            p = jnp.exp(st - m)
            l = jnp.sum(p, axis=0, keepdims=True) + jnp.exp(sink - m)
            ot = _dot(vcat[g * HEAD_DIM:(g + 1) * HEAD_DIM, :], p.astype(BF16)) / l
            pieces += [ot[:, a * w:(a + 1) * w] for a in range(SWA_GROUP)]
        o_ref[r * w:(r + 1) * w, :] = jnp.concatenate(pieces, axis=0).T.astype(BF16)


def _swa(sinks, sqt, sk, svt, batch, seq):
    t = sk.shape[0]
    tq = SWA_ROWS
    w = SWA_WINDOW
    steps = seq // tq
    halo = lambda b, i: jnp.maximum((b * seq + i * tq) // w - 1, 0)
    return pl.pallas_call(
        _swa_kernel,
        grid=(batch, steps),
        in_specs=[
            pl.BlockSpec(memory_space=pltpu.SMEM),
            pl.BlockSpec((SWA_Q_WIDTH, tq), lambda b, i: (0, b * steps + i)),
            pl.BlockSpec((tq, SWA_KV_WIDTH), lambda b, i: (b * steps + i, 0)),
            pl.BlockSpec((w, SWA_KV_WIDTH), lambda b, i: (halo(b, i), 0)),
            pl.BlockSpec((SWA_KV_WIDTH, tq), lambda b, i: (0, b * steps + i)),
            pl.BlockSpec((SWA_KV_WIDTH, w), lambda b, i: (0, halo(b, i))),
        ],
        out_specs=pl.BlockSpec((tq, SWA_Q_WIDTH), lambda b, i: (b * steps + i, 0)),
        out_shape=jax.ShapeDtypeStruct((t, SWA_Q_WIDTH), BF16),
        compiler_params=pltpu.CompilerParams(dimension_semantics=("arbitrary", "arbitrary")),
        name="swa",
    )(sinks, sqt, sk, sk, svt, svt)


def _log_sigmoid(v):
    return jnp.minimum(v, 0.0) - jnp.log1p(jnp.exp(-jnp.abs(v)))


def _mlstm_kernel(xqk_ref, halo_ref, xvt_ref, xot_ref, gt_ref, gc_ref, convw_ref, convb_ref,
                  gain_ref, cum_ref, cumt_ref, o_ref, ext_ref, c_ref, n_ref, m_ref):
    tt = xqk_ref.shape[0]
    lc = MLSTM_CHUNK
    i = pl.program_id(1)

    @pl.when(i == 0)
    def _():
        c_ref[...] = jnp.zeros_like(c_ref)
        n_ref[...] = jnp.zeros_like(n_ref)
        m_ref[...] = jnp.zeros_like(m_ref)

    hist = halo_ref[...].astype(F32)
    ext_ref[0:CONV_HALO, :] = jnp.where(i > 0, hist, jnp.zeros_like(hist))
    ext_ref[CONV_HALO:CONV_HALO + tt, :] = xqk_ref[...].astype(F32)
    convw = convw_ref[...]
    conv = jnp.zeros((tt, 2 * MLSTM_WIDTH), F32) + convb_ref[...]
    for j in range(CONV_WIDTH):
        start = CONV_HALO - (CONV_WIDTH - 1) + j
        conv = conv + convw[j:j + 1, :] * ext_ref[start:start + tt, :]
    qk = conv * jax.nn.sigmoid(conv)
    q_all = qk[:, :MLSTM_WIDTH]
    k_all = qk[:, MLSTM_WIDTH:] * SM_SCALE

    g_row = gt_ref[...]
    g_col = gc_ref[...]
    cum = cum_ref[...]
    b_row = jnp.dot(_log_sigmoid(g_row), cum, preferred_element_type=F32,
                    precision=lax.Precision.HIGHEST)
    b_col = jnp.dot(cumt_ref[...], _log_sigmoid(g_col), preferred_element_type=F32,
                    precision=lax.Precision.HIGHEST)
    u_col = b_col[:, MLSTM_HEADS:] - g_col[:, :MLSTM_HEADS]

    lane = lax.broadcasted_iota(jnp.int32, (lc, HEAD_PAIR), 1)
    s_idx = lax.broadcasted_iota(jnp.int32, (lc, lc), 0)
    t_idx = lax.broadcasted_iota(jnp.int32, (lc, lc), 1)
    causal = s_idx <= t_idx
    ht_rows = []
    for h in range(MLSTM_HEADS):
        p, hh = divmod(h, 2)
        in_head = (lane >= hh * HEAD_DIM) & (lane < (hh + 1) * HEAD_DIM)
        c_state = c_ref[h]
        n_state = n_ref[h]
        m_state = m_ref[h][0:1, 0:lc]
        ht_chunks = []
        for c in range(tt // lc):
            sl = slice(c * lc, (c + 1) * lc)
            q_pair = q_all[sl, p * HEAD_PAIR:(p + 1) * HEAD_PAIR]
            qm = jnp.where(in_head, q_pair, 0.0).astype(BF16)
            kp = k_all[sl, p * HEAD_PAIR:(p + 1) * HEAD_PAIR].astype(BF16)
            vt = xvt_ref[h * HEAD_DIM:(h + 1) * HEAD_DIM, sl]
            br = b_row[MLSTM_HEADS + h:MLSTM_HEADS + h + 1, sl]
            ir = g_row[h:h + 1, sl]
            uc = u_col[sl, h:h + 1]
            a = br[:, lc - 1:lc]
            dt = jnp.where(causal, br - uc, NEG_INF)
            inter = br + m_state
            m_t = jnp.maximum(inter, jnp.max(dt, axis=0, keepdims=True))
            w_intra = jnp.exp(dt - m_t)
            w_inter = jnp.exp(inter - m_t)
            qkt = _dot_nt(kp, qm) * w_intra
            num = w_inter * _dot_nt(c_state.astype(BF16), qm) + _dot(vt, qkt.astype(BF16))
            nq = _dot_nt(n_state.astype(BF16), qm)[0:1, :]
            den = w_inter * nq + jnp.sum(qkt, axis=0, keepdims=True)
            ht_chunks.append(num / jnp.maximum(jnp.abs(den), jnp.exp(-m_t)))
            g_end = a - br + ir
            m_new = jnp.maximum(a + m_state, jnp.max(g_end, axis=1, keepdims=True))
            w_s = jnp.exp(g_end - m_new)
            decay = jnp.exp(a + m_state - m_new)
            vtw = (vt.astype(F32) * w_s).astype(BF16)
            decay_col = decay[:, 0:1]
            c_state = decay_col * c_state + jnp.where(in_head, _dot(vtw, kp), 0.0)
            w_rows = jnp.broadcast_to(w_s, (8, lc)).astype(BF16)
            n_state = decay_col * n_state + jnp.where(in_head[0:8], _dot(w_rows, kp), 0.0)
            m_state = m_new + jnp.zeros_like(m_state)
        c_ref[h] = c_state
        n_ref[h] = n_state
        m_ref[h] = jnp.broadcast_to(m_state[:, 0:1], m_ref.shape[1:])
        ht_rows.append(jnp.concatenate(ht_chunks, axis=1))

    gain = jnp.concatenate([gain_ref[...]] * (tt // LANES_V7X), axis=1)
    outs = []
    for h in range(MLSTM_HEADS):
        rows = slice(h * HEAD_DIM, (h + 1) * HEAD_DIM)
        hg = ht_rows[h] * jax.nn.sigmoid(xot_ref[rows, :].astype(F32))
        msq = jnp.mean(hg * hg, axis=0, keepdims=True)
        outs.append(hg * lax.rsqrt(msq + NORM_EPS) * gain[rows])
    o_ref[...] = jnp.concatenate(outs, axis=0).T.astype(BF16)


def _mlstm(xqk, xvt, xot, gt, gc, convw, convb, gain, cum, batch, seq):
    t = xqk.shape[0]
    tt = MLSTM_ROWS
    steps = seq // tt
    col = lambda r: pl.BlockSpec((r, tt), lambda b, i: (0, b * steps + i))
    return pl.pallas_call(
        _mlstm_kernel,
        grid=(batch, steps),
        in_specs=[
            pl.BlockSpec((tt, 2 * MLSTM_WIDTH), lambda b, i: (b * steps + i, 0)),
            pl.BlockSpec((CONV_HALO, 2 * MLSTM_WIDTH),
                         lambda b, i: (jnp.maximum((b * seq + i * tt) // CONV_HALO - 1, 0), 0)),
            col(MLSTM_WIDTH),
            col(MLSTM_WIDTH),
            col(NUM_GATES),
            pl.BlockSpec((tt, NUM_GATES), lambda b, i: (b * steps + i, 0)),
            _const_spec(convw.shape),
            _const_spec(convb.shape),
            _const_spec(gain.shape),
            _const_spec(cum.shape),
            _const_spec(cum.shape),
        ],
        out_specs=pl.BlockSpec((tt, MLSTM_WIDTH), lambda b, i: (b * steps + i, 0)),
        out_shape=jax.ShapeDtypeStruct((t, MLSTM_WIDTH), BF16),
        scratch_shapes=[
            pltpu.VMEM((CONV_HALO + tt, 2 * MLSTM_WIDTH), F32),
            pltpu.VMEM((MLSTM_HEADS, HEAD_DIM, HEAD_PAIR), F32),
            pltpu.VMEM((MLSTM_HEADS, 8, HEAD_PAIR), F32),
            pltpu.VMEM((MLSTM_HEADS, 8, HEAD_PAIR), F32),
        ],
        compiler_params=pltpu.CompilerParams(dimension_semantics=("arbitrary", "arbitrary")),
        name="mlstm",
    )(xqk, xqk, xvt, xot, gt, gc, convw, convb, gain, cum, cum.T)


def _out_mlp_kernel(x_ref, ym_ref, yl_ref, ys_ref, wom_ref, wol_ref, wos_ref, ln2_ref,
                    wup_ref, wdn_ref, o_ref):
    x1 = (x_ref[...] + _dot(ym_ref[...], wom_ref[...]) + _dot(yl_ref[...], wol_ref[...])
          + _dot(ys_ref[...], wos_ref[...]))
    ms = jnp.mean(x1 * x1, axis=-1, keepdims=True)
    hn = (x1 * lax.rsqrt(ms + NORM_EPS) * ln2_ref[...]).astype(BF16)
    o_ref[...] = x1
    for c in range(0, D_FF, MLP_FF_CHUNK):
        u = _dot(hn, wup_ref[:, c:c + MLP_FF_CHUNK])
        act = jnp.square(jnp.maximum(u, 0.0)).astype(BF16)
        o_ref[...] += _dot(act, wdn_ref[c:c + MLP_FF_CHUNK, :])


def _out_mlp(xf, ym, yl, ys, wom, wol, wos, ln2, wup, wdn):
    t = xf.shape[0]
    tm = OUT_MLP_ROWS
    row = lambda w: pl.BlockSpec((tm, w), lambda i: (i, 0))
    weights = 2 * (wom.size + wol.size + wos.size + wup.size + wdn.size)
    tiles = tm * (4 * 4 * D_MODEL + 2 * 2 * D_MODEL + 4 * 4 * D_MODEL + 6 * MLP_FF_CHUNK)
    return pl.pallas_call(
        _out_mlp_kernel,
        grid=(t // tm,),
        in_specs=[
            row(D_MODEL), row(MOBA_WIDTH), row(MLSTM_WIDTH), row(SWA_Q_WIDTH),
            _const_spec(wom.shape), _const_spec(wol.shape), _const_spec(wos.shape),
            _const_spec(ln2.shape), _const_spec(wup.shape), _const_spec(wdn.shape),
        ],
        out_specs=row(D_MODEL),
        out_shape=jax.ShapeDtypeStruct((t, D_MODEL), F32),
        compiler_params=pltpu.CompilerParams(
            dimension_semantics=("arbitrary",),
            vmem_limit_bytes=_vmem_limit(weights + tiles + (8 << 20))),
        name="out_mlp",
    )(xf, ym, yl, ys, wom, wol, wos, ln2, wup, wdn)


def _rope_tables(seq):
    inv = ROPE_THETA ** (-jnp.arange(0, HEAD_DIM, 2, dtype=F32) / HEAD_DIM)
    ang = jnp.arange(seq, dtype=F32)[:, None] * inv[None, :]
    cos, sin = jnp.cos(ang), jnp.sin(ang)
    cosn = jnp.concatenate([cos, cos, cos, cos], axis=1)
    sinn = jnp.concatenate([-sin, sin, -sin, sin], axis=1)
    return cosn, sinn, cos.T, sin.T


def _layer(xf, tables, consts, batch, seq, ln1, w_in, conv_w, conv_b, igate_b, fgate_b, mlstm_norm,
           moba_q_norm, moba_k_norm, swa_q_norm, swa_k_norm, swa_sinks, w_out, ln2, w_up, w_down):
    cosn, sinn, cost, sint = tables
    bd, cum = consts
    o = 0
    cols = {}
    for name, width in (("mq", MOBA_WIDTH), ("mk", MOBA_WIDTH), ("mv", MOBA_WIDTH), ("sq", SWA_Q_WIDTH),
                        ("sk", SWA_KV_WIDTH), ("sv", SWA_KV_WIDTH), ("xqk", 2 * MLSTM_WIDTH),
                        ("xv", MLSTM_WIDTH), ("xo", MLSTM_WIDTH), ("xi", MLSTM_HEADS), ("xf", MLSTM_HEADS)):
        cols[name] = w_in[:, o:o + width]
        o += width
    wnat = jnp.concatenate([cols["mk"], cols["sk"], cols["xqk"]], axis=1).astype(BF16)
    wtr = jnp.concatenate([cols["mq"], cols["sq"], cols["mv"], cols["sv"], cols["xv"], cols["xo"],
                           cols["xi"], cols["xf"]], axis=1).T.astype(BF16)
    gk = jnp.concatenate([jnp.tile(moba_k_norm, MOBA_HEADS), jnp.tile(swa_k_norm, SWA_KV_HEADS)])[None, :]
    gq = jnp.broadcast_to(
        jnp.concatenate([moba_q_norm * (SM_SCALE * LOG2E), swa_q_norm * SM_SCALE])[:, None],
        (2 * HEAD_DIM, LANES_V7X))
    gbias = jnp.broadcast_to(jnp.concatenate([igate_b, fgate_b])[:, None], (NUM_GATES, LANES_V7X))

    (mk, kmean, sk, xqk, mqt, sqt, mvt, svt, xvt, xot, gt) = _in_proj(
        xf, ln1[None, :], wnat, wtr, bd, gk, cosn, sinn, gq, cost, sint, gbias, seq)

    kmean = kmean.reshape(-1, MOBA_WIDTH)
    score_bound = (HEAD_DIM * SM_SCALE * LOG2E) * jnp.max(jnp.abs(moba_q_norm)) * jnp.max(jnp.abs(moba_k_norm))
    bounded = (score_bound <= MOBA_SAFE_LOG2).astype(jnp.int32).reshape(1)
    ym = _moba(bounded, mqt, mk, mvt, kmean, batch, seq)
    ys = _swa(swa_sinks, sqt, sk, svt, batch, seq)
    gain = jnp.broadcast_to(mlstm_norm.reshape(MLSTM_WIDTH, 1), (MLSTM_WIDTH, LANES_V7X))
    yl = _mlstm(xqk, xvt, xot, gt, gt.T, conv_w, conv_b[None, :], gain, cum, batch, seq)

    wo = w_out.astype(BF16)
    return _out_mlp(xf, ym, yl, ys, wo[:MOBA_WIDTH], wo[MOBA_WIDTH:MOBA_WIDTH + MLSTM_WIDTH],
                    wo[MOBA_WIDTH + MLSTM_WIDTH:], ln2[None, :], w_up.astype(BF16), w_down.astype(BF16))


def kernel(x, ln1, w_in, conv_w, conv_b, igate_b, fgate_b, mlstm_norm, moba_q_norm, moba_k_norm,
           swa_q_norm, swa_k_norm, swa_sinks, w_out, ln2, w_up, w_down):
    batch, seq, d = x.shape
    assert d == D_MODEL and seq % max(IN_PROJ_ROWS, SWA_ROWS, MLSTM_ROWS, MOBA_BLOCK) == 0
    depth = ln1.shape[0]
    tables = _rope_tables(seq)
    bw = 2 * LANES_V7X
    ids = jnp.arange(bw) // HEAD_DIM
    bd = jnp.where(ids[:, None] == ids[None, :], 1.0 / HEAD_DIM, 0.0).astype(BF16)
    tids = jnp.arange(MLSTM_ROWS)
    cum = ((tids[:, None] // MLSTM_CHUNK == tids[None, :] // MLSTM_CHUNK)
           & (tids[:, None] <= tids[None, :])).astype(F32)
    xf = x.reshape(batch * seq, d)
    for l in range(depth):
        xf = _layer(xf, tables, (bd, cum), batch, seq, ln1[l], w_in[l], conv_w[l], conv_b[l], igate_b[l],
                    fgate_b[l], mlstm_norm[l], moba_q_norm[l], moba_k_norm[l], swa_q_norm[l],
                    swa_k_norm[l], swa_sinks[l], w_out[l], ln2[l], w_up[l], w_down[l])
    return xf.reshape(batch, seq, d)
```

```python
import functools

import jax
import jax.numpy as jnp
from jax import lax
from jax.experimental import pallas as pl
from jax.experimental.pallas import tpu as pltpu

F32 = jnp.float32
BF16 = jnp.bfloat16
NEG_INF = float("-inf")

D_MODEL = 1024
HEAD_DIM = 64
MOBA_HEADS = 6
MLSTM_HEADS = 4
SWA_Q_HEADS = 6
SWA_KV_HEADS = 2
SWA_GROUP = SWA_Q_HEADS // SWA_KV_HEADS
MOBA_WIDTH = MOBA_HEADS * HEAD_DIM
MLSTM_WIDTH = MLSTM_HEADS * HEAD_DIM
SWA_Q_WIDTH = SWA_Q_HEADS * HEAD_DIM
SWA_KV_WIDTH = SWA_KV_HEADS * HEAD_DIM
MOBA_BLOCK = 256
MOBA_TOPK = 3
MLSTM_CHUNK = 64
CONV_WIDTH = 4
SWA_WINDOW = 128
ROPE_THETA = 10000.0
D_FF = 4 * D_MODEL
NORM_EPS = 1e-6
SM_SCALE = HEAD_DIM ** -0.5
LOG2E = 1.4426950408889634
MOBA_V_ROWS = HEAD_DIM + 16
MOBA_SAFE_LOG2 = 60.0

LANES_V7X = 128
VMEM_BYTES_V7X = 64 * 1024 * 1024
HEAD_PAIR = 2 * HEAD_DIM
assert HEAD_PAIR == LANES_V7X

NAT_WIDTH = MOBA_WIDTH + SWA_KV_WIDTH + 2 * MLSTM_WIDTH
KN_WIDTH = MOBA_WIDTH + SWA_KV_WIDTH
QT_ROWS = MOBA_WIDTH + SWA_Q_WIDTH
TR_ROWS = QT_ROWS + MOBA_WIDTH + SWA_KV_WIDTH + 2 * MLSTM_WIDTH + 2 * MLSTM_HEADS
NUM_GATES = 2 * MLSTM_HEADS

IN_PROJ_ROWS = 512
OUT_MLP_ROWS = 512
MLP_FF_CHUNK = 1024
SWA_ROWS = 512
MLSTM_ROWS = 256
MOBA_GROUP = 2
CONV_HALO = 16


def _dot(a, b):
    return jnp.dot(a, b, preferred_element_type=F32)


def _dot_nt(a, b):
    return lax.dot_general(a, b, (((1,), (1,)), ((), ())), preferred_element_type=F32)


def _split_bf16(v):
    hi = v.astype(BF16)
    lo = (v - hi.astype(F32)).astype(BF16)
    return hi, lo


def _vmem_limit(nbytes):
    return int(min(nbytes, VMEM_BYTES_V7X - 4 * 1024 * 1024))


def _const_spec(shape):
    nd = len(shape)
    return pl.BlockSpec(shape, lambda *_: (0,) * nd, pipeline_mode=pl.Buffered(1))


def _in_proj_kernel(x_ref, ln1_ref, wnat_ref, wtr_ref, bd_ref, gk_ref, cosn_ref, sinn_ref,
                    gq_ref, cost_ref, sint_ref, gbias_ref,
                    mk_ref, kmean_ref, sk_ref, xqk_ref, mqt_ref, sqt_ref, mvt_ref, svt_ref,
                    xvt_ref, xot_ref, gt_ref):
    tm = x_ref.shape[0]
    x = x_ref[...]
    ms = jnp.mean(x * x, axis=-1, keepdims=True)
    hn = (x * lax.rsqrt(ms + NORM_EPS) * ln1_ref[...]).astype(BF16)
    nat = _dot(hn, wnat_ref[...])
    tr = _dot_nt(wtr_ref[...], hn)

    kk = nat[:, :KN_WIDTH]
    hi, lo = _split_bf16(kk * kk)
    bd = bd_ref[...]
    bw = bd.shape[0]
    msk = jnp.concatenate(
        [_dot(hi[:, c:c + bw], bd) + _dot(lo[:, c:c + bw], bd) for c in range(0, KN_WIDTH, bw)],
        axis=1)
    kn = kk * lax.rsqrt(msk + NORM_EPS) * gk_ref[...]
    reps = KN_WIDTH // LANES_V7X
    cosn = jnp.concatenate([cosn_ref[...]] * reps, axis=1)
    sinn = jnp.concatenate([sinn_ref[...]] * reps, axis=1)
    lane = lax.broadcasted_iota(jnp.int32, kn.shape, 1)
    first_half = (lane % HEAD_DIM) < (HEAD_DIM // 2)
    swapped = jnp.where(first_half,
                        pltpu.roll(kn, KN_WIDTH - HEAD_DIM // 2, 1),
                        pltpu.roll(kn, HEAD_DIM // 2, 1))
    kr = kn * cosn + swapped * sinn
    mk = kr[:, :MOBA_WIDTH]
    mk_ref[...] = mk.astype(BF16)
    nblk = tm // MOBA_BLOCK
    kmean_ref[0] = jnp.concatenate(
        [jnp.mean(mk[c * MOBA_BLOCK:(c + 1) * MOBA_BLOCK], axis=0, keepdims=True) for c in range(nblk)],
        axis=0)
    sk_ref[...] = kr[:, MOBA_WIDTH:].astype(BF16)
    xqk_ref[...] = nat[:, KN_WIDTH:].astype(BF16)

    cost = cost_ref[...]
    sint = sint_ref[...]
    gq = jnp.concatenate([gq_ref[...]] * (tm // LANES_V7X), axis=1)
    half = HEAD_DIM // 2
    for h in range(QT_ROWS // HEAD_DIM):
        blk = tr[h * HEAD_DIM:(h + 1) * HEAD_DIM]
        is_swa = h >= MOBA_HEADS
        gain = gq[HEAD_DIM:] if is_swa else gq[:HEAD_DIM]
        msq = jnp.mean(blk * blk, axis=0, keepdims=True)
        qn = blk * lax.rsqrt(msq + NORM_EPS) * gain
        x1, x2 = qn[:half], qn[half:]
        rot = jnp.concatenate([x1 * cost - x2 * sint, x2 * cost + x1 * sint], axis=0).astype(BF16)
        if is_swa:
            r0 = (h - MOBA_HEADS) * HEAD_DIM
            sqt_ref[r0:r0 + HEAD_DIM, :] = rot
        else:
            mqt_ref[h * HEAD_DIM:(h + 1) * HEAD_DIM, :] = rot
    r = QT_ROWS
    mv = tr[r:r + MOBA_WIDTH].astype(BF16)
    ones = jnp.ones((MOBA_V_ROWS - HEAD_DIM, MOBA_BLOCK), BF16)
    for c in range(nblk):
        for h in range(MOBA_HEADS):
            r0 = h * MOBA_V_ROWS
            mvt_ref[c, r0:r0 + HEAD_DIM, :] = mv[h * HEAD_DIM:(h + 1) * HEAD_DIM,
                                                 c * MOBA_BLOCK:(c + 1) * MOBA_BLOCK]
            mvt_ref[c, r0 + HEAD_DIM:r0 + MOBA_V_ROWS, :] = ones
    r += MOBA_WIDTH
    svt_ref[...] = tr[r:r + SWA_KV_WIDTH].astype(BF16)
    r += SWA_KV_WIDTH
    xvt_ref[...] = tr[r:r + MLSTM_WIDTH].astype(BF16)
    r += MLSTM_WIDTH
    xot_ref[...] = tr[r:r + MLSTM_WIDTH].astype(BF16)
    r += MLSTM_WIDTH
    gbias = jnp.concatenate([gbias_ref[...]] * (tm // LANES_V7X), axis=1)
    gt_ref[...] = tr[r:r + NUM_GATES] + gbias


def _in_proj(xf, ln1, wnat, wtr, bd, gk, cosn, sinn, gq, cost, sint, gbias, seq):
    t = xf.shape[0]
    tm = IN_PROJ_ROWS
    steps = t // tm
    seq_steps = seq // tm
    nblk = tm // MOBA_BLOCK
    row = lambda w: pl.BlockSpec((tm, w), lambda i: (i, 0))
    col = lambda r: pl.BlockSpec((r, tm), lambda i: (0, i))
    in_specs = [
        row(D_MODEL),
        _const_spec((1, D_MODEL)),
        _const_spec(wnat.shape),
        _const_spec(wtr.shape),
        _const_spec(bd.shape),
        _const_spec(gk.shape),
        pl.BlockSpec((tm, LANES_V7X), lambda i: (i % seq_steps, 0)),
        pl.BlockSpec((tm, LANES_V7X), lambda i: (i % seq_steps, 0)),
        _const_spec(gq.shape),
        pl.BlockSpec((HEAD_DIM // 2, tm), lambda i: (0, i % seq_steps)),
        pl.BlockSpec((HEAD_DIM // 2, tm), lambda i: (0, i % seq_steps)),
        _const_spec(gbias.shape),
    ]
    out_shape = [
        jax.ShapeDtypeStruct((t, MOBA_WIDTH), BF16),
        jax.ShapeDtypeStruct((steps, nblk, MOBA_WIDTH), F32),
        jax.ShapeDtypeStruct((t, SWA_KV_WIDTH), BF16),
        jax.ShapeDtypeStruct((t, 2 * MLSTM_WIDTH), BF16),
        jax.ShapeDtypeStruct((MOBA_WIDTH, t), BF16),
        jax.ShapeDtypeStruct((SWA_Q_WIDTH, t), BF16),
        jax.ShapeDtypeStruct((t // MOBA_BLOCK, MOBA_HEADS * MOBA_V_ROWS, MOBA_BLOCK), BF16),
        jax.ShapeDtypeStruct((SWA_KV_WIDTH, t), BF16),
        jax.ShapeDtypeStruct((MLSTM_WIDTH, t), BF16),
        jax.ShapeDtypeStruct((MLSTM_WIDTH, t), BF16),
        jax.ShapeDtypeStruct((NUM_GATES, t), F32),
    ]
    out_specs = [
        row(MOBA_WIDTH),
        pl.BlockSpec((1, nblk, MOBA_WIDTH), lambda i: (i, 0, 0)),
        row(SWA_KV_WIDTH),
        row(2 * MLSTM_WIDTH),
        col(MOBA_WIDTH),
        col(SWA_Q_WIDTH),
        pl.BlockSpec((nblk, MOBA_HEADS * MOBA_V_ROWS, MOBA_BLOCK), lambda i: (i, 0, 0)),
        col(SWA_KV_WIDTH),
        col(MLSTM_WIDTH),
        col(MLSTM_WIDTH),
        col(NUM_GATES),
    ]
    weights = 2 * (wnat.size + wtr.size)
    tiles = tm * (2 * 4 * D_MODEL + 3 * 4 * (NAT_WIDTH + TR_ROWS) + 2 * 2 * (NAT_WIDTH + TR_ROWS))
    return pl.pallas_call(
        _in_proj_kernel,
        grid=(steps,),
        in_specs=in_specs,
        out_specs=out_specs,
        out_shape=out_shape,
        compiler_params=pltpu.CompilerParams(
            dimension_semantics=("arbitrary",),
            vmem_limit_bytes=_vmem_limit(2 * weights + tiles + (8 << 20))),
        name="in_proj",
    )(xf, ln1, wnat, wtr, bd, gk, cosn, sinn, gq, cost, sint, gbias)


def _moba_kernel(bounded_ref, qt_ref, k_ref, vt_ref, km_ref, o_ref, sel_ref, s_ref, p_ref):
    nb = k_ref.shape[0]
    lq = qt_ref.shape[1]
    nh = MOBA_HEADS
    i = pl.program_id(1)
    row = lax.broadcasted_iota(jnp.int32, (HEAD_PAIR, lq), 0)
    blk_id = lax.broadcasted_iota(jnp.int32, (nb, lq), 0)
    qh = []
    for h in range(nh):
        qt = qt_ref[(h // 2) * HEAD_PAIR:(h // 2 + 1) * HEAD_PAIR, :]
        keep = (row < HEAD_DIM) if h % 2 == 0 else (row >= HEAD_DIM)
        qh.append(jnp.where(keep, qt, jnp.zeros_like(qt)))

    def k_group(blk0, count, h):
        g = h // 2
        return k_ref[pl.ds(blk0, count), :, g * HEAD_PAIR:(g + 1) * HEAD_PAIR].reshape(count * lq, HEAD_PAIR)

    def v_aug(j, h):
        return vt_ref[j, h * MOBA_V_ROWS:(h + 1) * MOBA_V_ROWS, :]

    def select_blocks():
        for h in range(nh):
            g = h // 2
            km_hi, km_lo = _split_bf16(km_ref[:, g * HEAD_PAIR:(g + 1) * HEAD_PAIR])
            gate = _dot(km_hi, qh[h]) + _dot(km_lo, qh[h])
            gate = jnp.where(blk_id < i, gate, NEG_INF)
            sel = jnp.full((nb, lq), NEG_INF, F32)
            for _ in range(MOBA_TOPK):
                mx = jnp.max(gate, axis=0, keepdims=True)
                idx = jnp.min(jnp.where(gate == mx, blk_id, nb), axis=0, keepdims=True)
                pick = blk_id == jnp.where(idx < i, idx, nb)
                sel = jnp.where(pick, 0.0, sel)
                gate = jnp.where(pick, NEG_INF, gate)
            sel_ref[h] = sel

    kpos = lax.broadcasted_iota(jnp.int32, (lq, lq), 0)
    qpos = lax.broadcasted_iota(jnp.int32, (lq, lq), 1)
    causal = kpos <= qpos

    def weighted_values(st, mu, v_rows):
        pv = _dot(v_rows, jnp.exp2(st - mu).astype(BF16))
        return pv[HEAD_DIM:HEAD_DIM + 1], pv[:HEAD_DIM]

    def finish(nums, dens):
        for g in range(nh // 2):
            ot = jnp.concatenate([nums[2 * g] / dens[2 * g], nums[2 * g + 1] / dens[2 * g + 1]], axis=0)
            o_ref[:, g * HEAD_PAIR:(g + 1) * HEAD_PAIR] = ot.T.astype(BF16)

    trips = lax.shift_right_logical(i + (2 * MOBA_GROUP - 1), MOBA_GROUP.bit_length())
    units = nh * MOBA_GROUP
    own_slot = 2 * units

    @pl.when(bounded_ref[0] != 0)
    def _():
        for h in range(nh):
            st = jnp.where(causal, _dot(k_group(i, 1, h), qh[h]), NEG_INF)
            p_ref[own_slot + h] = jnp.exp2(st).astype(BF16)

        def stage_p(blk0, slot0):
            blk0 = jnp.minimum(blk0, nb - MOBA_GROUP)
            for h in range(nh):
                p = jnp.exp2(_dot(k_group(blk0, MOBA_GROUP, h), qh[h])).astype(BF16)
                for u in range(MOBA_GROUP):
                    p_ref[slot0 + nh * u + h] = p[u * lq:(u + 1) * lq]

        def stage_v(blk0, slot0, acc):
            acc = list(acc)
            for u in range(MOBA_GROUP):
                j = blk0 + u
                for h in range(nh):
                    pv = _dot(v_aug(j, h), p_ref[slot0 + nh * u + h])
                    acc[h] = acc[h] + jnp.exp2(sel_ref[h, pl.ds(j, 1), :]) * pv
            return acc

        def body(t, acc):
            blk = t * (2 * MOBA_GROUP)
            stage_p(blk + MOBA_GROUP, units)
            acc = stage_v(blk, 0, acc)
            stage_p(blk + 2 * MOBA_GROUP, 0)
            return tuple(stage_v(blk + MOBA_GROUP, units, acc))

        stage_p(0, 0)
        select_blocks()
        acc = [_dot(v_aug(i, h), p_ref[own_slot + h]) for h in range(nh)]
        acc = lax.fori_loop(0, trips, body, tuple(acc))
        finish([a[:HEAD_DIM] for a in acc], [a[HEAD_DIM:HEAD_DIM + 1] for a in acc])


    def stage_a(blk0, slot0):
        mus = []
        for u in range(MOBA_GROUP):
            for h in range(nh):
                st = _dot(k_group(jnp.minimum(blk0 + u, nb - 1), 1, h), qh[h])
                s_ref[slot0 + nh * u + h] = st
                mus.append(jnp.max(st, axis=0, keepdims=True))
        return mus

    def stage_b(blk0, slot0, mus, state):
        state = list(state)
        for u in range(MOBA_GROUP):
            j = blk0 + u
            for h in range(nh):
                mu = mus[nh * u + h]
                ls, pv = weighted_values(s_ref[slot0 + nh * u + h], mu, v_aug(j, h))
                m, l, acc = state[3 * h:3 * h + 3]
                mu_sel = mu + sel_ref[h, pl.ds(j, 1), :]
                m_new = jnp.maximum(m, mu_sel)
                f = jnp.exp2(mu_sel - m_new)
                c = jnp.exp2(m - m_new)
                state[3 * h:3 * h + 3] = [m_new, c * l + f * ls, c * acc + f * pv]
        return state

    def body(t, loop_carry):
        mus0, state = loop_carry[:units], loop_carry[units:]
        blk = t * (2 * MOBA_GROUP)
        mus1 = stage_a(blk + MOBA_GROUP, units)
        state = stage_b(blk, 0, mus0, state)
        mus0 = stage_a(blk + 2 * MOBA_GROUP, 0)
        state = stage_b(blk + MOBA_GROUP, units, mus1, state)
        return (*mus0, *state)

    @pl.when(bounded_ref[0] == 0)
    def _():
        state = []
        for h in range(nh):
            st = jnp.where(causal, _dot(k_group(i, 1, h), qh[h]), NEG_INF)
            mu = jnp.max(st, axis=0, keepdims=True)
            state += [mu, *weighted_values(st, mu, v_aug(i, h))]
        mus0 = stage_a(0, 0)
        select_blocks()
        state = lax.fori_loop(0, trips, body, (*mus0, *state))[units:]
        finish(state[2::3], state[1::3])


def _moba(bounded, mqt, mk, mvt, kmean, batch, seq):
    t = mk.shape[0]
    nb = seq // MOBA_BLOCK
    lq = MOBA_BLOCK
    k3 = mk.reshape(t // MOBA_BLOCK, MOBA_BLOCK, MOBA_WIDTH)
    assert nb % (2 * MOBA_GROUP) == 0
    units = MOBA_HEADS * MOBA_GROUP
    kv_bytes = 2 * nb * MOBA_BLOCK * (MOBA_WIDTH + MOBA_HEADS * MOBA_V_ROWS)
    scratch_bytes = lq * lq * (4 * 2 * units + 2 * (2 * units + MOBA_HEADS)) + 4 * MOBA_HEADS * nb * lq
    resident = lambda shape, imap: pl.BlockSpec(shape, imap, pipeline_mode=pl.Buffered(1))
    return pl.pallas_call(
        _moba_kernel,
        grid=(batch, nb),
        in_specs=[
            pl.BlockSpec(memory_space=pltpu.SMEM),
            pl.BlockSpec((MOBA_WIDTH, lq), lambda b, i: (0, b * nb + i)),
            resident((nb, MOBA_BLOCK, MOBA_WIDTH), lambda b, i: (b, 0, 0)),
            resident((nb, MOBA_HEADS * MOBA_V_ROWS, MOBA_BLOCK), lambda b, i: (b, 0, 0)),
            resident((nb, MOBA_WIDTH), lambda b, i: (b, 0)),
        ],
        out_specs=pl.BlockSpec((lq, MOBA_WIDTH), lambda b, i: (b * nb + i, 0)),
        out_shape=jax.ShapeDtypeStruct((t, MOBA_WIDTH), BF16),
        scratch_shapes=[pltpu.VMEM((MOBA_HEADS, nb, lq), F32),
                        pltpu.VMEM((2 * units, lq, lq), F32),
                        pltpu.VMEM((2 * units + MOBA_HEADS, lq, lq), BF16)],
        compiler_params=pltpu.CompilerParams(
            dimension_semantics=("arbitrary", "arbitrary"),
            vmem_limit_bytes=_vmem_limit(kv_bytes + scratch_bytes + (12 << 20))),
        name="moba",
    )(bounded, mqt, k3, mvt, kmean)


def _swa_kernel(sink_ref, qt_ref, k_ref, kh_ref, vt_ref, vth_ref, o_ref):
    tq = k_ref.shape[0]
    w = SWA_WINDOW
    i = pl.program_id(1)
    ncol = SWA_GROUP * w
    kpos = lax.broadcasted_iota(jnp.int32, (2 * w, ncol), 0)
    col = lax.broadcasted_iota(jnp.int32, (2 * w, ncol), 1)
    diff = w + (col % w) - kpos
    in_window = (diff >= 0) & (diff < w)
    zeros = jnp.zeros((HEAD_DIM, ncol), BF16)
    for r in range(tq // w):
        if r == 0:
            kcat = jnp.concatenate([kh_ref[...], k_ref[0:w, :]], axis=0)
            vcat = jnp.concatenate([vth_ref[...], vt_ref[:, 0:w]], axis=1)
            mask = in_window & ((kpos >= w) | (i > 0))
        else:
            kcat = k_ref[(r - 1) * w:(r + 1) * w, :]
            vcat = vt_ref[:, (r - 1) * w:(r + 1) * w]
            mask = in_window
        pieces = []
        for g in range(SWA_KV_HEADS):
            qs = jnp.concatenate(
                [qt_ref[(SWA_GROUP * g + a) * HEAD_DIM:(SWA_GROUP * g + a + 1) * HEAD_DIM, r * w:(r + 1) * w]
                 for a in range(SWA_GROUP)], axis=1)
            qpad = jnp.concatenate([qs, zeros] if g == 0 else [zeros, qs], axis=0)
            st = jnp.where(mask, _dot(kcat, qpad), NEG_INF)
            sink = jnp.concatenate(
                [jnp.full((1, w), sink_ref[SWA_GROUP * g + a], F32) for a in range(SWA_GROUP)], axis=1)
            m = jnp.maximum(jnp.max(st, axis=0, keepdims=True), sink)
            p = jnp.exp(st - m)
            l = jnp.sum(p, axis=0, keepdims=True) + jnp.exp(sink - m)
            ot = _dot(vcat[g * HEAD_DIM:(g + 1) * HEAD_DIM, :], p.astype(BF16)) / l
            pieces += [ot[:, a * w:(a + 1) * w] for a in range(SWA_GROUP)]
        o_ref[r * w:(r + 1) * w, :] = jnp.concatenate(pieces, axis=0).T.astype(BF16)


def _swa(sinks, sqt, sk, svt, batch, seq):
    t = sk.shape[0]
    tq = SWA_ROWS
    w = SWA_WINDOW
    steps = seq // tq
    halo = lambda b, i: jnp.maximum((b * seq + i * tq) // w - 1, 0)
    return pl.pallas_call(
        _swa_kernel,
        grid=(batch, steps),
        in_specs=[
            pl.BlockSpec(memory_space=pltpu.SMEM),
            pl.BlockSpec((SWA_Q_WIDTH, tq), lambda b, i: (0, b * steps + i)),
            pl.BlockSpec((tq, SWA_KV_WIDTH), lambda b, i: (b * steps + i, 0)),
            pl.BlockSpec((w, SWA_KV_WIDTH), lambda b, i: (halo(b, i), 0)),
            pl.BlockSpec((SWA_KV_WIDTH, tq), lambda b, i: (0, b * steps + i)),
            pl.BlockSpec((SWA_KV_WIDTH, w), lambda b, i: (0, halo(b, i))),
        ],
        out_specs=pl.BlockSpec((tq, SWA_Q_WIDTH), lambda b, i: (b * steps + i, 0)),
        out_shape=jax.ShapeDtypeStruct((t, SWA_Q_WIDTH), BF16),
        compiler_params=pltpu.CompilerParams(dimension_semantics=("arbitrary", "arbitrary")),
        name="swa",
    )(sinks, sqt, sk, sk, svt, svt)


def _log_sigmoid(v):
    return jnp.minimum(v, 0.0) - jnp.log1p(jnp.exp(-jnp.abs(v)))


def _mlstm_kernel(xqk_ref, halo_ref, xvt_ref, xot_ref, gt_ref, gc_ref, convw_ref, convb_ref,
                  gain_ref, cum_ref, cumt_ref, o_ref, ext_ref, c_ref, n_ref, m_ref):
    tt = xqk_ref.shape[0]
    lc = MLSTM_CHUNK
    i = pl.program_id(1)

    @pl.when(i == 0)
    def _():
        c_ref[...] = jnp.zeros_like(c_ref)
        n_ref[...] = jnp.zeros_like(n_ref)
        m_ref[...] = jnp.zeros_like(m_ref)

    hist = halo_ref[...].astype(F32)
    ext_ref[0:CONV_HALO, :] = jnp.where(i > 0, hist, jnp.zeros_like(hist))
    ext_ref[CONV_HALO:CONV_HALO + tt, :] = xqk_ref[...].astype(F32)
    convw = convw_ref[...]
    conv = jnp.zeros((tt, 2 * MLSTM_WIDTH), F32) + convb_ref[...]
    for j in range(CONV_WIDTH):
        start = CONV_HALO - (CONV_WIDTH - 1) + j
        conv = conv + convw[j:j + 1, :] * ext_ref[start:start + tt, :]
    qk = conv * jax.nn.sigmoid(conv)
    q_all = qk[:, :MLSTM_WIDTH]
    k_all = qk[:, MLSTM_WIDTH:] * SM_SCALE

    g_row = gt_ref[...]
    g_col = gc_ref[...]
    cum = cum_ref[...]
    b_row = jnp.dot(_log_sigmoid(g_row), cum, preferred_element_type=F32,
                    precision=lax.Precision.HIGHEST)
    b_col = jnp.dot(cumt_ref[...], _log_sigmoid(g_col), preferred_element_type=F32,
                    precision=lax.Precision.HIGHEST)
    u_col = b_col[:, MLSTM_HEADS:] - g_col[:, :MLSTM_HEADS]

    lane = lax.broadcasted_iota(jnp.int32, (lc, HEAD_PAIR), 1)
    s_idx = lax.broadcasted_iota(jnp.int32, (lc, lc), 0)
    t_idx = lax.broadcasted_iota(jnp.int32, (lc, lc), 1)
    causal = s_idx <= t_idx
    ht_rows = []
    for h in range(MLSTM_HEADS):
        p, hh = divmod(h, 2)
        in_head = (lane >= hh * HEAD_DIM) & (lane < (hh + 1) * HEAD_DIM)
        c_state = c_ref[h]
        n_state = n_ref[h]
        m_state = m_ref[h][0:1, 0:lc]
        ht_chunks = []
        for c in range(tt // lc):
            sl = slice(c * lc, (c + 1) * lc)
            q_pair = q_all[sl, p * HEAD_PAIR:(p + 1) * HEAD_PAIR]
            qm = jnp.where(in_head, q_pair, 0.0).astype(BF16)
            kp = k_all[sl, p * HEAD_PAIR:(p + 1) * HEAD_PAIR].astype(BF16)
            vt = xvt_ref[h * HEAD_DIM:(h + 1) * HEAD_DIM, sl]
            br = b_row[MLSTM_HEADS + h:MLSTM_HEADS + h + 1, sl]
            ir = g_row[h:h + 1, sl]
            uc = u_col[sl, h:h + 1]
            a = br[:, lc - 1:lc]
            dt = jnp.where(causal, br - uc, NEG_INF)
            inter = br + m_state
            m_t = jnp.maximum(inter, jnp.max(dt, axis=0, keepdims=True))
            w_intra = jnp.exp(dt - m_t)
            w_inter = jnp.exp(inter - m_t)
            qkt = _dot_nt(kp, qm) * w_intra
            num = w_inter * _dot_nt(c_state.astype(BF16), qm) + _dot(vt, qkt.astype(BF16))
            nq = _dot_nt(n_state.astype(BF16), qm)[0:1, :]
            den = w_inter * nq + jnp.sum(qkt, axis=0, keepdims=True)
            ht_chunks.append(num / jnp.maximum(jnp.abs(den), jnp.exp(-m_t)))
            g_end = a - br + ir
            m_new = jnp.maximum(a + m_state, jnp.max(g_end, axis=1, keepdims=True))
            w_s = jnp.exp(g_end - m_new)
            decay = jnp.exp(a + m_state - m_new)
            vtw = (vt.astype(F32) * w_s).astype(BF16)
            decay_col = decay[:, 0:1]
            c_state = decay_col * c_state + jnp.where(in_head, _dot(vtw, kp), 0.0)
            w_rows = jnp.broadcast_to(w_s, (8, lc)).astype(BF16)
            n_state = decay_col * n_state + jnp.where(in_head[0:8], _dot(w_rows, kp), 0.0)
            m_state = m_new + jnp.zeros_like(m_state)
        c_ref[h] = c_state
        n_ref[h] = n_state
        m_ref[h] = jnp.broadcast_to(m_state[:, 0:1], m_ref.shape[1:])
        ht_rows.append(jnp.concatenate(ht_chunks, axis=1))

    gain = jnp.concatenate([gain_ref[...]] * (tt // LANES_V7X), axis=1)
    outs = []
    for h in range(MLSTM_HEADS):
        rows = slice(h * HEAD_DIM, (h + 1) * HEAD_DIM)
        hg = ht_rows[h] * jax.nn.sigmoid(xot_ref[rows, :].astype(F32))
        msq = jnp.mean(hg * hg, axis=0, keepdims=True)
        outs.append(hg * lax.rsqrt(msq + NORM_EPS) * gain[rows])
    o_ref[...] = jnp.concatenate(outs, axis=0).T.astype(BF16)


def _mlstm(xqk, xvt, xot, gt, gc, convw, convb, gain, cum, batch, seq):
    t = xqk.shape[0]
    tt = MLSTM_ROWS
    steps = seq // tt
    col = lambda r: pl.BlockSpec((r, tt), lambda b, i: (0, b * steps + i))
    return pl.pallas_call(
        _mlstm_kernel,
        grid=(batch, steps),
        in_specs=[
            pl.BlockSpec((tt, 2 * MLSTM_WIDTH), lambda b, i: (b * steps + i, 0)),
            pl.BlockSpec((CONV_HALO, 2 * MLSTM_WIDTH),
                         lambda b, i: (jnp.maximum((b * seq + i * tt) // CONV_HALO - 1, 0), 0)),
            col(MLSTM_WIDTH),
            col(MLSTM_WIDTH),
            col(NUM_GATES),
            pl.BlockSpec((tt, NUM_GATES), lambda b, i: (b * steps + i, 0)),
            _const_spec(convw.shape),
            _const_spec(convb.shape),
            _const_spec(gain.shape),
            _const_spec(cum.shape),
            _const_spec(cum.shape),
        ],
        out_specs=pl.BlockSpec((tt, MLSTM_WIDTH), lambda b, i: (b * steps + i, 0)),
        out_shape=jax.ShapeDtypeStruct((t, MLSTM_WIDTH), BF16),
        scratch_shapes=[
            pltpu.VMEM((CONV_HALO + tt, 2 * MLSTM_WIDTH), F32),
            pltpu.VMEM((MLSTM_HEADS, HEAD_DIM, HEAD_PAIR), F32),
            pltpu.VMEM((MLSTM_HEADS, 8, HEAD_PAIR), F32),
            pltpu.VMEM((MLSTM_HEADS, 8, HEAD_PAIR), F32),
        ],
        compiler_params=pltpu.CompilerParams(dimension_semantics=("arbitrary", "arbitrary")),
        name="mlstm",
    )(xqk, xqk, xvt, xot, gt, gc, convw, convb, gain, cum, cum.T)


def _out_mlp_kernel(x_ref, ym_ref, yl_ref, ys_ref, wom_ref, wol_ref, wos_ref, ln2_ref,
                    wup_ref, wdn_ref, o_ref):
    x1 = (x_ref[...] + _dot(ym_ref[...], wom_ref[...]) + _dot(yl_ref[...], wol_ref[...])
          + _dot(ys_ref[...], wos_ref[...]))
    ms = jnp.mean(x1 * x1, axis=-1, keepdims=True)
    hn = (x1 * lax.rsqrt(ms + NORM_EPS) * ln2_ref[...]).astype(BF16)
    o_ref[...] = x1
    for c in range(0, D_FF, MLP_FF_CHUNK):
        u = _dot(hn, wup_ref[:, c:c + MLP_FF_CHUNK])
        act = jnp.square(jnp.maximum(u, 0.0)).astype(BF16)
        o_ref[...] += _dot(act, wdn_ref[c:c + MLP_FF_CHUNK, :])


def _out_mlp(xf, ym, yl, ys, wom, wol, wos, ln2, wup, wdn):
    t = xf.shape[0]
    tm = OUT_MLP_ROWS
    row = lambda w: pl.BlockSpec((tm, w), lambda i: (i, 0))
    weights = 2 * (wom.size + wol.size + wos.size + wup.size + wdn.size)
    tiles = tm * (4 * 4 * D_MODEL + 2 * 2 * D_MODEL + 4 * 4 * D_MODEL + 6 * MLP_FF_CHUNK)
    return pl.pallas_call(
        _out_mlp_kernel,
        grid=(t // tm,),
        in_specs=[
            row(D_MODEL), row(MOBA_WIDTH), row(MLSTM_WIDTH), row(SWA_Q_WIDTH),
            _const_spec(wom.shape), _const_spec(wol.shape), _const_spec(wos.shape),
            _const_spec(ln2.shape), _const_spec(wup.shape), _const_spec(wdn.shape),
        ],
        out_specs=row(D_MODEL),
        out_shape=jax.ShapeDtypeStruct((t, D_MODEL), F32),
        compiler_params=pltpu.CompilerParams(
            dimension_semantics=("arbitrary",),
            vmem_limit_bytes=_vmem_limit(weights + tiles + (8 << 20))),
        name="out_mlp",
    )(xf, ym, yl, ys, wom, wol, wos, ln2, wup, wdn)


def _rope_tables(seq):
    inv = ROPE_THETA ** (-jnp.arange(0, HEAD_DIM, 2, dtype=F32) / HEAD_DIM)
    ang = jnp.arange(seq, dtype=F32)[:, None] * inv[None, :]
    cos, sin = jnp.cos(ang), jnp.sin(ang)
    cosn = jnp.concatenate([cos, cos, cos, cos], axis=1)
    sinn = jnp.concatenate([-sin, sin, -sin, sin], axis=1)
    return cosn, sinn, cos.T, sin.T


def _layer(xf, tables, consts, batch, seq, ln1, w_in, conv_w, conv_b, igate_b, fgate_b, mlstm_norm,
           moba_q_norm, moba_k_norm, swa_q_norm, swa_k_norm, swa_sinks, w_out, ln2, w_up, w_down):
    cosn, sinn, cost, sint = tables
    bd, cum = consts
    o = 0
    cols = {}
    for name, width in (("mq", MOBA_WIDTH), ("mk", MOBA_WIDTH), ("mv", MOBA_WIDTH), ("sq", SWA_Q_WIDTH),
                        ("sk", SWA_KV_WIDTH), ("sv", SWA_KV_WIDTH), ("xqk", 2 * MLSTM_WIDTH),
                        ("xv", MLSTM_WIDTH), ("xo", MLSTM_WIDTH), ("xi", MLSTM_HEADS), ("xf", MLSTM_HEADS)):
        cols[name] = w_in[:, o:o + width]
        o += width
    wnat = jnp.concatenate([cols["mk"], cols["sk"], cols["xqk"]], axis=1).astype(BF16)
    wtr = jnp.concatenate([cols["mq"], cols["sq"], cols["mv"], cols["sv"], cols["xv"], cols["xo"],
                           cols["xi"], cols["xf"]], axis=1).T.astype(BF16)
    gk = jnp.concatenate([jnp.tile(moba_k_norm, MOBA_HEADS), jnp.tile(swa_k_norm, SWA_KV_HEADS)])[None, :]
    gq = jnp.broadcast_to(
        jnp.concatenate([moba_q_norm * (SM_SCALE * LOG2E), swa_q_norm * SM_SCALE])[:, None],
        (2 * HEAD_DIM, LANES_V7X))
    gbias = jnp.broadcast_to(jnp.concatenate([igate_b, fgate_b])[:, None], (NUM_GATES, LANES_V7X))

    (mk, kmean, sk, xqk, mqt, sqt, mvt, svt, xvt, xot, gt) = _in_proj(
        xf, ln1[None, :], wnat, wtr, bd, gk, cosn, sinn, gq, cost, sint, gbias, seq)

    kmean = kmean.reshape(-1, MOBA_WIDTH)
    score_bound = (HEAD_DIM * SM_SCALE * LOG2E) * jnp.max(jnp.abs(moba_q_norm)) * jnp.max(jnp.abs(moba_k_norm))
    bounded = (score_bound <= MOBA_SAFE_LOG2).astype(jnp.int32).reshape(1)
    ym = _moba(bounded, mqt, mk, mvt, kmean, batch, seq)
    ys = _swa(swa_sinks, sqt, sk, svt, batch, seq)
    gain = jnp.broadcast_to(mlstm_norm.reshape(MLSTM_WIDTH, 1), (MLSTM_WIDTH, LANES_V7X))
    yl = _mlstm(xqk, xvt, xot, gt, gt.T, conv_w, conv_b[None, :], gain, cum, batch, seq)

    wo = w_out.astype(BF16)
    return _out_mlp(xf, ym, yl, ys, wo[:MOBA_WIDTH], wo[MOBA_WIDTH:MOBA_WIDTH + MLSTM_WIDTH],
                    wo[MOBA_WIDTH + MLSTM_WIDTH:], ln2[None, :], w_up.astype(BF16), w_down.astype(BF16))


def kernel(x, ln1, w_in, conv_w, conv_b, igate_b, fgate_b, mlstm_norm, moba_q_norm, moba_k_norm,
           swa_q_norm, swa_k_norm, swa_sinks, w_out, ln2, w_up, w_down):
    batch, seq, d = x.shape
    assert d == D_MODEL and seq % max(IN_PROJ_ROWS, SWA_ROWS, MLSTM_ROWS, MOBA_BLOCK) == 0
    depth = ln1.shape[0]
    tables = _rope_tables(seq)
    bw = 2 * LANES_V7X
    ids = jnp.arange(bw) // HEAD_DIM
    bd = jnp.where(ids[:, None] == ids[None, :], 1.0 / HEAD_DIM, 0.0).astype(BF16)
    tids = jnp.arange(MLSTM_ROWS)
    cum = ((tids[:, None] // MLSTM_CHUNK == tids[None, :] // MLSTM_CHUNK)
           & (tids[:, None] <= tids[None, :])).astype(F32)
    xf = x.reshape(batch * seq, d)
    for l in range(depth):
        xf = _layer(xf, tables, (bd, cum), batch, seq, ln1[l], w_in[l], conv_w[l], conv_b[l], igate_b[l],
                    fgate_b[l], mlstm_norm[l], moba_q_norm[l], moba_k_norm[l], swa_q_norm[l],
                    swa_k_norm[l], swa_sinks[l], w_out[l], ln2[l], w_up[l], w_down[l])
    return xf.reshape(batch, seq, d)
```

```python
import functools

import jax
import jax.numpy as jnp
from jax import lax
from jax.experimental import pallas as pl
from jax.experimental.pallas import tpu as pltpu

F32 = jnp.float32
BF16 = jnp.bfloat16
NEG_INF = float("-inf")

D_MODEL = 1024
HEAD_DIM = 64
MOBA_HEADS = 6
MLSTM_HEADS = 4
SWA_Q_HEADS = 6
SWA_KV_HEADS = 2
SWA_GROUP = SWA_Q_HEADS // SWA_KV_HEADS
MOBA_WIDTH = MOBA_HEADS * HEAD_DIM
MLSTM_WIDTH = MLSTM_HEADS * HEAD_DIM
SWA_Q_WIDTH = SWA_Q_HEADS * HEAD_DIM
SWA_KV_WIDTH = SWA_KV_HEADS * HEAD_DIM
MOBA_BLOCK = 256
MOBA_TOPK = 3
MLSTM_CHUNK = 256
CONV_WIDTH = 4
SWA_WINDOW = 128
ROPE_THETA = 10000.0
D_FF = 4 * D_MODEL
NORM_EPS = 1e-6
SM_SCALE = HEAD_DIM ** -0.5
LOG2E = 1.4426950408889634
MOBA_V_ROWS = HEAD_DIM + 16
MOBA_SAFE_LOG2 = 60.0

LANES_V7X = 128
VMEM_BYTES_V7X = 64 * 1024 * 1024
HEAD_PAIR = 2 * HEAD_DIM
assert HEAD_PAIR == LANES_V7X

NAT_WIDTH = MOBA_WIDTH + SWA_KV_WIDTH + 2 * MLSTM_WIDTH
KN_WIDTH = MOBA_WIDTH + SWA_KV_WIDTH
QT_ROWS = MOBA_WIDTH + SWA_Q_WIDTH
TR_ROWS = QT_ROWS + MOBA_WIDTH + SWA_KV_WIDTH + 2 * MLSTM_WIDTH + 2 * MLSTM_HEADS
NUM_GATES = 2 * MLSTM_HEADS

IN_PROJ_ROWS = 512
OUT_MLP_ROWS = 512
MLP_FF_CHUNK = 1024
SWA_ROWS = 512
MLSTM_ROWS = 256
MOBA_GROUP = 2
CONV_HALO = 16


def _dot(a, b):
    return jnp.dot(a, b, preferred_element_type=F32)


def _dot_nt(a, b):
    return lax.dot_general(a, b, (((1,), (1,)), ((), ())), preferred_element_type=F32)


def _split_bf16(v):
    hi = v.astype(BF16)
    lo = (v - hi.astype(F32)).astype(BF16)
    return hi, lo


def _split3_bf16(v):
    hi = v.astype(BF16)
    rest = v - hi.astype(F32)
    mid = rest.astype(BF16)
    return hi, mid, (rest - mid.astype(F32)).astype(BF16)


def _vmem_limit(nbytes):
    return int(min(nbytes, VMEM_BYTES_V7X - 4 * 1024 * 1024))


def _const_spec(shape):
    nd = len(shape)
    return pl.BlockSpec(shape, lambda *_: (0,) * nd, pipeline_mode=pl.Buffered(1))


def _in_proj_kernel(x_ref, ln1_ref, wnat_ref, wtr_ref, bd_ref, gk_ref, cosn_ref, sinn_ref,
                    gq_ref, cost_ref, sint_ref, gbias_ref,
                    mk_ref, kmean_ref, sk_ref, xqk_ref, mqt_ref, sqt_ref, mvt_ref, svt_ref,
                    xvt_ref, xot_ref, gt_ref):
    tm = x_ref.shape[0]
    x = x_ref[...]
    ms = jnp.mean(x * x, axis=-1, keepdims=True)
    hn = (x * lax.rsqrt(ms + NORM_EPS) * ln1_ref[...]).astype(BF16)
    nat = _dot(hn, wnat_ref[...])
    tr = _dot_nt(wtr_ref[...], hn)

    kk = nat[:, :KN_WIDTH]
    hi, lo = _split_bf16(kk * kk)
    bd = bd_ref[...]
    bw = bd.shape[0]
    msk = jnp.concatenate(
        [_dot(hi[:, c:c + bw], bd) + _dot(lo[:, c:c + bw], bd) for c in range(0, KN_WIDTH, bw)],
        axis=1)
    kn = kk * lax.rsqrt(msk + NORM_EPS) * gk_ref[...]
    reps = KN_WIDTH // LANES_V7X
    cosn = jnp.concatenate([cosn_ref[...]] * reps, axis=1)
    sinn = jnp.concatenate([sinn_ref[...]] * reps, axis=1)
    lane = lax.broadcasted_iota(jnp.int32, kn.shape, 1)
    first_half = (lane % HEAD_DIM) < (HEAD_DIM // 2)
    swapped = jnp.where(first_half,
                        pltpu.roll(kn, KN_WIDTH - HEAD_DIM // 2, 1),
                        pltpu.roll(kn, HEAD_DIM // 2, 1))
    kr = kn * cosn + swapped * sinn
    mk = kr[:, :MOBA_WIDTH]
    mk_ref[...] = mk.astype(BF16)
    nblk = tm // MOBA_BLOCK
    kmean_ref[0] = jnp.concatenate(
        [jnp.mean(mk[c * MOBA_BLOCK:(c + 1) * MOBA_BLOCK], axis=0, keepdims=True) for c in range(nblk)],
        axis=0)
    sk_ref[...] = kr[:, MOBA_WIDTH:].astype(BF16)
    xqk_ref[...] = nat[:, KN_WIDTH:].astype(BF16)

    cost = cost_ref[...]
    sint = sint_ref[...]
    gq = jnp.concatenate([gq_ref[...]] * (tm // LANES_V7X), axis=1)
    half = HEAD_DIM // 2
    for h in range(QT_ROWS // HEAD_DIM):
        blk = tr[h * HEAD_DIM:(h + 1) * HEAD_DIM]
        is_swa = h >= MOBA_HEADS
        gain = gq[HEAD_DIM:] if is_swa else gq[:HEAD_DIM]
        msq = jnp.mean(blk * blk, axis=0, keepdims=True)
        qn = blk * lax.rsqrt(msq + NORM_EPS) * gain
        x1, x2 = qn[:half], qn[half:]
        rot = jnp.concatenate([x1 * cost - x2 * sint, x2 * cost + x1 * sint], axis=0).astype(BF16)
        if is_swa:
            r0 = (h - MOBA_HEADS) * HEAD_DIM
            sqt_ref[r0:r0 + HEAD_DIM, :] = rot
        else:
            mqt_ref[h * HEAD_DIM:(h + 1) * HEAD_DIM, :] = rot
    r = QT_ROWS
    mv = tr[r:r + MOBA_WIDTH].astype(BF16)
    ones = jnp.ones((MOBA_V_ROWS - HEAD_DIM, MOBA_BLOCK), BF16)
    for c in range(nblk):
        for h in range(MOBA_HEADS):
            r0 = h * MOBA_V_ROWS
            mvt_ref[c, r0:r0 + HEAD_DIM, :] = mv[h * HEAD_DIM:(h + 1) * HEAD_DIM,
                                                 c * MOBA_BLOCK:(c + 1) * MOBA_BLOCK]
            mvt_ref[c, r0 + HEAD_DIM:r0 + MOBA_V_ROWS, :] = ones
    r += MOBA_WIDTH
    svt_ref[...] = tr[r:r + SWA_KV_WIDTH].astype(BF16)
    r += SWA_KV_WIDTH
    xvt_ref[...] = tr[r:r + MLSTM_WIDTH].astype(BF16)
    r += MLSTM_WIDTH
    xot_ref[...] = tr[r:r + MLSTM_WIDTH].astype(BF16)
    r += MLSTM_WIDTH
    gbias = jnp.concatenate([gbias_ref[...]] * (tm // LANES_V7X), axis=1)
    gt_ref[...] = tr[r:r + NUM_GATES] + gbias


def _in_proj(xf, ln1, wnat, wtr, bd, gk, cosn, sinn, gq, cost, sint, gbias, seq):
    t = xf.shape[0]
    tm = IN_PROJ_ROWS
    steps = t // tm
    seq_steps = seq // tm
    nblk = tm // MOBA_BLOCK
    row = lambda w: pl.BlockSpec((tm, w), lambda i: (i, 0))
    col = lambda r: pl.BlockSpec((r, tm), lambda i: (0, i))
    in_specs = [
        row(D_MODEL),
        _const_spec((1, D_MODEL)),
        _const_spec(wnat.shape),
        _const_spec(wtr.shape),
        _const_spec(bd.shape),
        _const_spec(gk.shape),
        pl.BlockSpec((tm, LANES_V7X), lambda i: (i % seq_steps, 0)),
        pl.BlockSpec((tm, LANES_V7X), lambda i: (i % seq_steps, 0)),
        _const_spec(gq.shape),
        pl.BlockSpec((HEAD_DIM // 2, tm), lambda i: (0, i % seq_steps)),
        pl.BlockSpec((HEAD_DIM // 2, tm), lambda i: (0, i % seq_steps)),
        _const_spec(gbias.shape),
    ]
    out_shape = [
        jax.ShapeDtypeStruct((t, MOBA_WIDTH), BF16),
        jax.ShapeDtypeStruct((steps, nblk, MOBA_WIDTH), F32),
        jax.ShapeDtypeStruct((t, SWA_KV_WIDTH), BF16),
        jax.ShapeDtypeStruct((t, 2 * MLSTM_WIDTH), BF16),
        jax.ShapeDtypeStruct((MOBA_WIDTH, t), BF16),
        jax.ShapeDtypeStruct((SWA_Q_WIDTH, t), BF16),
        jax.ShapeDtypeStruct((t // MOBA_BLOCK, MOBA_HEADS * MOBA_V_ROWS, MOBA_BLOCK), BF16),
        jax.ShapeDtypeStruct((SWA_KV_WIDTH, t), BF16),
        jax.ShapeDtypeStruct((MLSTM_WIDTH, t), BF16),
        jax.ShapeDtypeStruct((MLSTM_WIDTH, t), BF16),
        jax.ShapeDtypeStruct((NUM_GATES, t), F32),
    ]
    out_specs = [
        row(MOBA_WIDTH),
        pl.BlockSpec((1, nblk, MOBA_WIDTH), lambda i: (i, 0, 0)),
        row(SWA_KV_WIDTH),
        row(2 * MLSTM_WIDTH),
        col(MOBA_WIDTH),
        col(SWA_Q_WIDTH),
        pl.BlockSpec((nblk, MOBA_HEADS * MOBA_V_ROWS, MOBA_BLOCK), lambda i: (i, 0, 0)),
        col(SWA_KV_WIDTH),
        col(MLSTM_WIDTH),
        col(MLSTM_WIDTH),
        col(NUM_GATES),
    ]
    weights = 2 * (wnat.size + wtr.size)
    tiles = tm * (2 * 4 * D_MODEL + 3 * 4 * (NAT_WIDTH + TR_ROWS) + 2 * 2 * (NAT_WIDTH + TR_ROWS))
    return pl.pallas_call(
        _in_proj_kernel,
        grid=(steps,),
        in_specs=in_specs,
        out_specs=out_specs,
        out_shape=out_shape,
        compiler_params=pltpu.CompilerParams(
            dimension_semantics=("arbitrary",),
            vmem_limit_bytes=_vmem_limit(2 * weights + tiles + (8 << 20))),
        name="in_proj",
    )(xf, ln1, wnat, wtr, bd, gk, cosn, sinn, gq, cost, sint, gbias)


def _moba_kernel(bounded_ref, qt_ref, k_ref, vt_ref, km_ref, o_ref, sel_ref, s_ref, p_ref):
    nb = k_ref.shape[0]
    lq = qt_ref.shape[1]
    nh = MOBA_HEADS
    i = pl.program_id(1)
    row = lax.broadcasted_iota(jnp.int32, (HEAD_PAIR, lq), 0)
    blk_id = lax.broadcasted_iota(jnp.int32, (nb, lq), 0)
    qh = []
    for h in range(nh):
        qt = qt_ref[(h // 2) * HEAD_PAIR:(h // 2 + 1) * HEAD_PAIR, :]
        keep = (row < HEAD_DIM) if h % 2 == 0 else (row >= HEAD_DIM)
        qh.append(jnp.where(keep, qt, jnp.zeros_like(qt)))

    def k_group(blk0, count, h):
        g = h // 2
        return k_ref[pl.ds(blk0, count), :, g * HEAD_PAIR:(g + 1) * HEAD_PAIR].reshape(count * lq, HEAD_PAIR)

    def v_aug(j, h):
        return vt_ref[j, h * MOBA_V_ROWS:(h + 1) * MOBA_V_ROWS, :]

    def select_blocks():
        for h in range(nh):
            g = h // 2
            km_hi, km_lo = _split_bf16(km_ref[:, g * HEAD_PAIR:(g + 1) * HEAD_PAIR])
            gate = _dot(km_hi, qh[h]) + _dot(km_lo, qh[h])
            gate = jnp.where(blk_id < i, gate, NEG_INF)
            sel = jnp.full((nb, lq), NEG_INF, F32)
            for _ in range(MOBA_TOPK):
                mx = jnp.max(gate, axis=0, keepdims=True)
                idx = jnp.min(jnp.where(gate == mx, blk_id, nb), axis=0, keepdims=True)
                pick = blk_id == jnp.where(idx < i, idx, nb)
                sel = jnp.where(pick, 0.0, sel)
                gate = jnp.where(pick, NEG_INF, gate)
            sel_ref[h] = sel

    kpos = lax.broadcasted_iota(jnp.int32, (lq, lq), 0)
    qpos = lax.broadcasted_iota(jnp.int32, (lq, lq), 1)
    causal = kpos <= qpos

    def weighted_values(st, mu, v_rows):
        pv = _dot(v_rows, jnp.exp2(st - mu).astype(BF16))
        return pv[HEAD_DIM:HEAD_DIM + 1], pv[:HEAD_DIM]

    def finish(nums, dens):
        for g in range(nh // 2):
            ot = jnp.concatenate([nums[2 * g] / dens[2 * g], nums[2 * g + 1] / dens[2 * g + 1]], axis=0)
            o_ref[:, g * HEAD_PAIR:(g + 1) * HEAD_PAIR] = ot.T.astype(BF16)

    trips = lax.shift_right_logical(i + (2 * MOBA_GROUP - 1), MOBA_GROUP.bit_length())
    units = nh * MOBA_GROUP
    own_slot = 2 * units

    @pl.when(bounded_ref[0] != 0)
    def _():
        for h in range(nh):
            st = jnp.where(causal, _dot(k_group(i, 1, h), qh[h]), NEG_INF)
            p_ref[own_slot + h] = jnp.exp2(st).astype(BF16)

        def stage_p(blk0, slot0):
            blk0 = jnp.minimum(blk0, nb - MOBA_GROUP)
            for h in range(nh):
                p = jnp.exp2(_dot(k_group(blk0, MOBA_GROUP, h), qh[h])).astype(BF16)
                for u in range(MOBA_GROUP):
                    p_ref[slot0 + nh * u + h] = p[u * lq:(u + 1) * lq]

        def stage_v(blk0, slot0, acc):
            acc = list(acc)
            for u in range(MOBA_GROUP):
                j = blk0 + u
                for h in range(nh):
                    pv = _dot(v_aug(j, h), p_ref[slot0 + nh * u + h])
                    acc[h] = acc[h] + jnp.exp2(sel_ref[h, pl.ds(j, 1), :]) * pv
            return acc

        def body(t, acc):
            blk = t * (2 * MOBA_GROUP)
            stage_p(blk + MOBA_GROUP, units)
            acc = stage_v(blk, 0, acc)
            stage_p(blk + 2 * MOBA_GROUP, 0)
            return tuple(stage_v(blk + MOBA_GROUP, units, acc))

        stage_p(0, 0)
        select_blocks()
        acc = [_dot(v_aug(i, h), p_ref[own_slot + h]) for h in range(nh)]
        acc = lax.fori_loop(0, trips, body, tuple(acc))
        finish([a[:HEAD_DIM] for a in acc], [a[HEAD_DIM:HEAD_DIM + 1] for a in acc])


    def stage_a(blk0, slot0):
        mus = []
        for u in range(MOBA_GROUP):
            for h in range(nh):
                st = _dot(k_group(jnp.minimum(blk0 + u, nb - 1), 1, h), qh[h])
                s_ref[slot0 + nh * u + h] = st
                mus.append(jnp.max(st, axis=0, keepdims=True))
        return mus

    def stage_b(blk0, slot0, mus, state):
        state = list(state)
        for u in range(MOBA_GROUP):
            j = blk0 + u
            for h in range(nh):
                mu = mus[nh * u + h]
                ls, pv = weighted_values(s_ref[slot0 + nh * u + h], mu, v_aug(j, h))
                m, l, acc = state[3 * h:3 * h + 3]
                mu_sel = mu + sel_ref[h, pl.ds(j, 1), :]
                m_new = jnp.maximum(m, mu_sel)
                f = jnp.exp2(mu_sel - m_new)
                c = jnp.exp2(m - m_new)
                state[3 * h:3 * h + 3] = [m_new, c * l + f * ls, c * acc + f * pv]
        return state

    def body(t, loop_carry):
        mus0, state = loop_carry[:units], loop_carry[units:]
        blk = t * (2 * MOBA_GROUP)
        mus1 = stage_a(blk + MOBA_GROUP, units)
        state = stage_b(blk, 0, mus0, state)
        mus0 = stage_a(blk + 2 * MOBA_GROUP, 0)
        state = stage_b(blk + MOBA_GROUP, units, mus1, state)
        return (*mus0, *state)

    @pl.when(bounded_ref[0] == 0)
    def _():
        state = []
        for h in range(nh):
            st = jnp.where(causal, _dot(k_group(i, 1, h), qh[h]), NEG_INF)
            mu = jnp.max(st, axis=0, keepdims=True)
            state += [mu, *weighted_values(st, mu, v_aug(i, h))]
        mus0 = stage_a(0, 0)
        select_blocks()
        state = lax.fori_loop(0, trips, body, (*mus0, *state))[units:]
        finish(state[2::3], state[1::3])


def _moba(bounded, mqt, mk, mvt, kmean, batch, seq):
    t = mk.shape[0]
    nb = seq // MOBA_BLOCK
    lq = MOBA_BLOCK
    k3 = mk.reshape(t // MOBA_BLOCK, MOBA_BLOCK, MOBA_WIDTH)
    assert nb % (2 * MOBA_GROUP) == 0
    units = MOBA_HEADS * MOBA_GROUP
    kv_bytes = 2 * nb * MOBA_BLOCK * (MOBA_WIDTH + MOBA_HEADS * MOBA_V_ROWS)
    scratch_bytes = lq * lq * (4 * 2 * units + 2 * (2 * units + MOBA_HEADS)) + 4 * MOBA_HEADS * nb * lq
    resident = lambda shape, imap: pl.BlockSpec(shape, imap, pipeline_mode=pl.Buffered(1))
    return pl.pallas_call(
        _moba_kernel,
        grid=(batch, nb),
        in_specs=[
            pl.BlockSpec(memory_space=pltpu.SMEM),
            pl.BlockSpec((MOBA_WIDTH, lq), lambda b, i: (0, b * nb + i)),
            resident((nb, MOBA_BLOCK, MOBA_WIDTH), lambda b, i: (b, 0, 0)),
            resident((nb, MOBA_HEADS * MOBA_V_ROWS, MOBA_BLOCK), lambda b, i: (b, 0, 0)),
            resident((nb, MOBA_WIDTH), lambda b, i: (b, 0)),
        ],
        out_specs=pl.BlockSpec((lq, MOBA_WIDTH), lambda b, i: (b * nb + i, 0)),
        out_shape=jax.ShapeDtypeStruct((t, MOBA_WIDTH), BF16),
        scratch_shapes=[pltpu.VMEM((MOBA_HEADS, nb, lq), F32),
                        pltpu.VMEM((2 * units, lq, lq), F32),
                        pltpu.VMEM((2 * units + MOBA_HEADS, lq, lq), BF16)],
        compiler_params=pltpu.CompilerParams(
            dimension_semantics=("arbitrary", "arbitrary"),
            vmem_limit_bytes=_vmem_limit(kv_bytes + scratch_bytes + (12 << 20))),
        name="moba",
    )(bounded, mqt, k3, mvt, kmean)


def _swa_kernel(bounded_ref, sink_ref, qt_ref, k_ref, kh_ref, vt_ref, vth_ref, o_ref, p_ref):
    tq = k_ref.shape[0]
    w = SWA_WINDOW
    i = pl.program_id(1)
    ncol = SWA_GROUP * w
    kpos = lax.broadcasted_iota(jnp.int32, (2 * w, ncol), 0)
    col = lax.broadcasted_iota(jnp.int32, (2 * w, ncol), 1)
    diff = w + (col % w) - kpos
    in_window = (diff >= 0) & (diff < w)
    zeros = jnp.zeros((HEAD_DIM, ncol), BF16)
    ones = jnp.ones((MOBA_V_ROWS - HEAD_DIM, 2 * w), BF16)

    def attend(shifted):
        sink_terms = []
        for r in range(tq // w):
            if r == 0:
                kcat = jnp.concatenate([kh_ref[...], k_ref[0:w, :]], axis=0)
                mask = in_window & ((kpos >= w) | (i > 0))
            else:
                kcat = k_ref[(r - 1) * w:(r + 1) * w, :]
                mask = in_window
            for g in range(SWA_KV_HEADS):
                qs = jnp.concatenate(
                    [qt_ref[(SWA_GROUP * g + a) * HEAD_DIM:(SWA_GROUP * g + a + 1) * HEAD_DIM, r * w:(r + 1) * w]
                     for a in range(SWA_GROUP)], axis=1)
                qpad = jnp.concatenate([qs, zeros] if g == 0 else [zeros, qs], axis=0)
                st = jnp.where(mask, _dot(kcat, qpad), NEG_INF)
                sink = jnp.concatenate(
                    [jnp.full((1, w), sink_ref[SWA_GROUP * g + a] * LOG2E, F32) for a in range(SWA_GROUP)],
                    axis=1)
                if shifted:
                    m = jnp.maximum(jnp.max(st, axis=0, keepdims=True), sink)
                    st, sink = st - m, sink - m
                p_ref[r * SWA_KV_HEADS + g] = jnp.exp2(st).astype(BF16)
                sink_terms.append(jnp.exp2(sink))
        for r in range(tq // w):
            if r == 0:
                vcat = jnp.concatenate([vth_ref[...], vt_ref[:, 0:w]], axis=1)
            else:
                vcat = vt_ref[:, (r - 1) * w:(r + 1) * w]
            pieces = []
            for g in range(SWA_KV_HEADS):
                v_aug = jnp.concatenate([vcat[g * HEAD_DIM:(g + 1) * HEAD_DIM, :], ones], axis=0)
                pv = _dot(v_aug, p_ref[r * SWA_KV_HEADS + g])
                ot = pv[:HEAD_DIM] / (pv[HEAD_DIM:HEAD_DIM + 1] + sink_terms[r * SWA_KV_HEADS + g])
                pieces += [ot[:, a * w:(a + 1) * w] for a in range(SWA_GROUP)]
            o_ref[r * w:(r + 1) * w, :] = jnp.concatenate(pieces, axis=0).T.astype(BF16)

    @pl.when(bounded_ref[0] != 0)
    def _():
        attend(False)

    @pl.when(bounded_ref[0] == 0)
    def _():
        attend(True)


def _swa(bounded, sinks, sqt, sk, svt, batch, seq):
    t = sk.shape[0]
    tq = SWA_ROWS
    w = SWA_WINDOW
    steps = seq // tq
    halo = lambda b, i: jnp.maximum((b * seq + i * tq) // w - 1, 0)
    return pl.pallas_call(
        _swa_kernel,
        grid=(batch, steps),
        in_specs=[
            pl.BlockSpec(memory_space=pltpu.SMEM),
            pl.BlockSpec(memory_space=pltpu.SMEM),
            pl.BlockSpec((SWA_Q_WIDTH, tq), lambda b, i: (0, b * steps + i)),
            pl.BlockSpec((tq, SWA_KV_WIDTH), lambda b, i: (b * steps + i, 0)),
            pl.BlockSpec((w, SWA_KV_WIDTH), lambda b, i: (halo(b, i), 0)),
            pl.BlockSpec((SWA_KV_WIDTH, tq), lambda b, i: (0, b * steps + i)),
            pl.BlockSpec((SWA_KV_WIDTH, w), lambda b, i: (0, halo(b, i))),
        ],
        out_specs=pl.BlockSpec((tq, SWA_Q_WIDTH), lambda b, i: (b * steps + i, 0)),
        out_shape=jax.ShapeDtypeStruct((t, SWA_Q_WIDTH), BF16),
        scratch_shapes=[pltpu.VMEM(((tq // w) * SWA_KV_HEADS, 2 * w, SWA_GROUP * w), BF16)],
        compiler_params=pltpu.CompilerParams(dimension_semantics=("arbitrary", "arbitrary")),
        name="swa",
    )(bounded, sinks, sqt, sk, sk, svt, svt)


def _log_sigmoid(v):
    return jnp.minimum(v, 0.0) - jnp.log1p(jnp.exp(-jnp.abs(v)))


def _mlstm_kernel(xqk_ref, halo_ref, xvt_ref, xot_ref, gt_ref, gc_ref, convw_ref, convb_ref,
                  gain_ref, cum_ref, cumt_ref, o_ref, ext_ref, c_ref, n_ref, m_ref,
                  qm_ref, k_ref, pv_ref, row_ref, kv_ref, nk_ref, cprev_ref, nprev_ref, ht_ref):
    tt = xqk_ref.shape[0]
    lc = MLSTM_CHUNK
    i = pl.program_id(1)

    @pl.when(i == 0)
    def _():
        c_ref[...] = jnp.zeros_like(c_ref)
        n_ref[...] = jnp.zeros_like(n_ref)
        m_ref[...] = jnp.zeros_like(m_ref)

    hist = halo_ref[...].astype(F32)
    ext_ref[0:CONV_HALO, :] = jnp.where(i > 0, hist, jnp.zeros_like(hist))
    ext_ref[CONV_HALO:CONV_HALO + tt, :] = xqk_ref[...].astype(F32)
    convw = convw_ref[...]
    conv = jnp.zeros((tt, 2 * MLSTM_WIDTH), F32) + convb_ref[...]
    for j in range(CONV_WIDTH):
        start = CONV_HALO - (CONV_WIDTH - 1) + j
        conv = conv + convw[j:j + 1, :] * ext_ref[start:start + tt, :]
    qk = conv * jax.nn.sigmoid(conv)
    q_all = qk[:, :MLSTM_WIDTH]
    lane_q = lax.broadcasted_iota(jnp.int32, q_all.shape, 1)
    even_head = (lane_q % HEAD_PAIR) < HEAD_DIM
    qm_ref[0] = jnp.where(even_head, q_all, 0.0).astype(BF16)
    qm_ref[1] = jnp.where(even_head, 0.0, q_all).astype(BF16)
    k_ref[...] = (qk[:, MLSTM_WIDTH:] * SM_SCALE).astype(BF16)

    g_row = gt_ref[...]
    g_col = gc_ref[...]
    cum = cum_ref[...]
    cumt = cumt_ref[...]
    b_row = sum(_dot(piece, cum) for piece in _split3_bf16(_log_sigmoid(g_row)))
    b_col = sum(_dot(cumt, piece) for piece in _split3_bf16(_log_sigmoid(g_col)))
    u_col = b_col[:, MLSTM_HEADS:] - g_col[:, :MLSTM_HEADS]

    lane = lax.broadcasted_iota(jnp.int32, (HEAD_DIM, HEAD_PAIR), 1)
    s_idx = lax.broadcasted_iota(jnp.int32, (lc, lc), 0)
    t_idx = lax.broadcasted_iota(jnp.int32, (lc, lc), 1)
    causal = s_idx <= t_idx
    nchunks = tt // lc
    heads = range(MLSTM_HEADS)
    in_head = [(lane < HEAD_DIM), (lane >= HEAD_DIM)]

    def operands(c, h):
        p, hh = divmod(h, 2)
        sl = slice(c * lc, (c + 1) * lc)
        qm = qm_ref[hh, sl, p * HEAD_PAIR:(p + 1) * HEAD_PAIR]
        kp = k_ref[sl, p * HEAD_PAIR:(p + 1) * HEAD_PAIR]
        return sl, hh, qm, kp

    m_state = [m_ref[h][0:1, 0:lc] for h in heads]
    for c in range(nchunks):
        for h in heads:
            u = c * MLSTM_HEADS + h
            sl, hh, qm, kp = operands(c, h)
            vt = xvt_ref[h * HEAD_DIM:(h + 1) * HEAD_DIM, sl]
            br = b_row[MLSTM_HEADS + h:MLSTM_HEADS + h + 1, sl]
            ir = g_row[h:h + 1, sl]
            uc = u_col[sl, h:h + 1]
            a = br[:, lc - 1:lc]
            dt = jnp.where(causal, br - uc, NEG_INF)
            inter = br + m_state[h]
            m_t = jnp.maximum(inter, jnp.max(dt, axis=0, keepdims=True))
            w_inter = jnp.exp(inter - m_t)
            qkt = _dot_nt(kp, qm) * jnp.exp(dt - m_t)
            pv_ref[u] = _dot(vt, qkt.astype(BF16))
            row_ref[u, 0:1, 0:lc] = w_inter
            row_ref[u, 1:2, 0:lc] = jnp.sum(qkt, axis=0, keepdims=True)
            row_ref[u, 2:3, 0:lc] = jnp.exp(-m_t)
            g_end = a - br + ir
            m_new = jnp.maximum(a + m_state[h], jnp.max(g_end, axis=1, keepdims=True))
            w_s = jnp.exp(g_end - m_new)
            decay = jnp.exp(a + m_state[h] - m_new)
            row_ref[u, 3:4, 0:lc] = decay
            vtw = (vt.astype(F32) * w_s).astype(BF16)
            kv_ref[u] = jnp.where(in_head[hh], _dot(vtw, kp), 0.0)
            w_rows = jnp.broadcast_to(w_s, (8, lc)).astype(BF16)
            nk_ref[u] = jnp.where(in_head[hh][0:8], _dot(w_rows, kp), 0.0)
            m_state[h] = m_new

    for h in heads:
        c_state = c_ref[h]
        n_state = n_ref[h]
        for c in range(nchunks):
            u = c * MLSTM_HEADS + h
            cprev_ref[u] = c_state.astype(BF16)
            nprev_ref[u] = n_state
            decay = row_ref[u, 3:4, 0:HEAD_PAIR]
            c_state = decay * c_state + kv_ref[u]
            n_state = decay * n_state + nk_ref[u]
        c_ref[h] = c_state
        n_ref[h] = n_state
        m_ref[h] = jnp.broadcast_to(m_state[h][:, 0:1], m_ref.shape[1:])

    for c in range(nchunks):
        for h in heads:
            u = c * MLSTM_HEADS + h
            sl, hh, qm, _ = operands(c, h)
            w_inter = row_ref[u, 0:1, 0:lc]
            num = w_inter * _dot_nt(cprev_ref[u], qm) + pv_ref[u]
            den = w_inter * _dot_nt(nprev_ref[u].astype(BF16), qm)[0:1, :] + row_ref[u, 1:2, 0:lc]
            ht_ref[h * HEAD_DIM:(h + 1) * HEAD_DIM, sl] = num / jnp.maximum(jnp.abs(den), row_ref[u, 2:3, 0:lc])

    gain = jnp.concatenate([gain_ref[...]] * (tt // LANES_V7X), axis=1)
    outs = []
    for h in heads:
        rows = slice(h * HEAD_DIM, (h + 1) * HEAD_DIM)
        hg = ht_ref[rows, :] * jax.nn.sigmoid(xot_ref[rows, :].astype(F32))
        msq = jnp.mean(hg * hg, axis=0, keepdims=True)
        outs.append(hg * lax.rsqrt(msq + NORM_EPS) * gain[rows])
    o_ref[...] = jnp.concatenate(outs, axis=0).T.astype(BF16)


def _mlstm(xqk, xvt, xot, gt, gc, convw, convb, gain, cum, batch, seq):
    t = xqk.shape[0]
    tt = MLSTM_ROWS
    steps = seq // tt
    units = MLSTM_HEADS * (tt // MLSTM_CHUNK)
    col = lambda r: pl.BlockSpec((r, tt), lambda b, i: (0, b * steps + i))
    return pl.pallas_call(
        _mlstm_kernel,
        grid=(batch, steps),
        in_specs=[
            pl.BlockSpec((tt, 2 * MLSTM_WIDTH), lambda b, i: (b * steps + i, 0)),
            pl.BlockSpec((CONV_HALO, 2 * MLSTM_WIDTH),
                         lambda b, i: (jnp.maximum((b * seq + i * tt) // CONV_HALO - 1, 0), 0)),
            col(MLSTM_WIDTH),
            col(MLSTM_WIDTH),
            col(NUM_GATES),
            pl.BlockSpec((tt, NUM_GATES), lambda b, i: (b * steps + i, 0)),
            _const_spec(convw.shape),
            _const_spec(convb.shape),
            _const_spec(gain.shape),
            _const_spec(cum.shape),
            _const_spec(cum.shape),
        ],
        out_specs=pl.BlockSpec((tt, MLSTM_WIDTH), lambda b, i: (b * steps + i, 0)),
        out_shape=jax.ShapeDtypeStruct((t, MLSTM_WIDTH), BF16),
        scratch_shapes=[
            pltpu.VMEM((CONV_HALO + tt, 2 * MLSTM_WIDTH), F32),
            pltpu.VMEM((MLSTM_HEADS, HEAD_DIM, HEAD_PAIR), F32),
            pltpu.VMEM((MLSTM_HEADS, 8, HEAD_PAIR), F32),
            pltpu.VMEM((MLSTM_HEADS, 8, MLSTM_CHUNK), F32),
            pltpu.VMEM((2, tt, MLSTM_WIDTH), BF16),
            pltpu.VMEM((tt, MLSTM_WIDTH), BF16),
            pltpu.VMEM((units, HEAD_DIM, MLSTM_CHUNK), F32),
            pltpu.VMEM((units, 8, MLSTM_CHUNK), F32),
            pltpu.VMEM((units, HEAD_DIM, HEAD_PAIR), F32),
            pltpu.VMEM((units, 8, HEAD_PAIR), F32),
            pltpu.VMEM((units, HEAD_DIM, HEAD_PAIR), BF16),
            pltpu.VMEM((units, 8, HEAD_PAIR), F32),
            pltpu.VMEM((MLSTM_WIDTH, tt), F32),
        ],
        compiler_params=pltpu.CompilerParams(dimension_semantics=("arbitrary", "arbitrary")),
        name="mlstm",
    )(xqk, xqk, xvt, xot, gt, gc, convw, convb, gain, cum, cum.T)


def _out_mlp_kernel(x_ref, ym_ref, yl_ref, ys_ref, wom_ref, wol_ref, wos_ref, ln2_ref,
                    wup_ref, wdn_ref, o_ref):
    x1 = (x_ref[...] + _dot(ym_ref[...], wom_ref[...]) + _dot(yl_ref[...], wol_ref[...])
          + _dot(ys_ref[...], wos_ref[...]))
    ms = jnp.mean(x1 * x1, axis=-1, keepdims=True)
    hn = (x1 * lax.rsqrt(ms + NORM_EPS) * ln2_ref[...]).astype(BF16)
    o_ref[...] = x1
    for c in range(0, D_FF, MLP_FF_CHUNK):
        u = _dot(hn, wup_ref[:, c:c + MLP_FF_CHUNK])
        act = jnp.square(jnp.maximum(u, 0.0)).astype(BF16)
        o_ref[...] += _dot(act, wdn_ref[c:c + MLP_FF_CHUNK, :])


def _out_mlp(xf, ym, yl, ys, wom, wol, wos, ln2, wup, wdn):
    t = xf.shape[0]
    tm = OUT_MLP_ROWS
    row = lambda w: pl.BlockSpec((tm, w), lambda i: (i, 0))
    weights = 2 * (wom.size + wol.size + wos.size + wup.size + wdn.size)
    tiles = tm * (4 * 4 * D_MODEL + 2 * 2 * D_MODEL + 4 * 4 * D_MODEL + 6 * MLP_FF_CHUNK)
    return pl.pallas_call(
        _out_mlp_kernel,
        grid=(t // tm,),
        in_specs=[
            row(D_MODEL), row(MOBA_WIDTH), row(MLSTM_WIDTH), row(SWA_Q_WIDTH),
            _const_spec(wom.shape), _const_spec(wol.shape), _const_spec(wos.shape),
            _const_spec(ln2.shape), _const_spec(wup.shape), _const_spec(wdn.shape),
        ],
        out_specs=row(D_MODEL),
        out_shape=jax.ShapeDtypeStruct((t, D_MODEL), F32),
        compiler_params=pltpu.CompilerParams(
            dimension_semantics=("arbitrary",),
            vmem_limit_bytes=_vmem_limit(weights + tiles + (8 << 20))),
        name="out_mlp",
    )(xf, ym, yl, ys, wom, wol, wos, ln2, wup, wdn)


def _rope_tables(seq):
    inv = ROPE_THETA ** (-jnp.arange(0, HEAD_DIM, 2, dtype=F32) / HEAD_DIM)
    ang = jnp.arange(seq, dtype=F32)[:, None] * inv[None, :]
    cos, sin = jnp.cos(ang), jnp.sin(ang)
    cosn = jnp.concatenate([cos, cos, cos, cos], axis=1)
    sinn = jnp.concatenate([-sin, sin, -sin, sin], axis=1)
    return cosn, sinn, cos.T, sin.T


def _layer(xf, tables, consts, batch, seq, ln1, w_in, conv_w, conv_b, igate_b, fgate_b, mlstm_norm,
           moba_q_norm, moba_k_norm, swa_q_norm, swa_k_norm, swa_sinks, w_out, ln2, w_up, w_down):
    cosn, sinn, cost, sint = tables
    bd, cum = consts
    o = 0
    cols = {}
    for name, width in (("mq", MOBA_WIDTH), ("mk", MOBA_WIDTH), ("mv", MOBA_WIDTH), ("sq", SWA_Q_WIDTH),
                        ("sk", SWA_KV_WIDTH), ("sv", SWA_KV_WIDTH), ("xqk", 2 * MLSTM_WIDTH),
                        ("xv", MLSTM_WIDTH), ("xo", MLSTM_WIDTH), ("xi", MLSTM_HEADS), ("xf", MLSTM_HEADS)):
        cols[name] = w_in[:, o:o + width]
        o += width
    wnat = jnp.concatenate([cols["mk"], cols["sk"], cols["xqk"]], axis=1).astype(BF16)
    wtr = jnp.concatenate([cols["mq"], cols["sq"], cols["mv"], cols["sv"], cols["xv"], cols["xo"],
                           cols["xi"], cols["xf"]], axis=1).T.astype(BF16)
    gk = jnp.concatenate([jnp.tile(moba_k_norm, MOBA_HEADS), jnp.tile(swa_k_norm, SWA_KV_HEADS)])[None, :]
    gq = jnp.broadcast_to(
        (jnp.concatenate([moba_q_norm, swa_q_norm]) * (SM_SCALE * LOG2E))[:, None], (2 * HEAD_DIM, LANES_V7X))
    gbias = jnp.broadcast_to(jnp.concatenate([igate_b, fgate_b])[:, None], (NUM_GATES, LANES_V7X))

    (mk, kmean, sk, xqk, mqt, sqt, mvt, svt, xvt, xot, gt) = _in_proj(
        xf, ln1[None, :], wnat, wtr, bd, gk, cosn, sinn, gq, cost, sint, gbias, seq)

    kmean = kmean.reshape(-1, MOBA_WIDTH)
    def bounded(q_gain, k_gain, *extra):
        bound = (HEAD_DIM * SM_SCALE * LOG2E) * jnp.max(jnp.abs(q_gain)) * jnp.max(jnp.abs(k_gain))
        for e in extra:
            bound = jnp.maximum(bound, jnp.max(jnp.abs(e)) * LOG2E)
        return (bound <= MOBA_SAFE_LOG2).astype(jnp.int32).reshape(1)

    ym = _moba(bounded(moba_q_norm, moba_k_norm), mqt, mk, mvt, kmean, batch, seq)
    ys = _swa(bounded(swa_q_norm, swa_k_norm, swa_sinks), swa_sinks, sqt, sk, svt, batch, seq)
    gain = jnp.broadcast_to(mlstm_norm.reshape(MLSTM_WIDTH, 1), (MLSTM_WIDTH, LANES_V7X))
    yl = _mlstm(xqk, xvt, xot, gt, gt.T, conv_w, conv_b[None, :], gain, cum, batch, seq)

    wo = w_out.astype(BF16)
    return _out_mlp(xf, ym, yl, ys, wo[:MOBA_WIDTH], wo[MOBA_WIDTH:MOBA_WIDTH + MLSTM_WIDTH],
                    wo[MOBA_WIDTH + MLSTM_WIDTH:], ln2[None, :], w_up.astype(BF16), w_down.astype(BF16))


def kernel(x, ln1, w_in, conv_w, conv_b, igate_b, fgate_b, mlstm_norm, moba_q_norm, moba_k_norm,
           swa_q_norm, swa_k_norm, swa_sinks, w_out, ln2, w_up, w_down):
    batch, seq, d = x.shape
    assert d == D_MODEL and seq % max(IN_PROJ_ROWS, SWA_ROWS, MLSTM_ROWS, MOBA_BLOCK) == 0
    depth = ln1.shape[0]
    tables = _rope_tables(seq)
    bw = 2 * LANES_V7X
    ids = jnp.arange(bw) // HEAD_DIM
    bd = jnp.where(ids[:, None] == ids[None, :], 1.0 / HEAD_DIM, 0.0).astype(BF16)
    tids = jnp.arange(MLSTM_ROWS)
    cum = ((tids[:, None] // MLSTM_CHUNK == tids[None, :] // MLSTM_CHUNK)
           & (tids[:, None] <= tids[None, :])).astype(BF16)
    xf = x.reshape(batch * seq, d)
    for l in range(depth):
        xf = _layer(xf, tables, (bd, cum), batch, seq, ln1[l], w_in[l], conv_w[l], conv_b[l], igate_b[l],
                    fgate_b[l], mlstm_norm[l], moba_q_norm[l], moba_k_norm[l], swa_q_norm[l],
                    swa_k_norm[l], swa_sinks[l], w_out[l], ln2[l], w_up[l], w_down[l])
    return xf.reshape(batch, seq, d)
```

```python
import functools

import jax
import jax.numpy as jnp
from jax import lax
from jax.experimental import pallas as pl
from jax.experimental.pallas import tpu as pltpu

F32 = jnp.float32
BF16 = jnp.bfloat16
NEG_INF = float("-inf")

D_MODEL = 1024
HEAD_DIM = 64
MOBA_HEADS = 6
MLSTM_HEADS = 4
SWA_Q_HEADS = 6
SWA_KV_HEADS = 2
SWA_GROUP = SWA_Q_HEADS // SWA_KV_HEADS
MOBA_WIDTH = MOBA_HEADS * HEAD_DIM
MLSTM_WIDTH = MLSTM_HEADS * HEAD_DIM
SWA_Q_WIDTH = SWA_Q_HEADS * HEAD_DIM
SWA_KV_WIDTH = SWA_KV_HEADS * HEAD_DIM
MOBA_BLOCK = 256
MOBA_TOPK = 3
MLSTM_CHUNK = 256
CONV_WIDTH = 4
SWA_WINDOW = 128
ROPE_THETA = 10000.0
D_FF = 4 * D_MODEL
NORM_EPS = 1e-6
SM_SCALE = HEAD_DIM ** -0.5
LOG2E = 1.4426950408889634
MOBA_V_ROWS = HEAD_DIM + 16
MOBA_SAFE_LOG2 = 60.0

LANES_V7X = 128
VMEM_BYTES_V7X = 64 * 1024 * 1024
HEAD_PAIR = 2 * HEAD_DIM
assert HEAD_PAIR == LANES_V7X

NAT_WIDTH = MOBA_WIDTH + SWA_KV_WIDTH + 2 * MLSTM_WIDTH
KN_WIDTH = MOBA_WIDTH + SWA_KV_WIDTH
QT_ROWS = MOBA_WIDTH + SWA_Q_WIDTH
TR_ROWS = QT_ROWS + MOBA_WIDTH + SWA_KV_WIDTH + 2 * MLSTM_WIDTH + 2 * MLSTM_HEADS
NUM_GATES = 2 * MLSTM_HEADS

IN_PROJ_ROWS = 512
OUT_MLP_ROWS = 512
MLP_FF_CHUNK = 1024
SWA_ROWS = 512
MLSTM_ROWS = 256
MOBA_GROUP = 2
MOBA_Q_BLOCKS = 1
CONV_HALO = 16


def _dot(a, b):
    return jnp.dot(a, b, preferred_element_type=F32)


def _dot_nt(a, b):
    return lax.dot_general(a, b, (((1,), (1,)), ((), ())), preferred_element_type=F32)


def _split_bf16(v):
    hi = v.astype(BF16)
    lo = (v - hi.astype(F32)).astype(BF16)
    return hi, lo


def _split3_bf16(v):
    hi = v.astype(BF16)
    rest = v - hi.astype(F32)
    mid = rest.astype(BF16)
    return hi, mid, (rest - mid.astype(F32)).astype(BF16)


def _vmem_limit(nbytes):
    return int(min(nbytes, VMEM_BYTES_V7X - 4 * 1024 * 1024))


def _const_spec(shape):
    nd = len(shape)
    return pl.BlockSpec(shape, lambda *_: (0,) * nd, pipeline_mode=pl.Buffered(1))


def _in_proj_kernel(x_ref, ln1_ref, wnat_ref, wtr_ref, bd_ref, gk_ref, cosn_ref, sinn_ref,
                    gq_ref, cost_ref, sint_ref, gbias_ref,
                    mk_ref, kmean_ref, sk_ref, xqk_ref, mqt_ref, sqt_ref, mvt_ref, svt_ref,
                    xvt_ref, xot_ref, gt_ref):
    tm = x_ref.shape[0]
    x = x_ref[...]
    ms = jnp.mean(x * x, axis=-1, keepdims=True)
    hn = (x * lax.rsqrt(ms + NORM_EPS) * ln1_ref[...]).astype(BF16)
    nat = _dot(hn, wnat_ref[...])

    kk = nat[:, :KN_WIDTH]
    hi, lo = _split_bf16(kk * kk)
    bd = bd_ref[...]
    bw = bd.shape[0]
    msk = jnp.concatenate(
        [_dot(hi[:, c:c + bw], bd) + _dot(lo[:, c:c + bw], bd) for c in range(0, KN_WIDTH, bw)],
        axis=1)
    kn = kk * lax.rsqrt(msk + NORM_EPS) * gk_ref[...]
    reps = KN_WIDTH // LANES_V7X
    cosn = jnp.concatenate([cosn_ref[...]] * reps, axis=1)
    sinn = jnp.concatenate([sinn_ref[...]] * reps, axis=1)
    lane = lax.broadcasted_iota(jnp.int32, kn.shape, 1)
    first_half = (lane % HEAD_DIM) < (HEAD_DIM // 2)
    swapped = jnp.where(first_half,
                        pltpu.roll(kn, KN_WIDTH - HEAD_DIM // 2, 1),
                        pltpu.roll(kn, HEAD_DIM // 2, 1))
    kr = kn * cosn + swapped * sinn
    mk = kr[:, :MOBA_WIDTH]
    mk_ref[...] = mk.astype(BF16)
    nblk = tm // MOBA_BLOCK
    kmean_ref[0] = jnp.concatenate(
        [jnp.mean(mk[c * MOBA_BLOCK:(c + 1) * MOBA_BLOCK], axis=0, keepdims=True) for c in range(nblk)],
        axis=0)
    sk_ref[...] = kr[:, MOBA_WIDTH:].astype(BF16)
    xqk_ref[...] = nat[:, KN_WIDTH:].astype(BF16)

    tr = _dot_nt(wtr_ref[...], hn)
    cost = cost_ref[...]
    sint = sint_ref[...]
    gq = jnp.concatenate([gq_ref[...]] * (tm // LANES_V7X), axis=1)
    half = HEAD_DIM // 2
    for h in range(QT_ROWS // HEAD_DIM):
        blk = tr[h * HEAD_DIM:(h + 1) * HEAD_DIM]
        is_swa = h >= MOBA_HEADS
        gain = gq[HEAD_DIM:] if is_swa else gq[:HEAD_DIM]
        msq = jnp.mean(blk * blk, axis=0, keepdims=True)
        qn = blk * lax.rsqrt(msq + NORM_EPS) * gain
        x1, x2 = qn[:half], qn[half:]
        rot = jnp.concatenate([x1 * cost - x2 * sint, x2 * cost + x1 * sint], axis=0).astype(BF16)
        if is_swa:
            r0 = (h - MOBA_HEADS) * HEAD_DIM
            sqt_ref[r0:r0 + HEAD_DIM, :] = rot
        else:
            mqt_ref[h * HEAD_DIM:(h + 1) * HEAD_DIM, :] = rot
    r = QT_ROWS
    mv = tr[r:r + MOBA_WIDTH].astype(BF16)
    ones = jnp.ones((MOBA_V_ROWS - HEAD_DIM, MOBA_BLOCK), BF16)
    for c in range(nblk):
        for h in range(MOBA_HEADS):
            r0 = h * MOBA_V_ROWS
            mvt_ref[c, r0:r0 + HEAD_DIM, :] = mv[h * HEAD_DIM:(h + 1) * HEAD_DIM,
                                                 c * MOBA_BLOCK:(c + 1) * MOBA_BLOCK]
            mvt_ref[c, r0 + HEAD_DIM:r0 + MOBA_V_ROWS, :] = ones
    r += MOBA_WIDTH
    svt_ref[...] = tr[r:r + SWA_KV_WIDTH].astype(BF16)
    r += SWA_KV_WIDTH
    xvt_ref[...] = tr[r:r + MLSTM_WIDTH].astype(BF16)
    r += MLSTM_WIDTH
    xot_ref[...] = tr[r:r + MLSTM_WIDTH].astype(BF16)
    r += MLSTM_WIDTH
    gbias = jnp.concatenate([gbias_ref[...]] * (tm // LANES_V7X), axis=1)
    gt_ref[...] = tr[r:r + NUM_GATES] + gbias


def _in_proj(xf, ln1, wnat, wtr, bd, gk, cosn, sinn, gq, cost, sint, gbias, seq):
    t = xf.shape[0]
    tm = IN_PROJ_ROWS
    steps = t // tm
    seq_steps = seq // tm
    nblk = tm // MOBA_BLOCK
    row = lambda w: pl.BlockSpec((tm, w), lambda i: (i, 0))
    col = lambda r: pl.BlockSpec((r, tm), lambda i: (0, i))
    in_specs = [
        row(D_MODEL),
        _const_spec((1, D_MODEL)),
        _const_spec(wnat.shape),
        _const_spec(wtr.shape),
        _const_spec(bd.shape),
        _const_spec(gk.shape),
        pl.BlockSpec((tm, LANES_V7X), lambda i: (i % seq_steps, 0)),
        pl.BlockSpec((tm, LANES_V7X), lambda i: (i % seq_steps, 0)),
        _const_spec(gq.shape),
        pl.BlockSpec((HEAD_DIM // 2, tm), lambda i: (0, i % seq_steps)),
        pl.BlockSpec((HEAD_DIM // 2, tm), lambda i: (0, i % seq_steps)),
        _const_spec(gbias.shape),
    ]
    out_shape = [
        jax.ShapeDtypeStruct((t, MOBA_WIDTH), BF16),
        jax.ShapeDtypeStruct((steps, nblk, MOBA_WIDTH), F32),
        jax.ShapeDtypeStruct((t, SWA_KV_WIDTH), BF16),
        jax.ShapeDtypeStruct((t, 2 * MLSTM_WIDTH), BF16),
        jax.ShapeDtypeStruct((MOBA_WIDTH, t), BF16),
        jax.ShapeDtypeStruct((SWA_Q_WIDTH, t), BF16),
        jax.ShapeDtypeStruct((t // MOBA_BLOCK, MOBA_HEADS * MOBA_V_ROWS, MOBA_BLOCK), BF16),
        jax.ShapeDtypeStruct((SWA_KV_WIDTH, t), BF16),
        jax.ShapeDtypeStruct((MLSTM_WIDTH, t), BF16),
        jax.ShapeDtypeStruct((MLSTM_WIDTH, t), BF16),
        jax.ShapeDtypeStruct((NUM_GATES, t), F32),
    ]
    out_specs = [
        row(MOBA_WIDTH),
        pl.BlockSpec((1, nblk, MOBA_WIDTH), lambda i: (i, 0, 0)),
        row(SWA_KV_WIDTH),
        row(2 * MLSTM_WIDTH),
        col(MOBA_WIDTH),
        col(SWA_Q_WIDTH),
        pl.BlockSpec((nblk, MOBA_HEADS * MOBA_V_ROWS, MOBA_BLOCK), lambda i: (i, 0, 0)),
        col(SWA_KV_WIDTH),
        col(MLSTM_WIDTH),
        col(MLSTM_WIDTH),
        col(NUM_GATES),
    ]
    weights = 2 * (wnat.size + wtr.size)
    tiles = tm * (2 * 4 * D_MODEL + 3 * 4 * (NAT_WIDTH + TR_ROWS) + 2 * 2 * (NAT_WIDTH + TR_ROWS))
    return pl.pallas_call(
        _in_proj_kernel,
        grid=(steps,),
        in_specs=in_specs,
        out_specs=out_specs,
        out_shape=out_shape,
        compiler_params=pltpu.CompilerParams(
            dimension_semantics=("arbitrary",),
            vmem_limit_bytes=_vmem_limit(2 * weights + tiles + (8 << 20))),
        name="in_proj",
    )(xf, ln1, wnat, wtr, bd, gk, cosn, sinn, gq, cost, sint, gbias)


def _moba_kernel(bounded_ref, qt_ref, k_ref, vt_ref, km_ref, o_ref, sel_ref, s_ref, p_ref):
    nb = k_ref.shape[0]
    lq = MOBA_BLOCK
    nh = MOBA_Q_BLOCKS * MOBA_HEADS
    step = pl.program_id(1)
    row = lax.broadcasted_iota(jnp.int32, (HEAD_PAIR, lq), 0)
    own, qh = [], []
    for s in range(nh):
        qb, h = divmod(s, MOBA_HEADS)
        own.append(step * MOBA_Q_BLOCKS + qb)
        qt = qt_ref[(h // 2) * HEAD_PAIR:(h // 2 + 1) * HEAD_PAIR, qb * lq:(qb + 1) * lq]
        keep = (row < HEAD_DIM) if h % 2 == 0 else (row >= HEAD_DIM)
        qh.append(jnp.where(keep, qt, jnp.zeros_like(qt)))
    last_own = step * MOBA_Q_BLOCKS + (MOBA_Q_BLOCKS - 1)

    def k_group(blk0, count, s):
        g = (s % MOBA_HEADS) // 2
        return k_ref[pl.ds(blk0, count), :, g * HEAD_PAIR:(g + 1) * HEAD_PAIR].reshape(count * lq, HEAD_PAIR)

    def v_aug(j, s):
        h = s % MOBA_HEADS
        return vt_ref[j, h * MOBA_V_ROWS:(h + 1) * MOBA_V_ROWS, :]

    def select_blocks():
        gates = []
        for s in range(nh):
            g = (s % MOBA_HEADS) // 2
            km_hi, km_lo = _split_bf16(km_ref[:, g * HEAD_PAIR:(g + 1) * HEAD_PAIR])
            gates.append(_dot(km_hi, qh[s]) + _dot(km_lo, qh[s]))
        gate = jnp.concatenate(gates, axis=1)
        own_all = jnp.concatenate([jnp.full((1, lq), own[s], jnp.int32) for s in range(nh)], axis=1)
        blk_all = lax.broadcasted_iota(jnp.int32, gate.shape, 0)
        gate = jnp.where(blk_all < own_all, gate, NEG_INF)
        sel = jnp.full(gate.shape, NEG_INF, F32)
        for _ in range(MOBA_TOPK):
            mx = jnp.max(gate, axis=0, keepdims=True)
            idx = jnp.min(jnp.where(gate == mx, blk_all, nb), axis=0, keepdims=True)
            pick = blk_all == jnp.where(idx < own_all, idx, nb)
            sel = jnp.where(pick, 0.0, sel)
            gate = jnp.where(pick, NEG_INF, gate)
        for s in range(nh):
            sel_ref[s] = sel[:, s * lq:(s + 1) * lq]

    kpos = lax.broadcasted_iota(jnp.int32, (lq, lq), 0)
    qpos = lax.broadcasted_iota(jnp.int32, (lq, lq), 1)
    causal = kpos <= qpos

    def weighted_values(st, mu, v_rows):
        pv = _dot(v_rows, jnp.exp2(st - mu).astype(BF16))
        return pv[HEAD_DIM:HEAD_DIM + 1], pv[:HEAD_DIM]

    def finish(nums, dens):
        for s in range(0, nh, 2):
            qb, h = divmod(s, MOBA_HEADS)
            ot = jnp.concatenate([nums[s] / dens[s], nums[s + 1] / dens[s + 1]], axis=0)
            o_ref[qb * lq:(qb + 1) * lq, (h // 2) * HEAD_PAIR:(h // 2 + 1) * HEAD_PAIR] = ot.T.astype(BF16)

    trips = lax.shift_right_logical(last_own + (2 * MOBA_GROUP - 1), MOBA_GROUP.bit_length())
    units = nh * MOBA_GROUP
    own_slot = 2 * units

    @pl.when(bounded_ref[0] != 0)
    def _():
        select_blocks()
        for h in range(nh):
            st = jnp.where(causal, _dot(k_group(own[h], 1, h), qh[h]), NEG_INF)
            p_ref[own_slot + h] = jnp.exp2(st).astype(BF16)

        def stage_p(blk0, slot0):
            blk0 = jnp.minimum(blk0, nb - MOBA_GROUP)
            for h in range(nh):
                p = jnp.exp2(_dot(k_group(blk0, MOBA_GROUP, h), qh[h])).astype(BF16)
                for u in range(MOBA_GROUP):
                    p_ref[slot0 + nh * u + h] = p[u * lq:(u + 1) * lq]

        def stage_v(blk0, slot0, acc):
            acc = list(acc)
            for u in range(MOBA_GROUP):
                j = blk0 + u
                for h in range(nh):
                    pv = _dot(v_aug(j, h), p_ref[slot0 + nh * u + h])
                    acc[h] = acc[h] + jnp.exp2(sel_ref[h, pl.ds(j, 1), :]) * pv
            return acc

        def trip(t, acc, last):
            blk = t * (2 * MOBA_GROUP)
            stage_p(blk + MOBA_GROUP, units)
            acc = stage_v(blk, 0, acc)
            if not last:
                stage_p(blk + 2 * MOBA_GROUP, 0)
            return tuple(stage_v(blk + MOBA_GROUP, units, acc))

        stage_p(0, 0)
        acc = tuple(_dot(v_aug(own[h], h), p_ref[own_slot + h]) for h in range(nh))
        acc = lax.fori_loop(0, jnp.maximum(trips - 1, 0), lambda t, a: trip(t, a, False), acc)
        acc = lax.cond(trips > 0, lambda a: trip(trips - 1, a, True), lambda a: a, acc)
        finish([a[:HEAD_DIM] for a in acc], [a[HEAD_DIM:HEAD_DIM + 1] for a in acc])


    def stage_a(blk0, slot0):
        mus = []
        for u in range(MOBA_GROUP):
            for h in range(nh):
                st = _dot(k_group(jnp.minimum(blk0 + u, nb - 1), 1, h), qh[h])
                s_ref[slot0 + nh * u + h] = st
                mus.append(jnp.max(st, axis=0, keepdims=True))
        return mus

    def stage_b(blk0, slot0, mus, state):
        state = list(state)
        for u in range(MOBA_GROUP):
            j = blk0 + u
            for h in range(nh):
                mu = mus[nh * u + h]
                ls, pv = weighted_values(s_ref[slot0 + nh * u + h], mu, v_aug(j, h))
                m, l, acc = state[3 * h:3 * h + 3]
                mu_sel = mu + sel_ref[h, pl.ds(j, 1), :]
                m_new = jnp.maximum(m, mu_sel)
                f = jnp.exp2(mu_sel - m_new)
                c = jnp.exp2(m - m_new)
                state[3 * h:3 * h + 3] = [m_new, c * l + f * ls, c * acc + f * pv]
        return state

    def body(t, loop_carry):
        mus0, state = loop_carry[:units], loop_carry[units:]
        blk = t * (2 * MOBA_GROUP)
        mus1 = stage_a(blk + MOBA_GROUP, units)
        state = stage_b(blk, 0, mus0, state)
        mus0 = stage_a(blk + 2 * MOBA_GROUP, 0)
        state = stage_b(blk + MOBA_GROUP, units, mus1, state)
        return (*mus0, *state)

    @pl.when(bounded_ref[0] == 0)
    def _():
        state = []
        for h in range(nh):
            st = jnp.where(causal, _dot(k_group(own[h], 1, h), qh[h]), NEG_INF)
            mu = jnp.max(st, axis=0, keepdims=True)
            state += [mu, *weighted_values(st, mu, v_aug(own[h], h))]
        mus0 = stage_a(0, 0)
        select_blocks()
        state = lax.fori_loop(0, trips, body, (*mus0, *state))[units:]
        finish(state[2::3], state[1::3])


def _moba(bounded, mqt, mk, mvt, kmean, batch, seq):
    t = mk.shape[0]
    nb = seq // MOBA_BLOCK
    lq = MOBA_BLOCK
    k3 = mk.reshape(t // MOBA_BLOCK, MOBA_BLOCK, MOBA_WIDTH)
    assert nb % (2 * MOBA_GROUP) == 0 and nb % MOBA_Q_BLOCKS == 0
    streams = MOBA_Q_BLOCKS * MOBA_HEADS
    units = streams * MOBA_GROUP
    steps = nb // MOBA_Q_BLOCKS
    kv_bytes = 2 * nb * MOBA_BLOCK * (MOBA_WIDTH + MOBA_HEADS * MOBA_V_ROWS)
    scratch_bytes = lq * lq * (4 * 2 * units + 2 * (2 * units + streams)) + 4 * streams * nb * lq
    resident = lambda shape, imap: pl.BlockSpec(shape, imap, pipeline_mode=pl.Buffered(1))
    return pl.pallas_call(
        _moba_kernel,
        grid=(batch, steps),
        in_specs=[
            pl.BlockSpec(memory_space=pltpu.SMEM),
            pl.BlockSpec((MOBA_WIDTH, MOBA_Q_BLOCKS * lq), lambda b, i: (0, b * steps + i)),
            resident((nb, MOBA_BLOCK, MOBA_WIDTH), lambda b, i: (b, 0, 0)),
            resident((nb, MOBA_HEADS * MOBA_V_ROWS, MOBA_BLOCK), lambda b, i: (b, 0, 0)),
            resident((nb, MOBA_WIDTH), lambda b, i: (b, 0)),
        ],
        out_specs=pl.BlockSpec((MOBA_Q_BLOCKS * lq, MOBA_WIDTH), lambda b, i: (b * steps + i, 0)),
        out_shape=jax.ShapeDtypeStruct((t, MOBA_WIDTH), BF16),
        scratch_shapes=[pltpu.VMEM((streams, nb, lq), F32),
                        pltpu.VMEM((2 * units, lq, lq), F32),
                        pltpu.VMEM((2 * units + streams, lq, lq), BF16)],
        compiler_params=pltpu.CompilerParams(
            dimension_semantics=("arbitrary", "arbitrary"),
            vmem_limit_bytes=_vmem_limit(kv_bytes + scratch_bytes + (12 << 20))),
        name="moba",
    )(bounded, mqt, k3, mvt, kmean)


def _swa_kernel(bounded_ref, sink_ref, qt_ref, k_ref, kh_ref, vt_ref, vth_ref, o_ref, p_ref):
    tq = k_ref.shape[0]
    w = SWA_WINDOW
    i = pl.program_id(1)
    ncol = SWA_GROUP * w
    kpos = lax.broadcasted_iota(jnp.int32, (2 * w, ncol), 0)
    col = lax.broadcasted_iota(jnp.int32, (2 * w, ncol), 1)
    diff = w + (col % w) - kpos
    in_window = (diff >= 0) & (diff < w)
    zeros = jnp.zeros((HEAD_DIM, ncol), BF16)
    ones = jnp.ones((MOBA_V_ROWS - HEAD_DIM, 2 * w), BF16)

    def attend(shifted):
        sink_terms = []
        for r in range(tq // w):
            if r == 0:
                kcat = jnp.concatenate([kh_ref[...], k_ref[0:w, :]], axis=0)
                mask = in_window & ((kpos >= w) | (i > 0))
            else:
                kcat = k_ref[(r - 1) * w:(r + 1) * w, :]
                mask = in_window
            for g in range(SWA_KV_HEADS):
                qs = jnp.concatenate(
                    [qt_ref[(SWA_GROUP * g + a) * HEAD_DIM:(SWA_GROUP * g + a + 1) * HEAD_DIM, r * w:(r + 1) * w]
                     for a in range(SWA_GROUP)], axis=1)
                qpad = jnp.concatenate([qs, zeros] if g == 0 else [zeros, qs], axis=0)
                st = jnp.where(mask, _dot(kcat, qpad), NEG_INF)
                sink = jnp.concatenate(
                    [jnp.full((1, w), sink_ref[SWA_GROUP * g + a] * LOG2E, F32) for a in range(SWA_GROUP)],
                    axis=1)
                if shifted:
                    m = jnp.maximum(jnp.max(st, axis=0, keepdims=True), sink)
                    st, sink = st - m, sink - m
                p_ref[r * SWA_KV_HEADS + g] = jnp.exp2(st).astype(BF16)
                sink_terms.append(jnp.exp2(sink))
        for r in range(tq // w):
            if r == 0:
                vcat = jnp.concatenate([vth_ref[...], vt_ref[:, 0:w]], axis=1)
            else:
                vcat = vt_ref[:, (r - 1) * w:(r + 1) * w]
            pieces = []
            for g in range(SWA_KV_HEADS):
                v_aug = jnp.concatenate([vcat[g * HEAD_DIM:(g + 1) * HEAD_DIM, :], ones], axis=0)
                pv = _dot(v_aug, p_ref[r * SWA_KV_HEADS + g])
                ot = pv[:HEAD_DIM] / (pv[HEAD_DIM:HEAD_DIM + 1] + sink_terms[r * SWA_KV_HEADS + g])
                pieces += [ot[:, a * w:(a + 1) * w] for a in range(SWA_GROUP)]
            o_ref[r * w:(r + 1) * w, :] = jnp.concatenate(pieces, axis=0).T.astype(BF16)

    @pl.when(bounded_ref[0] != 0)
    def _():
        attend(False)

    @pl.when(bounded_ref[0] == 0)
    def _():
        attend(True)


def _swa(bounded, sinks, sqt, sk, svt, batch, seq):
    t = sk.shape[0]
    tq = SWA_ROWS
    w = SWA_WINDOW
    steps = seq // tq
    halo = lambda b, i: jnp.maximum((b * seq + i * tq) // w - 1, 0)
    return pl.pallas_call(
        _swa_kernel,
        grid=(batch, steps),
        in_specs=[
            pl.BlockSpec(memory_space=pltpu.SMEM),
            pl.BlockSpec(memory_space=pltpu.SMEM),
            pl.BlockSpec((SWA_Q_WIDTH, tq), lambda b, i: (0, b * steps + i)),
            pl.BlockSpec((tq, SWA_KV_WIDTH), lambda b, i: (b * steps + i, 0)),
            pl.BlockSpec((w, SWA_KV_WIDTH), lambda b, i: (halo(b, i), 0)),
            pl.BlockSpec((SWA_KV_WIDTH, tq), lambda b, i: (0, b * steps + i)),
            pl.BlockSpec((SWA_KV_WIDTH, w), lambda b, i: (0, halo(b, i))),
        ],
        out_specs=pl.BlockSpec((tq, SWA_Q_WIDTH), lambda b, i: (b * steps + i, 0)),
        out_shape=jax.ShapeDtypeStruct((t, SWA_Q_WIDTH), BF16),
        scratch_shapes=[pltpu.VMEM(((tq // w) * SWA_KV_HEADS, 2 * w, SWA_GROUP * w), BF16)],
        compiler_params=pltpu.CompilerParams(dimension_semantics=("arbitrary", "arbitrary")),
        name="swa",
    )(bounded, sinks, sqt, sk, sk, svt, svt)


def _log_sigmoid(v):
    return jnp.minimum(v, 0.0) - jnp.log1p(jnp.exp(-jnp.abs(v)))


def _mlstm_kernel(xqk_ref, halo_ref, xvt_ref, xot_ref, gt_ref, gc_ref, convw_ref, convb_ref,
                  gain_ref, cum_ref, cumt_ref, o_ref, ext_ref, c_ref, n_ref, m_ref,
                  qm_ref, k_ref, pv_ref, row_ref, kv_ref, nk_ref, cprev_ref, nprev_ref, ht_ref):
    tt = xqk_ref.shape[0]
    lc = MLSTM_CHUNK
    i = pl.program_id(1)

    @pl.when(i == 0)
    def _():
        c_ref[...] = jnp.zeros_like(c_ref)
        n_ref[...] = jnp.zeros_like(n_ref)
        m_ref[...] = jnp.zeros_like(m_ref)

    hist = halo_ref[...].astype(F32)
    ext_ref[0:CONV_HALO, :] = jnp.where(i > 0, hist, jnp.zeros_like(hist))
    ext_ref[CONV_HALO:CONV_HALO + tt, :] = xqk_ref[...].astype(F32)
    convw = convw_ref[...]
    conv = jnp.zeros((tt, 2 * MLSTM_WIDTH), F32) + convb_ref[...]
    for j in range(CONV_WIDTH):
        start = CONV_HALO - (CONV_WIDTH - 1) + j
        conv = conv + convw[j:j + 1, :] * ext_ref[start:start + tt, :]
    qk = conv * jax.nn.sigmoid(conv)
    q_all = qk[:, :MLSTM_WIDTH]
    lane_q = lax.broadcasted_iota(jnp.int32, q_all.shape, 1)
    even_head = (lane_q % HEAD_PAIR) < HEAD_DIM
    qm_ref[0] = jnp.where(even_head, q_all, 0.0).astype(BF16)
    qm_ref[1] = jnp.where(even_head, 0.0, q_all).astype(BF16)
    k_ref[...] = (qk[:, MLSTM_WIDTH:] * SM_SCALE).astype(BF16)

    g_row = gt_ref[...]
    g_col = gc_ref[...]
    cum = cum_ref[...]
    cumt = cumt_ref[...]
    b_row = sum(_dot(piece, cum) for piece in _split3_bf16(_log_sigmoid(g_row)))
    b_col = sum(_dot(cumt, piece) for piece in _split3_bf16(_log_sigmoid(g_col)))
    u_col = b_col[:, MLSTM_HEADS:] - g_col[:, :MLSTM_HEADS]

    lane = lax.broadcasted_iota(jnp.int32, (HEAD_DIM, HEAD_PAIR), 1)
    s_idx = lax.broadcasted_iota(jnp.int32, (lc, lc), 0)
    t_idx = lax.broadcasted_iota(jnp.int32, (lc, lc), 1)
    causal = s_idx <= t_idx
    nchunks = tt // lc
    heads = range(MLSTM_HEADS)
    in_head = [(lane < HEAD_DIM), (lane >= HEAD_DIM)]

    def operands(c, h):
        p, hh = divmod(h, 2)
        sl = slice(c * lc, (c + 1) * lc)
        qm = qm_ref[hh, sl, p * HEAD_PAIR:(p + 1) * HEAD_PAIR]
        kp = k_ref[sl, p * HEAD_PAIR:(p + 1) * HEAD_PAIR]
        return sl, hh, qm, kp

    m_state = [m_ref[h][0:1, 0:lc] for h in heads]
    for c in range(nchunks):
        for h in heads:
            u = c * MLSTM_HEADS + h
            sl, hh, qm, kp = operands(c, h)
            vt = xvt_ref[h * HEAD_DIM:(h + 1) * HEAD_DIM, sl]
            br = b_row[MLSTM_HEADS + h:MLSTM_HEADS + h + 1, sl]
            ir = g_row[h:h + 1, sl]
            uc = u_col[sl, h:h + 1]
            a = br[:, lc - 1:lc]
            dt = jnp.where(causal, br - uc, NEG_INF)
            inter = br + m_state[h]
            m_t = jnp.maximum(inter, jnp.max(dt, axis=0, keepdims=True))
            w_inter = jnp.exp(inter - m_t)
            qkt = _dot_nt(kp, qm) * jnp.exp(dt - m_t)
            pv_ref[u] = _dot(vt, qkt.astype(BF16))
            row_ref[u, 0:1, 0:lc] = w_inter
            row_ref[u, 1:2, 0:lc] = jnp.sum(qkt, axis=0, keepdims=True)
            row_ref[u, 2:3, 0:lc] = jnp.exp(-m_t)
            g_end = a - br + ir
            m_new = jnp.maximum(a + m_state[h], jnp.max(g_end, axis=1, keepdims=True))
            w_s = jnp.exp(g_end - m_new)
            decay = jnp.exp(a + m_state[h] - m_new)
            row_ref[u, 3:4, 0:lc] = decay
            vtw = (vt.astype(F32) * w_s).astype(BF16)
            kv_ref[u] = jnp.where(in_head[hh], _dot(vtw, kp), 0.0)
            w_rows = jnp.broadcast_to(w_s, (8, lc)).astype(BF16)
            nk_ref[u] = jnp.where(in_head[hh][0:8], _dot(w_rows, kp), 0.0)
            m_state[h] = m_new

    for h in heads:
        c_state = c_ref[h]
        n_state = n_ref[h]
        for c in range(nchunks):
            u = c * MLSTM_HEADS + h
            cprev_ref[u] = c_state.astype(BF16)
            nprev_ref[u] = n_state
            decay = row_ref[u, 3:4, 0:HEAD_PAIR]
            c_state = decay * c_state + kv_ref[u]
            n_state = decay * n_state + nk_ref[u]
        c_ref[h] = c_state
        n_ref[h] = n_state
        m_ref[h] = jnp.broadcast_to(m_state[h][:, 0:1], m_ref.shape[1:])

    for c in range(nchunks):
        for h in heads:
            u = c * MLSTM_HEADS + h
            sl, hh, qm, _ = operands(c, h)
            w_inter = row_ref[u, 0:1, 0:lc]
            num = w_inter * _dot_nt(cprev_ref[u], qm) + pv_ref[u]
            den = w_inter * _dot_nt(nprev_ref[u].astype(BF16), qm)[0:1, :] + row_ref[u, 1:2, 0:lc]
            ht_ref[h * HEAD_DIM:(h + 1) * HEAD_DIM, sl] = num / jnp.maximum(jnp.abs(den), row_ref[u, 2:3, 0:lc])

    gain = jnp.concatenate([gain_ref[...]] * (tt // LANES_V7X), axis=1)
    outs = []
    for h in heads:
        rows = slice(h * HEAD_DIM, (h + 1) * HEAD_DIM)
        hg = ht_ref[rows, :] * jax.nn.sigmoid(xot_ref[rows, :].astype(F32))
        msq = jnp.mean(hg * hg, axis=0, keepdims=True)
        outs.append(hg * lax.rsqrt(msq + NORM_EPS) * gain[rows])
    o_ref[...] = jnp.concatenate(outs, axis=0).T.astype(BF16)


def _mlstm(xqk, xvt, xot, gt, gc, convw, convb, gain, cum, batch, seq):
    t = xqk.shape[0]
    tt = MLSTM_ROWS
    steps = seq // tt
    units = MLSTM_HEADS * (tt // MLSTM_CHUNK)
    col = lambda r: pl.BlockSpec((r, tt), lambda b, i: (0, b * steps + i))
    return pl.pallas_call(
        _mlstm_kernel,
        grid=(batch, steps),
        in_specs=[
            pl.BlockSpec((tt, 2 * MLSTM_WIDTH), lambda b, i: (b * steps + i, 0)),
            pl.BlockSpec((CONV_HALO, 2 * MLSTM_WIDTH),
                         lambda b, i: (jnp.maximum((b * seq + i * tt) // CONV_HALO - 1, 0), 0)),
            col(MLSTM_WIDTH),
            col(MLSTM_WIDTH),
            col(NUM_GATES),
            pl.BlockSpec((tt, NUM_GATES), lambda b, i: (b * steps + i, 0)),
            _const_spec(convw.shape),
            _const_spec(convb.shape),
            _const_spec(gain.shape),
            _const_spec(cum.shape),
            _const_spec(cum.shape),
        ],
        out_specs=pl.BlockSpec((tt, MLSTM_WIDTH), lambda b, i: (b * steps + i, 0)),
        out_shape=jax.ShapeDtypeStruct((t, MLSTM_WIDTH), BF16),
        scratch_shapes=[
            pltpu.VMEM((CONV_HALO + tt, 2 * MLSTM_WIDTH), F32),
            pltpu.VMEM((MLSTM_HEADS, HEAD_DIM, HEAD_PAIR), F32),
            pltpu.VMEM((MLSTM_HEADS, 8, HEAD_PAIR), F32),
            pltpu.VMEM((MLSTM_HEADS, 8, MLSTM_CHUNK), F32),
            pltpu.VMEM((2, tt, MLSTM_WIDTH), BF16),
            pltpu.VMEM((tt, MLSTM_WIDTH), BF16),
            pltpu.VMEM((units, HEAD_DIM, MLSTM_CHUNK), F32),
            pltpu.VMEM((units, 8, MLSTM_CHUNK), F32),
            pltpu.VMEM((units, HEAD_DIM, HEAD_PAIR), F32),
            pltpu.VMEM((units, 8, HEAD_PAIR), F32),
            pltpu.VMEM((units, HEAD_DIM, HEAD_PAIR), BF16),
            pltpu.VMEM((units, 8, HEAD_PAIR), F32),
            pltpu.VMEM((MLSTM_WIDTH, tt), F32),
        ],
        compiler_params=pltpu.CompilerParams(dimension_semantics=("arbitrary", "arbitrary")),
        name="mlstm",
    )(xqk, xqk, xvt, xot, gt, gc, convw, convb, gain, cum, cum.T)


def _out_mlp_kernel(x_ref, ym_ref, yl_ref, ys_ref, wo_ref, ln2_ref, wup_ref, wdn_ref, o_ref):
    y = jnp.concatenate([ym_ref[...], yl_ref[...], ys_ref[...]], axis=1)
    x1 = x_ref[...] + _dot(y, wo_ref[...])
    ms = jnp.mean(x1 * x1, axis=-1, keepdims=True)
    hn = (x1 * lax.rsqrt(ms + NORM_EPS) * ln2_ref[...]).astype(BF16)
    o_ref[...] = x1
    for c in range(0, D_FF, MLP_FF_CHUNK):
        u = _dot(hn, wup_ref[:, c:c + MLP_FF_CHUNK])
        act = jnp.square(jnp.maximum(u, 0.0)).astype(BF16)
        o_ref[...] += _dot(act, wdn_ref[c:c + MLP_FF_CHUNK, :])


def _out_mlp(xf, ym, yl, ys, wo, ln2, wup, wdn):
    t = xf.shape[0]
    tm = OUT_MLP_ROWS
    row = lambda w: pl.BlockSpec((tm, w), lambda i: (i, 0))
    weights = 2 * (wo.size + wup.size + wdn.size)
    tiles = tm * (4 * 4 * D_MODEL + 2 * 2 * D_MODEL + 4 * 4 * D_MODEL + 6 * MLP_FF_CHUNK)
    return pl.pallas_call(
        _out_mlp_kernel,
        grid=(t // tm,),
        in_specs=[
            row(D_MODEL), row(MOBA_WIDTH), row(MLSTM_WIDTH), row(SWA_Q_WIDTH),
            _const_spec(wo.shape), _const_spec(ln2.shape), _const_spec(wup.shape), _const_spec(wdn.shape),
        ],
        out_specs=row(D_MODEL),
        out_shape=jax.ShapeDtypeStruct((t, D_MODEL), F32),
        compiler_params=pltpu.CompilerParams(
            dimension_semantics=("arbitrary",),
            vmem_limit_bytes=_vmem_limit(weights + tiles + (8 << 20))),
        name="out_mlp",
    )(xf, ym, yl, ys, wo, ln2, wup, wdn)


def _rope_tables(seq):
    inv = ROPE_THETA ** (-jnp.arange(0, HEAD_DIM, 2, dtype=F32) / HEAD_DIM)
    ang = jnp.arange(seq, dtype=F32)[:, None] * inv[None, :]
    cos, sin = jnp.cos(ang), jnp.sin(ang)
    cosn = jnp.concatenate([cos, cos, cos, cos], axis=1)
    sinn = jnp.concatenate([-sin, sin, -sin, sin], axis=1)
    return cosn, sinn, cos.T, sin.T


def _layer(xf, tables, consts, batch, seq, ln1, w_in, conv_w, conv_b, igate_b, fgate_b, mlstm_norm,
           moba_q_norm, moba_k_norm, swa_q_norm, swa_k_norm, swa_sinks, w_out, ln2, w_up, w_down):
    cosn, sinn, cost, sint = tables
    bd, cum = consts
    o = 0
    cols = {}
    for name, width in (("mq", MOBA_WIDTH), ("mk", MOBA_WIDTH), ("mv", MOBA_WIDTH), ("sq", SWA_Q_WIDTH),
                        ("sk", SWA_KV_WIDTH), ("sv", SWA_KV_WIDTH), ("xqk", 2 * MLSTM_WIDTH),
                        ("xv", MLSTM_WIDTH), ("xo", MLSTM_WIDTH), ("xi", MLSTM_HEADS), ("xf", MLSTM_HEADS)):
        cols[name] = w_in[:, o:o + width]
        o += width
    wnat = jnp.concatenate([cols["mk"], cols["sk"], cols["xqk"]], axis=1).astype(BF16)
    wtr = jnp.concatenate([cols["mq"], cols["sq"], cols["mv"], cols["sv"], cols["xv"], cols["xo"],
                           cols["xi"], cols["xf"]], axis=1).T.astype(BF16)
    gk = jnp.concatenate([jnp.tile(moba_k_norm, MOBA_HEADS), jnp.tile(swa_k_norm, SWA_KV_HEADS)])[None, :]
    gq = jnp.broadcast_to(
        (jnp.concatenate([moba_q_norm, swa_q_norm]) * (SM_SCALE * LOG2E))[:, None], (2 * HEAD_DIM, LANES_V7X))
    gbias = jnp.broadcast_to(jnp.concatenate([igate_b, fgate_b])[:, None], (NUM_GATES, LANES_V7X))

    (mk, kmean, sk, xqk, mqt, sqt, mvt, svt, xvt, xot, gt) = _in_proj(
        xf, ln1[None, :], wnat, wtr, bd, gk, cosn, sinn, gq, cost, sint, gbias, seq)

    kmean = kmean.reshape(-1, MOBA_WIDTH)
    def bounded(q_gain, k_gain, *extra):
        bound = (HEAD_DIM * SM_SCALE * LOG2E) * jnp.max(jnp.abs(q_gain)) * jnp.max(jnp.abs(k_gain))
        for e in extra:
            bound = jnp.maximum(bound, jnp.max(jnp.abs(e)) * LOG2E)
        return (bound <= MOBA_SAFE_LOG2).astype(jnp.int32).reshape(1)

    ym = _moba(bounded(moba_q_norm, moba_k_norm), mqt, mk, mvt, kmean, batch, seq)
    ys = _swa(bounded(swa_q_norm, swa_k_norm, swa_sinks), swa_sinks, sqt, sk, svt, batch, seq)
    gain = jnp.broadcast_to(mlstm_norm.reshape(MLSTM_WIDTH, 1), (MLSTM_WIDTH, LANES_V7X))
    yl = _mlstm(xqk, xvt, xot, gt, gt.T, conv_w, conv_b[None, :], gain, cum, batch, seq)

    return _out_mlp(xf, ym, yl, ys, w_out.astype(BF16), ln2[None, :], w_up.astype(BF16), w_down.astype(BF16))


def kernel(x, ln1, w_in, conv_w, conv_b, igate_b, fgate_b, mlstm_norm, moba_q_norm, moba_k_norm,
           swa_q_norm, swa_k_norm, swa_sinks, w_out, ln2, w_up, w_down):
    batch, seq, d = x.shape
    assert d == D_MODEL and seq % max(IN_PROJ_ROWS, SWA_ROWS, MLSTM_ROWS, MOBA_BLOCK) == 0
    depth = ln1.shape[0]
    tables = _rope_tables(seq)
    bw = 2 * LANES_V7X
    ids = jnp.arange(bw) // HEAD_DIM
    bd = jnp.where(ids[:, None] == ids[None, :], 1.0 / HEAD_DIM, 0.0).astype(BF16)
    tids = jnp.arange(MLSTM_ROWS)
    cum = ((tids[:, None] // MLSTM_CHUNK == tids[None, :] // MLSTM_CHUNK)
           & (tids[:, None] <= tids[None, :])).astype(BF16)
    xf = x.reshape(batch * seq, d)
    for l in range(depth):
        xf = _layer(xf, tables, (bd, cum), batch, seq, ln1[l], w_in[l], conv_w[l], conv_b[l], igate_b[l],
                    fgate_b[l], mlstm_norm[l], moba_q_norm[l], moba_k_norm[l], swa_q_norm[l],
                    swa_k_norm[l], swa_sinks[l], w_out[l], ln2[l], w_up[l], w_down[l])
    return xf.reshape(batch, seq, d)
```

```python
import functools

import jax
import jax.numpy as jnp
from jax import lax
from jax.experimental import pallas as pl
from jax.experimental.pallas import tpu as pltpu

F32 = jnp.float32
BF16 = jnp.bfloat16
NEG_INF = float("-inf")

D_MODEL = 1024
HEAD_DIM = 64
MOBA_HEADS = 6
MLSTM_HEADS = 4
SWA_Q_HEADS = 6
SWA_KV_HEADS = 2
SWA_GROUP = SWA_Q_HEADS // SWA_KV_HEADS
MOBA_WIDTH = MOBA_HEADS * HEAD_DIM
MLSTM_WIDTH = MLSTM_HEADS * HEAD_DIM
SWA_Q_WIDTH = SWA_Q_HEADS * HEAD_DIM
SWA_KV_WIDTH = SWA_KV_HEADS * HEAD_DIM
MOBA_BLOCK = 256
MOBA_TOPK = 3
MLSTM_CHUNK = 256
CONV_WIDTH = 4
SWA_WINDOW = 128
ROPE_THETA = 10000.0
D_FF = 4 * D_MODEL
NORM_EPS = 1e-6
SM_SCALE = HEAD_DIM ** -0.5
LOG2E = 1.4426950408889634
MOBA_V_ROWS = HEAD_DIM + 16
MOBA_SAFE_LOG2 = 60.0

LANES_V7X = 128
VMEM_BYTES_V7X = 64 * 1024 * 1024
HEAD_PAIR = 2 * HEAD_DIM
assert HEAD_PAIR == LANES_V7X

NAT_WIDTH = MOBA_WIDTH + SWA_KV_WIDTH + 2 * MLSTM_WIDTH
KN_WIDTH = MOBA_WIDTH + SWA_KV_WIDTH
QT_ROWS = MOBA_WIDTH + SWA_Q_WIDTH
TR_ROWS = QT_ROWS + MOBA_WIDTH + SWA_KV_WIDTH + 2 * MLSTM_WIDTH + 2 * MLSTM_HEADS
NUM_GATES = 2 * MLSTM_HEADS

IN_PROJ_ROWS = 1024
OUT_MLP_ROWS = 1024
MLP_FF_CHUNK = 1024
SWA_ROWS = 512
MLSTM_ROWS = 256
MOBA_GROUP = 2
MOBA_Q_BLOCKS = 1
CONV_HALO = 16


def _dot(a, b):
    return jnp.dot(a, b, preferred_element_type=F32)


def _dot_nt(a, b):
    return lax.dot_general(a, b, (((1,), (1,)), ((), ())), preferred_element_type=F32)


def _split_bf16(v):
    hi = v.astype(BF16)
    lo = (v - hi.astype(F32)).astype(BF16)
    return hi, lo


def _split3_bf16(v):
    hi = v.astype(BF16)
    rest = v - hi.astype(F32)
    mid = rest.astype(BF16)
    return hi, mid, (rest - mid.astype(F32)).astype(BF16)


def _vmem_limit(nbytes):
    return int(min(nbytes, VMEM_BYTES_V7X - 4 * 1024 * 1024))


def _const_spec(shape):
    nd = len(shape)
    return pl.BlockSpec(shape, lambda *_: (0,) * nd, pipeline_mode=pl.Buffered(1))


def _in_proj_kernel(x_ref, ln1_ref, wnat_ref, wtr_ref, bd_ref, gk_ref, cosn_ref, sinn_ref,
                    gq_ref, cost_ref, sint_ref, gbias_ref,
                    mk_ref, kmean_ref, sk_ref, xqk_ref, mqt_ref, sqt_ref, mvt_ref, svt_ref,
                    xvt_ref, xot_ref, gt_ref):
    tm = x_ref.shape[0]
    x = x_ref[...]
    ms = jnp.mean(x * x, axis=-1, keepdims=True)
    hn = (x * lax.rsqrt(ms + NORM_EPS) * ln1_ref[...]).astype(BF16)
    nat = _dot(hn, wnat_ref[...])

    kk = nat[:, :KN_WIDTH]
    hi, lo = _split_bf16(kk * kk)
    bd = bd_ref[...]
    bw = bd.shape[0]
    msk = jnp.concatenate(
        [_dot(hi[:, c:c + bw], bd) + _dot(lo[:, c:c + bw], bd) for c in range(0, KN_WIDTH, bw)],
        axis=1)
    kn = kk * lax.rsqrt(msk + NORM_EPS) * gk_ref[...]
    reps = KN_WIDTH // LANES_V7X
    cosn = jnp.concatenate([cosn_ref[...]] * reps, axis=1)
    sinn = jnp.concatenate([sinn_ref[...]] * reps, axis=1)
    lane = lax.broadcasted_iota(jnp.int32, kn.shape, 1)
    first_half = (lane % HEAD_DIM) < (HEAD_DIM // 2)
    swapped = jnp.where(first_half,
                        pltpu.roll(kn, KN_WIDTH - HEAD_DIM // 2, 1),
                        pltpu.roll(kn, HEAD_DIM // 2, 1))
    kr = kn * cosn + swapped * sinn
    mk = kr[:, :MOBA_WIDTH]
    mk_ref[...] = mk.astype(BF16)
    nblk = tm // MOBA_BLOCK
    kmean_ref[0] = jnp.concatenate(
        [jnp.mean(mk[c * MOBA_BLOCK:(c + 1) * MOBA_BLOCK], axis=0, keepdims=True) for c in range(nblk)],
        axis=0)
    sk_ref[...] = kr[:, MOBA_WIDTH:].astype(BF16)
    xqk_ref[...] = nat[:, KN_WIDTH:].astype(BF16)

    tr = _dot_nt(wtr_ref[...], hn)
    cost = cost_ref[...]
    sint = sint_ref[...]
    gq = jnp.concatenate([gq_ref[...]] * (tm // LANES_V7X), axis=1)
    half = HEAD_DIM // 2
    for h in range(QT_ROWS // HEAD_DIM):
        blk = tr[h * HEAD_DIM:(h + 1) * HEAD_DIM]
        is_swa = h >= MOBA_HEADS
        gain = gq[HEAD_DIM:] if is_swa else gq[:HEAD_DIM]
        msq = jnp.mean(blk * blk, axis=0, keepdims=True)
        qn = blk * lax.rsqrt(msq + NORM_EPS) * gain
        x1, x2 = qn[:half], qn[half:]
        rot = jnp.concatenate([x1 * cost - x2 * sint, x2 * cost + x1 * sint], axis=0).astype(BF16)
        if is_swa:
            r0 = (h - MOBA_HEADS) * HEAD_DIM
            sqt_ref[r0:r0 + HEAD_DIM, :] = rot
        else:
            mqt_ref[h * HEAD_DIM:(h + 1) * HEAD_DIM, :] = rot
    r = QT_ROWS
    mv = tr[r:r + MOBA_WIDTH].astype(BF16)
    ones = jnp.ones((MOBA_V_ROWS - HEAD_DIM, MOBA_BLOCK), BF16)
    for c in range(nblk):
        for h in range(MOBA_HEADS):
            r0 = h * MOBA_V_ROWS
            mvt_ref[c, r0:r0 + HEAD_DIM, :] = mv[h * HEAD_DIM:(h + 1) * HEAD_DIM,
                                                 c * MOBA_BLOCK:(c + 1) * MOBA_BLOCK]
            mvt_ref[c, r0 + HEAD_DIM:r0 + MOBA_V_ROWS, :] = ones
    r += MOBA_WIDTH
    svt_ref[...] = tr[r:r + SWA_KV_WIDTH].astype(BF16)
    r += SWA_KV_WIDTH
    xvt_ref[...] = tr[r:r + MLSTM_WIDTH].astype(BF16)
    r += MLSTM_WIDTH
    xot_ref[...] = tr[r:r + MLSTM_WIDTH].astype(BF16)
    r += MLSTM_WIDTH
    gbias = jnp.concatenate([gbias_ref[...]] * (tm // LANES_V7X), axis=1)
    gt_ref[...] = tr[r:r + NUM_GATES] + gbias


def _in_proj(xf, ln1, wnat, wtr, bd, gk, cosn, sinn, gq, cost, sint, gbias, seq):
    t = xf.shape[0]
    tm = IN_PROJ_ROWS
    steps = t // tm
    seq_steps = seq // tm
    nblk = tm // MOBA_BLOCK
    row = lambda w: pl.BlockSpec((tm, w), lambda i: (i, 0))
    col = lambda r: pl.BlockSpec((r, tm), lambda i: (0, i))
    in_specs = [
        row(D_MODEL),
        _const_spec((1, D_MODEL)),
        _const_spec(wnat.shape),
        _const_spec(wtr.shape),
        _const_spec(bd.shape),
        _const_spec(gk.shape),
        pl.BlockSpec((tm, LANES_V7X), lambda i: (i % seq_steps, 0)),
        pl.BlockSpec((tm, LANES_V7X), lambda i: (i % seq_steps, 0)),
        _const_spec(gq.shape),
        pl.BlockSpec((HEAD_DIM // 2, tm), lambda i: (0, i % seq_steps)),
        pl.BlockSpec((HEAD_DIM // 2, tm), lambda i: (0, i % seq_steps)),
        _const_spec(gbias.shape),
    ]
    out_shape = [
        jax.ShapeDtypeStruct((t, MOBA_WIDTH), BF16),
        jax.ShapeDtypeStruct((steps, nblk, MOBA_WIDTH), F32),
        jax.ShapeDtypeStruct((t, SWA_KV_WIDTH), BF16),
        jax.ShapeDtypeStruct((t, 2 * MLSTM_WIDTH), BF16),
        jax.ShapeDtypeStruct((MOBA_WIDTH, t), BF16),
        jax.ShapeDtypeStruct((SWA_Q_WIDTH, t), BF16),
        jax.ShapeDtypeStruct((t // MOBA_BLOCK, MOBA_HEADS * MOBA_V_ROWS, MOBA_BLOCK), BF16),
        jax.ShapeDtypeStruct((SWA_KV_WIDTH, t), BF16),
        jax.ShapeDtypeStruct((MLSTM_WIDTH, t), BF16),
        jax.ShapeDtypeStruct((MLSTM_WIDTH, t), BF16),
        jax.ShapeDtypeStruct((NUM_GATES, t), F32),
    ]
    out_specs = [
        row(MOBA_WIDTH),
        pl.BlockSpec((1, nblk, MOBA_WIDTH), lambda i: (i, 0, 0)),
        row(SWA_KV_WIDTH),
        row(2 * MLSTM_WIDTH),
        col(MOBA_WIDTH),
        col(SWA_Q_WIDTH),
        pl.BlockSpec((nblk, MOBA_HEADS * MOBA_V_ROWS, MOBA_BLOCK), lambda i: (i, 0, 0)),
        col(SWA_KV_WIDTH),
        col(MLSTM_WIDTH),
        col(MLSTM_WIDTH),
        col(NUM_GATES),
    ]
    weights = 2 * (wnat.size + wtr.size)
    tiles = tm * (2 * 4 * D_MODEL + 3 * 4 * (NAT_WIDTH + TR_ROWS) + 2 * 2 * (NAT_WIDTH + TR_ROWS))
    return pl.pallas_call(
        _in_proj_kernel,
        grid=(steps,),
        in_specs=in_specs,
        out_specs=out_specs,
        out_shape=out_shape,
        compiler_params=pltpu.CompilerParams(
            dimension_semantics=("arbitrary",),
            vmem_limit_bytes=_vmem_limit(2 * weights + tiles + (8 << 20))),
        name="in_proj",
    )(xf, ln1, wnat, wtr, bd, gk, cosn, sinn, gq, cost, sint, gbias)


def _moba_kernel(bounded_ref, qt_ref, k_ref, vt_ref, km_ref, o_ref, sel_ref, s_ref, p_ref):
    nb = k_ref.shape[0]
    lq = MOBA_BLOCK
    nh = MOBA_Q_BLOCKS * MOBA_HEADS
    step = pl.program_id(1)
    row = lax.broadcasted_iota(jnp.int32, (HEAD_PAIR, lq), 0)
    own, qh = [], []
    for s in range(nh):
        qb, h = divmod(s, MOBA_HEADS)
        own.append(step * MOBA_Q_BLOCKS + qb)
        qt = qt_ref[(h // 2) * HEAD_PAIR:(h // 2 + 1) * HEAD_PAIR, qb * lq:(qb + 1) * lq]
        keep = (row < HEAD_DIM) if h % 2 == 0 else (row >= HEAD_DIM)
        qh.append(jnp.where(keep, qt, jnp.zeros_like(qt)))
    last_own = step * MOBA_Q_BLOCKS + (MOBA_Q_BLOCKS - 1)

    def k_group(blk0, count, s):
        g = (s % MOBA_HEADS) // 2
        return k_ref[pl.ds(blk0, count), :, g * HEAD_PAIR:(g + 1) * HEAD_PAIR].reshape(count * lq, HEAD_PAIR)

    def v_aug(j, s):
        h = s % MOBA_HEADS
        return vt_ref[j, h * MOBA_V_ROWS:(h + 1) * MOBA_V_ROWS, :]

    def select_blocks():
        gates = []
        for s in range(nh):
            g = (s % MOBA_HEADS) // 2
            km_hi, km_lo = _split_bf16(km_ref[:, g * HEAD_PAIR:(g + 1) * HEAD_PAIR])
            gates.append(_dot(km_hi, qh[s]) + _dot(km_lo, qh[s]))
        gate = jnp.concatenate(gates, axis=1)
        own_all = jnp.concatenate([jnp.full((1, lq), own[s], jnp.int32) for s in range(nh)], axis=1)
        blk_all = lax.broadcasted_iota(jnp.int32, gate.shape, 0)
        gate = jnp.where(blk_all < own_all, gate, NEG_INF)
        sel = jnp.full(gate.shape, NEG_INF, F32)
        for _ in range(MOBA_TOPK):
            mx = jnp.max(gate, axis=0, keepdims=True)
            idx = jnp.min(jnp.where(gate == mx, blk_all, nb), axis=0, keepdims=True)
            pick = blk_all == jnp.where(idx < own_all, idx, nb)
            sel = jnp.where(pick, 0.0, sel)
            gate = jnp.where(pick, NEG_INF, gate)
        for s in range(nh):
            sel_ref[s] = sel[:, s * lq:(s + 1) * lq]

    kpos = lax.broadcasted_iota(jnp.int32, (lq, lq), 0)
    qpos = lax.broadcasted_iota(jnp.int32, (lq, lq), 1)
    causal = kpos <= qpos

    def weighted_values(st, mu, v_rows):
        pv = _dot(v_rows, jnp.exp2(st - mu).astype(BF16))
        return pv[HEAD_DIM:HEAD_DIM + 1], pv[:HEAD_DIM]

    def finish(nums, dens):
        for s in range(0, nh, 2):
            qb, h = divmod(s, MOBA_HEADS)
            ot = jnp.concatenate([nums[s] / dens[s], nums[s + 1] / dens[s + 1]], axis=0)
            o_ref[qb * lq:(qb + 1) * lq, (h // 2) * HEAD_PAIR:(h // 2 + 1) * HEAD_PAIR] = ot.T.astype(BF16)

    trips = lax.shift_right_logical(last_own + (2 * MOBA_GROUP - 1), MOBA_GROUP.bit_length())
    units = nh * MOBA_GROUP
    own_slot = 2 * units

    @pl.when(bounded_ref[0] != 0)
    def _():
        select_blocks()
        for h in range(nh):
            st = jnp.where(causal, _dot(k_group(own[h], 1, h), qh[h]), NEG_INF)
            p_ref[own_slot + h] = jnp.exp2(st).astype(BF16)

        def stage_p(blk0, slot0):
            blk0 = jnp.minimum(blk0, nb - MOBA_GROUP)
            for h in range(nh):
                st = _dot(k_group(blk0, MOBA_GROUP, h), qh[h])
                sel = jnp.concatenate(
                    [jnp.broadcast_to(sel_ref[h, pl.ds(blk0 + u, 1), :], (lq, lq)) for u in range(MOBA_GROUP)],
                    axis=0)
                slot = slot0 + MOBA_GROUP * h
                p_ref[slot:slot + MOBA_GROUP] = jnp.exp2(st + sel).astype(BF16).reshape(MOBA_GROUP, lq, lq)

        def stage_v(blk0, slot0, acc):
            acc = list(acc)
            for h in range(nh):
                slot = slot0 + MOBA_GROUP * h
                p = p_ref[slot:slot + MOBA_GROUP].reshape(MOBA_GROUP * lq, lq)
                v = jnp.concatenate([v_aug(blk0 + u, h) for u in range(MOBA_GROUP)], axis=1)
                acc[h] = acc[h] + _dot(v, p)
            return acc

        def trip(t, acc, last):
            blk = t * (2 * MOBA_GROUP)
            stage_p(blk + MOBA_GROUP, units)
            acc = stage_v(blk, 0, acc)
            if not last:
                stage_p(blk + 2 * MOBA_GROUP, 0)
            return tuple(stage_v(blk + MOBA_GROUP, units, acc))

        stage_p(0, 0)
        acc = tuple(_dot(v_aug(own[h], h), p_ref[own_slot + h]) for h in range(nh))
        acc = lax.fori_loop(0, jnp.maximum(trips - 1, 0), lambda t, a: trip(t, a, False), acc)
        acc = lax.cond(trips > 0, lambda a: trip(trips - 1, a, True), lambda a: a, acc)
        finish([a[:HEAD_DIM] for a in acc], [a[HEAD_DIM:HEAD_DIM + 1] for a in acc])


    def stage_a(blk0, slot0):
        mus = []
        for u in range(MOBA_GROUP):
            for h in range(nh):
                st = _dot(k_group(jnp.minimum(blk0 + u, nb - 1), 1, h), qh[h])
                s_ref[slot0 + nh * u + h] = st
                mus.append(jnp.max(st, axis=0, keepdims=True))
        return mus

    def stage_b(blk0, slot0, mus, state):
        state = list(state)
        for u in range(MOBA_GROUP):
            j = blk0 + u
            for h in range(nh):
                mu = mus[nh * u + h]
                ls, pv = weighted_values(s_ref[slot0 + nh * u + h], mu, v_aug(j, h))
                m, l, acc = state[3 * h:3 * h + 3]
                mu_sel = mu + sel_ref[h, pl.ds(j, 1), :]
                m_new = jnp.maximum(m, mu_sel)
                f = jnp.exp2(mu_sel - m_new)
                c = jnp.exp2(m - m_new)
                state[3 * h:3 * h + 3] = [m_new, c * l + f * ls, c * acc + f * pv]
        return state

    def body(t, loop_carry):
        mus0, state = loop_carry[:units], loop_carry[units:]
        blk = t * (2 * MOBA_GROUP)
        mus1 = stage_a(blk + MOBA_GROUP, units)
        state = stage_b(blk, 0, mus0, state)
        mus0 = stage_a(blk + 2 * MOBA_GROUP, 0)
        state = stage_b(blk + MOBA_GROUP, units, mus1, state)
        return (*mus0, *state)

    @pl.when(bounded_ref[0] == 0)
    def _():
        state = []
        for h in range(nh):
            st = jnp.where(causal, _dot(k_group(own[h], 1, h), qh[h]), NEG_INF)
            mu = jnp.max(st, axis=0, keepdims=True)
            state += [mu, *weighted_values(st, mu, v_aug(own[h], h))]
        mus0 = stage_a(0, 0)
        select_blocks()
        state = lax.fori_loop(0, trips, body, (*mus0, *state))[units:]
        finish(state[2::3], state[1::3])


def _moba(bounded, mqt, mk, mvt, kmean, batch, seq):
    t = mk.shape[0]
    nb = seq // MOBA_BLOCK
    lq = MOBA_BLOCK
    k3 = mk.reshape(t // MOBA_BLOCK, MOBA_BLOCK, MOBA_WIDTH)
    assert nb % (2 * MOBA_GROUP) == 0 and nb % MOBA_Q_BLOCKS == 0
    streams = MOBA_Q_BLOCKS * MOBA_HEADS
    units = streams * MOBA_GROUP
    steps = nb // MOBA_Q_BLOCKS
    kv_bytes = 2 * nb * MOBA_BLOCK * (MOBA_WIDTH + MOBA_HEADS * MOBA_V_ROWS)
    scratch_bytes = lq * lq * (4 * 2 * units + 2 * (2 * units + streams)) + 4 * streams * nb * lq
    resident = lambda shape, imap: pl.BlockSpec(shape, imap, pipeline_mode=pl.Buffered(1))
    return pl.pallas_call(
        _moba_kernel,
        grid=(batch, steps),
        in_specs=[
            pl.BlockSpec(memory_space=pltpu.SMEM),
            pl.BlockSpec((MOBA_WIDTH, MOBA_Q_BLOCKS * lq), lambda b, i: (0, b * steps + i)),
            resident((nb, MOBA_BLOCK, MOBA_WIDTH), lambda b, i: (b, 0, 0)),
            resident((nb, MOBA_HEADS * MOBA_V_ROWS, MOBA_BLOCK), lambda b, i: (b, 0, 0)),
            resident((nb, MOBA_WIDTH), lambda b, i: (b, 0)),
        ],
        out_specs=pl.BlockSpec((MOBA_Q_BLOCKS * lq, MOBA_WIDTH), lambda b, i: (b * steps + i, 0)),
        out_shape=jax.ShapeDtypeStruct((t, MOBA_WIDTH), BF16),
        scratch_shapes=[pltpu.VMEM((streams, nb, lq), F32),
                        pltpu.VMEM((2 * units, lq, lq), F32),
                        pltpu.VMEM((2 * units + streams, lq, lq), BF16)],
        compiler_params=pltpu.CompilerParams(
            dimension_semantics=("arbitrary", "arbitrary"),
            vmem_limit_bytes=_vmem_limit(kv_bytes + scratch_bytes + (12 << 20))),
        name="moba",
    )(bounded, mqt, k3, mvt, kmean)


def _swa_kernel(bounded_ref, sink_ref, qt_ref, k_ref, kh_ref, vt_ref, vth_ref, o_ref, p_ref):
    tq = k_ref.shape[0]
    w = SWA_WINDOW
    i = pl.program_id(1)
    ncol = SWA_GROUP * w
    kpos = lax.broadcasted_iota(jnp.int32, (2 * w, ncol), 0)
    col = lax.broadcasted_iota(jnp.int32, (2 * w, ncol), 1)
    diff = w + (col % w) - kpos
    in_window = (diff >= 0) & (diff < w)
    zeros = jnp.zeros((HEAD_DIM, ncol), BF16)
    ones = jnp.ones((MOBA_V_ROWS - HEAD_DIM, 2 * w), BF16)

    def attend(shifted):
        sink_terms = []
        for r in range(tq // w):
            if r == 0:
                kcat = jnp.concatenate([kh_ref[...], k_ref[0:w, :]], axis=0)
                mask = in_window & ((kpos >= w) | (i > 0))
            else:
                kcat = k_ref[(r - 1) * w:(r + 1) * w, :]
                mask = in_window
            for g in range(SWA_KV_HEADS):
                qs = jnp.concatenate(
                    [qt_ref[(SWA_GROUP * g + a) * HEAD_DIM:(SWA_GROUP * g + a + 1) * HEAD_DIM, r * w:(r + 1) * w]
                     for a in range(SWA_GROUP)], axis=1)
                qpad = jnp.concatenate([qs, zeros] if g == 0 else [zeros, qs], axis=0)
                st = jnp.where(mask, _dot(kcat, qpad), NEG_INF)
                sink = jnp.concatenate(
                    [jnp.full((1, w), sink_ref[SWA_GROUP * g + a] * LOG2E, F32) for a in range(SWA_GROUP)],
                    axis=1)
                if shifted:
                    m = jnp.maximum(jnp.max(st, axis=0, keepdims=True), sink)
                    st, sink = st - m, sink - m
                p_ref[r * SWA_KV_HEADS + g] = jnp.exp2(st).astype(BF16)
                sink_terms.append(jnp.exp2(sink))
        for r in range(tq // w):
            if r == 0:
                vcat = jnp.concatenate([vth_ref[...], vt_ref[:, 0:w]], axis=1)
            else:
                vcat = vt_ref[:, (r - 1) * w:(r + 1) * w]
            pieces = []
            for g in range(SWA_KV_HEADS):
                v_aug = jnp.concatenate([vcat[g * HEAD_DIM:(g + 1) * HEAD_DIM, :], ones], axis=0)
                pv = _dot(v_aug, p_ref[r * SWA_KV_HEADS + g])
                ot = pv[:HEAD_DIM] / (pv[HEAD_DIM:HEAD_DIM + 1] + sink_terms[r * SWA_KV_HEADS + g])
                pieces += [ot[:, a * w:(a + 1) * w] for a in range(SWA_GROUP)]
            o_ref[r * w:(r + 1) * w, :] = jnp.concatenate(pieces, axis=0).T.astype(BF16)

    @pl.when(bounded_ref[0] != 0)
    def _():
        attend(False)

    @pl.when(bounded_ref[0] == 0)
    def _():
        attend(True)


def _swa(bounded, sinks, sqt, sk, svt, batch, seq):
    t = sk.shape[0]
    tq = SWA_ROWS
    w = SWA_WINDOW
    steps = seq // tq
    halo = lambda b, i: jnp.maximum((b * seq + i * tq) // w - 1, 0)
    return pl.pallas_call(
        _swa_kernel,
        grid=(batch, steps),
        in_specs=[
            pl.BlockSpec(memory_space=pltpu.SMEM),
            pl.BlockSpec(memory_space=pltpu.SMEM),
            pl.BlockSpec((SWA_Q_WIDTH, tq), lambda b, i: (0, b * steps + i)),
            pl.BlockSpec((tq, SWA_KV_WIDTH), lambda b, i: (b * steps + i, 0)),
            pl.BlockSpec((w, SWA_KV_WIDTH), lambda b, i: (halo(b, i), 0)),
            pl.BlockSpec((SWA_KV_WIDTH, tq), lambda b, i: (0, b * steps + i)),
            pl.BlockSpec((SWA_KV_WIDTH, w), lambda b, i: (0, halo(b, i))),
        ],
        out_specs=pl.BlockSpec((tq, SWA_Q_WIDTH), lambda b, i: (b * steps + i, 0)),
        out_shape=jax.ShapeDtypeStruct((t, SWA_Q_WIDTH), BF16),
        scratch_shapes=[pltpu.VMEM(((tq // w) * SWA_KV_HEADS, 2 * w, SWA_GROUP * w), BF16)],
        compiler_params=pltpu.CompilerParams(dimension_semantics=("arbitrary", "arbitrary")),
        name="swa",
    )(bounded, sinks, sqt, sk, sk, svt, svt)


def _log_sigmoid(v):
    return jnp.minimum(v, 0.0) - jnp.log1p(jnp.exp(-jnp.abs(v)))


def _mlstm_kernel(xqk_ref, halo_ref, xvt_ref, xot_ref, gt_ref, gc_ref, convw_ref, convb_ref,
                  gain_ref, cum_ref, cumt_ref, o_ref, ext_ref, c_ref, n_ref, m_ref,
                  qm_ref, k_ref, pv_ref, row_ref, kv_ref, nk_ref, cprev_ref, nprev_ref, ht_ref):
    tt = xqk_ref.shape[0]
    lc = MLSTM_CHUNK
    i = pl.program_id(1)

    @pl.when(i == 0)
    def _():
        c_ref[...] = jnp.zeros_like(c_ref)
        n_ref[...] = jnp.zeros_like(n_ref)
        m_ref[...] = jnp.zeros_like(m_ref)

    hist = halo_ref[...].astype(F32)
    ext_ref[0:CONV_HALO, :] = jnp.where(i > 0, hist, jnp.zeros_like(hist))
    ext_ref[CONV_HALO:CONV_HALO + tt, :] = xqk_ref[...].astype(F32)
    convw = convw_ref[...]
    conv = jnp.zeros((tt, 2 * MLSTM_WIDTH), F32) + convb_ref[...]
    for j in range(CONV_WIDTH):
        start = CONV_HALO - (CONV_WIDTH - 1) + j
        conv = conv + convw[j:j + 1, :] * ext_ref[start:start + tt, :]
    qk = conv * jax.nn.sigmoid(conv)
    q_all = qk[:, :MLSTM_WIDTH]
    lane_q = lax.broadcasted_iota(jnp.int32, q_all.shape, 1)
    even_head = (lane_q % HEAD_PAIR) < HEAD_DIM
    qm_ref[0] = jnp.where(even_head, q_all, 0.0).astype(BF16)
    qm_ref[1] = jnp.where(even_head, 0.0, q_all).astype(BF16)
    k_ref[...] = (qk[:, MLSTM_WIDTH:] * SM_SCALE).astype(BF16)

    g_row = gt_ref[...]
    g_col = gc_ref[...]
    cum = cum_ref[...]
    cumt = cumt_ref[...]
    b_row = sum(_dot(piece, cum) for piece in _split3_bf16(_log_sigmoid(g_row)))
    b_col = sum(_dot(cumt, piece) for piece in _split3_bf16(_log_sigmoid(g_col)))
    u_col = b_col[:, MLSTM_HEADS:] - g_col[:, :MLSTM_HEADS]

    lane = lax.broadcasted_iota(jnp.int32, (HEAD_DIM, HEAD_PAIR), 1)
    s_idx = lax.broadcasted_iota(jnp.int32, (lc, lc), 0)
    t_idx = lax.broadcasted_iota(jnp.int32, (lc, lc), 1)
    causal = s_idx <= t_idx
    nchunks = tt // lc
    heads = range(MLSTM_HEADS)
    in_head = [(lane < HEAD_DIM), (lane >= HEAD_DIM)]

    def operands(c, h):
        p, hh = divmod(h, 2)
        sl = slice(c * lc, (c + 1) * lc)
        qm = qm_ref[hh, sl, p * HEAD_PAIR:(p + 1) * HEAD_PAIR]
        kp = k_ref[sl, p * HEAD_PAIR:(p + 1) * HEAD_PAIR]
        return sl, hh, qm, kp

    m_state = [m_ref[h][0:1, 0:lc] for h in heads]
    for c in range(nchunks):
        for h in heads:
            u = c * MLSTM_HEADS + h
            sl, hh, qm, kp = operands(c, h)
            vt = xvt_ref[h * HEAD_DIM:(h + 1) * HEAD_DIM, sl]
            br = b_row[MLSTM_HEADS + h:MLSTM_HEADS + h + 1, sl]
            ir = g_row[h:h + 1, sl]
            uc = u_col[sl, h:h + 1]
            a = br[:, lc - 1:lc]
            dt = jnp.where(causal, br - uc, NEG_INF)
            inter = br + m_state[h]
            m_t = jnp.maximum(inter, jnp.max(dt, axis=0, keepdims=True))
            w_inter = jnp.exp(inter - m_t)
            qkt = _dot_nt(kp, qm) * jnp.exp(dt - m_t)
            pv_ref[u] = _dot(vt, qkt.astype(BF16))
            row_ref[u, 0:1, 0:lc] = w_inter
            row_ref[u, 1:2, 0:lc] = jnp.sum(qkt, axis=0, keepdims=True)
            row_ref[u, 2:3, 0:lc] = jnp.exp(-m_t)
            g_end = a - br + ir
            m_new = jnp.maximum(a + m_state[h], jnp.max(g_end, axis=1, keepdims=True))
            w_s = jnp.exp(g_end - m_new)
            decay = jnp.exp(a + m_state[h] - m_new)
            row_ref[u, 3:4, 0:lc] = decay
            vtw = (vt.astype(F32) * w_s).astype(BF16)
            kv_ref[u] = jnp.where(in_head[hh], _dot(vtw, kp), 0.0)
            w_rows = jnp.broadcast_to(w_s, (8, lc)).astype(BF16)
            nk_ref[u] = jnp.where(in_head[hh][0:8], _dot(w_rows, kp), 0.0)
            m_state[h] = m_new

    for h in heads:
        c_state = c_ref[h]
        n_state = n_ref[h]
        for c in range(nchunks):
            u = c * MLSTM_HEADS + h
            cprev_ref[u] = c_state.astype(BF16)
            nprev_ref[u] = n_state
            decay = row_ref[u, 3:4, 0:HEAD_PAIR]
            c_state = decay * c_state + kv_ref[u]
            n_state = decay * n_state + nk_ref[u]
        c_ref[h] = c_state
        n_ref[h] = n_state
        m_ref[h] = jnp.broadcast_to(m_state[h][:, 0:1], m_ref.shape[1:])

    for c in range(nchunks):
        for h in heads:
            u = c * MLSTM_HEADS + h
            sl, hh, qm, _ = operands(c, h)
            w_inter = row_ref[u, 0:1, 0:lc]
            num = w_inter * _dot_nt(cprev_ref[u], qm) + pv_ref[u]
            den = w_inter * _dot_nt(nprev_ref[u].astype(BF16), qm)[0:1, :] + row_ref[u, 1:2, 0:lc]
            ht_ref[h * HEAD_DIM:(h + 1) * HEAD_DIM, sl] = num / jnp.maximum(jnp.abs(den), row_ref[u, 2:3, 0:lc])

    gain = jnp.concatenate([gain_ref[...]] * (tt // LANES_V7X), axis=1)
    outs = []
    for h in heads:
        rows = slice(h * HEAD_DIM, (h + 1) * HEAD_DIM)
        hg = ht_ref[rows, :] * jax.nn.sigmoid(xot_ref[rows, :].astype(F32))
        msq = jnp.mean(hg * hg, axis=0, keepdims=True)
        outs.append(hg * lax.rsqrt(msq + NORM_EPS) * gain[rows])
    o_ref[...] = jnp.concatenate(outs, axis=0).T.astype(BF16)


def _mlstm(xqk, xvt, xot, gt, gc, convw, convb, gain, cum, batch, seq):
    t = xqk.shape[0]
    tt = MLSTM_ROWS
    steps = seq // tt
    units = MLSTM_HEADS * (tt // MLSTM_CHUNK)
    col = lambda r: pl.BlockSpec((r, tt), lambda b, i: (0, b * steps + i))
    return pl.pallas_call(
        _mlstm_kernel,
        grid=(batch, steps),
        in_specs=[
            pl.BlockSpec((tt, 2 * MLSTM_WIDTH), lambda b, i: (b * steps + i, 0)),
            pl.BlockSpec((CONV_HALO, 2 * MLSTM_WIDTH),
                         lambda b, i: (jnp.maximum((b * seq + i * tt) // CONV_HALO - 1, 0), 0)),
            col(MLSTM_WIDTH),
            col(MLSTM_WIDTH),
            col(NUM_GATES),
            pl.BlockSpec((tt, NUM_GATES), lambda b, i: (b * steps + i, 0)),
            _const_spec(convw.shape),
            _const_spec(convb.shape),
            _const_spec(gain.shape),
            _const_spec(cum.shape),
            _const_spec(cum.shape),
        ],
        out_specs=pl.BlockSpec((tt, MLSTM_WIDTH), lambda b, i: (b * steps + i, 0)),
        out_shape=jax.ShapeDtypeStruct((t, MLSTM_WIDTH), BF16),
        scratch_shapes=[
            pltpu.VMEM((CONV_HALO + tt, 2 * MLSTM_WIDTH), F32),
            pltpu.VMEM((MLSTM_HEADS, HEAD_DIM, HEAD_PAIR), F32),
            pltpu.VMEM((MLSTM_HEADS, 8, HEAD_PAIR), F32),
            pltpu.VMEM((MLSTM_HEADS, 8, MLSTM_CHUNK), F32),
            pltpu.VMEM((2, tt, MLSTM_WIDTH), BF16),
            pltpu.VMEM((tt, MLSTM_WIDTH), BF16),
            pltpu.VMEM((units, HEAD_DIM, MLSTM_CHUNK), F32),
            pltpu.VMEM((units, 8, MLSTM_CHUNK), F32),
            pltpu.VMEM((units, HEAD_DIM, HEAD_PAIR), F32),
            pltpu.VMEM((units, 8, HEAD_PAIR), F32),
            pltpu.VMEM((units, HEAD_DIM, HEAD_PAIR), BF16),
            pltpu.VMEM((units, 8, HEAD_PAIR), F32),
            pltpu.VMEM((MLSTM_WIDTH, tt), F32),
        ],
        compiler_params=pltpu.CompilerParams(dimension_semantics=("arbitrary", "arbitrary")),
        name="mlstm",
    )(xqk, xqk, xvt, xot, gt, gc, convw, convb, gain, cum, cum.T)


def _out_mlp_kernel(x_ref, ym_ref, yl_ref, ys_ref, wo_ref, ln2_ref, wup_ref, wdn_ref, o_ref):
    y = jnp.concatenate([ym_ref[...], yl_ref[...], ys_ref[...]], axis=1)
    x1 = x_ref[...] + _dot(y, wo_ref[...])
    ms = jnp.mean(x1 * x1, axis=-1, keepdims=True)
    hn = (x1 * lax.rsqrt(ms + NORM_EPS) * ln2_ref[...]).astype(BF16)
    o_ref[...] = x1
    for c in range(0, D_FF, MLP_FF_CHUNK):
        u = _dot(hn, wup_ref[:, c:c + MLP_FF_CHUNK])
        act = jnp.square(jnp.maximum(u, 0.0)).astype(BF16)
        o_ref[...] += _dot(act, wdn_ref[c:c + MLP_FF_CHUNK, :])


def _out_mlp(xf, ym, yl, ys, wo, ln2, wup, wdn):
    t = xf.shape[0]
    tm = OUT_MLP_ROWS
    row = lambda w: pl.BlockSpec((tm, w), lambda i: (i, 0))
    weights = 2 * (wo.size + wup.size + wdn.size)
    tiles = tm * (4 * 4 * D_MODEL + 2 * 2 * D_MODEL + 4 * 4 * D_MODEL + 6 * MLP_FF_CHUNK)
    return pl.pallas_call(
        _out_mlp_kernel,
        grid=(t // tm,),
        in_specs=[
            row(D_MODEL), row(MOBA_WIDTH), row(MLSTM_WIDTH), row(SWA_Q_WIDTH),
            _const_spec(wo.shape), _const_spec(ln2.shape), _const_spec(wup.shape), _const_spec(wdn.shape),
        ],
        out_specs=row(D_MODEL),
        out_shape=jax.ShapeDtypeStruct((t, D_MODEL), F32),
        compiler_params=pltpu.CompilerParams(
            dimension_semantics=("arbitrary",),
            vmem_limit_bytes=_vmem_limit(weights + tiles + (8 << 20))),
        name="out_mlp",
    )(xf, ym, yl, ys, wo, ln2, wup, wdn)


def _rope_tables(seq):
    inv = ROPE_THETA ** (-jnp.arange(0, HEAD_DIM, 2, dtype=F32) / HEAD_DIM)
    ang = jnp.arange(seq, dtype=F32)[:, None] * inv[None, :]
    cos, sin = jnp.cos(ang), jnp.sin(ang)
    cosn = jnp.concatenate([cos, cos, cos, cos], axis=1)
    sinn = jnp.concatenate([-sin, sin, -sin, sin], axis=1)
    return cosn, sinn, cos.T, sin.T


def _layer(xf, tables, consts, batch, seq, ln1, w_in, conv_w, conv_b, igate_b, fgate_b, mlstm_norm,
           moba_q_norm, moba_k_norm, swa_q_norm, swa_k_norm, swa_sinks, w_out, ln2, w_up, w_down):
    cosn, sinn, cost, sint = tables
    bd, cum = consts
    o = 0
    cols = {}
    for name, width in (("mq", MOBA_WIDTH), ("mk", MOBA_WIDTH), ("mv", MOBA_WIDTH), ("sq", SWA_Q_WIDTH),
                        ("sk", SWA_KV_WIDTH), ("sv", SWA_KV_WIDTH), ("xqk", 2 * MLSTM_WIDTH),
                        ("xv", MLSTM_WIDTH), ("xo", MLSTM_WIDTH), ("xi", MLSTM_HEADS), ("xf", MLSTM_HEADS)):
        cols[name] = w_in[:, o:o + width]
        o += width
    wnat = jnp.concatenate([cols["mk"], cols["sk"], cols["xqk"]], axis=1).astype(BF16)
    wtr = jnp.concatenate([cols["mq"], cols["sq"], cols["mv"], cols["sv"], cols["xv"], cols["xo"],
                           cols["xi"], cols["xf"]], axis=1).T.astype(BF16)
    gk = jnp.concatenate([jnp.tile(moba_k_norm, MOBA_HEADS), jnp.tile(swa_k_norm, SWA_KV_HEADS)])[None, :]
    gq = jnp.broadcast_to(
        (jnp.concatenate([moba_q_norm, swa_q_norm]) * (SM_SCALE * LOG2E))[:, None], (2 * HEAD_DIM, LANES_V7X))
    gbias = jnp.broadcast_to(jnp.concatenate([igate_b, fgate_b])[:, None], (NUM_GATES, LANES_V7X))

    (mk, kmean, sk, xqk, mqt, sqt, mvt, svt, xvt, xot, gt) = _in_proj(
        xf, ln1[None, :], wnat, wtr, bd, gk, cosn, sinn, gq, cost, sint, gbias, seq)

    kmean = kmean.reshape(-1, MOBA_WIDTH)
    def bounded(q_gain, k_gain, *extra):
        bound = (HEAD_DIM * SM_SCALE * LOG2E) * jnp.max(jnp.abs(q_gain)) * jnp.max(jnp.abs(k_gain))
        for e in extra:
            bound = jnp.maximum(bound, jnp.max(jnp.abs(e)) * LOG2E)
        return (bound <= MOBA_SAFE_LOG2).astype(jnp.int32).reshape(1)

    ym = _moba(bounded(moba_q_norm, moba_k_norm), mqt, mk, mvt, kmean, batch, seq)
    ys = _swa(bounded(swa_q_norm, swa_k_norm, swa_sinks), swa_sinks, sqt, sk, svt, batch, seq)
    gain = jnp.broadcast_to(mlstm_norm.reshape(MLSTM_WIDTH, 1), (MLSTM_WIDTH, LANES_V7X))
    yl = _mlstm(xqk, xvt, xot, gt, gt.T, conv_w, conv_b[None, :], gain, cum, batch, seq)

    return _out_mlp(xf, ym, yl, ys, w_out.astype(BF16), ln2[None, :], w_up.astype(BF16), w_down.astype(BF16))


def kernel(x, ln1, w_in, conv_w, conv_b, igate_b, fgate_b, mlstm_norm, moba_q_norm, moba_k_norm,
           swa_q_norm, swa_k_norm, swa_sinks, w_out, ln2, w_up, w_down):
    batch, seq, d = x.shape
    assert d == D_MODEL and seq % max(IN_PROJ_ROWS, SWA_ROWS, MLSTM_ROWS, MOBA_BLOCK) == 0
    depth = ln1.shape[0]
    tables = _rope_tables(seq)
    bw = 2 * LANES_V7X
    ids = jnp.arange(bw) // HEAD_DIM
    bd = jnp.where(ids[:, None] == ids[None, :], 1.0 / HEAD_DIM, 0.0).astype(BF16)
    tids = jnp.arange(MLSTM_ROWS)
    cum = ((tids[:, None] // MLSTM_CHUNK == tids[None, :] // MLSTM_CHUNK)
           & (tids[:, None] <= tids[None, :])).astype(BF16)
    xf = x.reshape(batch * seq, d)
    for l in range(depth):
        xf = _layer(xf, tables, (bd, cum), batch, seq, ln1[l], w_in[l], conv_w[l], conv_b[l], igate_b[l],
                    fgate_b[l], mlstm_norm[l], moba_q_norm[l], moba_k_norm[l], swa_q_norm[l],
                    swa_k_norm[l], swa_sinks[l], w_out[l], ln2[l], w_up[l], w_down[l])
    return xf.reshape(batch, seq, d)
```

```python
import functools

import jax
import jax.numpy as jnp
from jax import lax
from jax.experimental import pallas as pl
from jax.experimental.pallas import tpu as pltpu

F32 = jnp.float32
BF16 = jnp.bfloat16
NEG_INF = float("-inf")

D_MODEL = 1024
HEAD_DIM = 64
MOBA_HEADS = 6
MLSTM_HEADS = 4
SWA_Q_HEADS = 6
SWA_KV_HEADS = 2
SWA_GROUP = SWA_Q_HEADS // SWA_KV_HEADS
MOBA_WIDTH = MOBA_HEADS * HEAD_DIM
MLSTM_WIDTH = MLSTM_HEADS * HEAD_DIM
SWA_Q_WIDTH = SWA_Q_HEADS * HEAD_DIM
SWA_KV_WIDTH = SWA_KV_HEADS * HEAD_DIM
MOBA_BLOCK = 256
MOBA_TOPK = 3
MLSTM_CHUNK = 256
CONV_WIDTH = 4
SWA_WINDOW = 128
ROPE_THETA = 10000.0
D_FF = 4 * D_MODEL
NORM_EPS = 1e-6
SM_SCALE = HEAD_DIM ** -0.5
LOG2E = 1.4426950408889634
MOBA_V_ROWS = HEAD_DIM + 16
MOBA_SAFE_LOG2 = 60.0

LANES_V7X = 128
VMEM_BYTES_V7X = 64 * 1024 * 1024
HEAD_PAIR = 2 * HEAD_DIM
assert HEAD_PAIR == LANES_V7X

NAT_WIDTH = MOBA_WIDTH + SWA_KV_WIDTH + 2 * MLSTM_WIDTH
KN_WIDTH = MOBA_WIDTH + SWA_KV_WIDTH
QT_ROWS = MOBA_WIDTH + SWA_Q_WIDTH
TR_ROWS = QT_ROWS + MOBA_WIDTH + SWA_KV_WIDTH + 2 * MLSTM_WIDTH + 2 * MLSTM_HEADS
NUM_GATES = 2 * MLSTM_HEADS

IN_PROJ_ROWS = 1024
OUT_MLP_ROWS = 1024
MLP_FF_CHUNK = 1024
SWA_ROWS = 512
MLSTM_ROWS = 256
MOBA_GROUP = 2
MOBA_Q_BLOCKS = 1
CONV_HALO = 16


def _dot(a, b):
    return jnp.dot(a, b, preferred_element_type=F32)


def _dot_nt(a, b):
    return lax.dot_general(a, b, (((1,), (1,)), ((), ())), preferred_element_type=F32)


def _split_bf16(v):
    hi = v.astype(BF16)
    lo = (v - hi.astype(F32)).astype(BF16)
    return hi, lo


def _split3_bf16(v):
    hi = v.astype(BF16)
    rest = v - hi.astype(F32)
    mid = rest.astype(BF16)
    return hi, mid, (rest - mid.astype(F32)).astype(BF16)


def _vmem_limit(nbytes):
    return int(min(nbytes, VMEM_BYTES_V7X - 4 * 1024 * 1024))


def _const_spec(shape):
    nd = len(shape)
    return pl.BlockSpec(shape, lambda *_: (0,) * nd, pipeline_mode=pl.Buffered(1))


def _in_proj_kernel(x_ref, ln1_ref, wnat_ref, wtr_ref, bd_ref, gk_ref, cosn_ref, sinn_ref,
                    gq_ref, cost_ref, sint_ref, gbias_ref,
                    mk_ref, kmean_ref, sk_ref, xqk_ref, mqt_ref, sqt_ref, mvt_ref, svt_ref,
                    xvt_ref, xot_ref, gt_ref):
    tm = x_ref.shape[0]
    x = x_ref[...]
    ms = jnp.mean(x * x, axis=-1, keepdims=True)
    hn = (x * lax.rsqrt(ms + NORM_EPS) * ln1_ref[...]).astype(BF16)
    nat = _dot(hn, wnat_ref[...])

    kk = nat[:, :KN_WIDTH]
    hi, lo = _split_bf16(kk * kk)
    bd = bd_ref[...]
    bw = bd.shape[0]
    msk = jnp.concatenate(
        [_dot(hi[:, c:c + bw], bd) + _dot(lo[:, c:c + bw], bd) for c in range(0, KN_WIDTH, bw)],
        axis=1)
    kn = kk * lax.rsqrt(msk + NORM_EPS) * gk_ref[...]
    reps = KN_WIDTH // LANES_V7X
    cosn = jnp.concatenate([cosn_ref[...]] * reps, axis=1)
    sinn = jnp.concatenate([sinn_ref[...]] * reps, axis=1)
    lane = lax.broadcasted_iota(jnp.int32, kn.shape, 1)
    first_half = (lane % HEAD_DIM) < (HEAD_DIM // 2)
    swapped = jnp.where(first_half,
                        pltpu.roll(kn, KN_WIDTH - HEAD_DIM // 2, 1),
                        pltpu.roll(kn, HEAD_DIM // 2, 1))
    kr = kn * cosn + swapped * sinn
    mk = kr[:, :MOBA_WIDTH]
    for g in range(MOBA_WIDTH // HEAD_PAIR):
        mk_ref[g] = mk[:, g * HEAD_PAIR:(g + 1) * HEAD_PAIR].astype(BF16)
    nblk = tm // MOBA_BLOCK
    kmean_ref[0] = jnp.concatenate(
        [jnp.mean(mk[c * MOBA_BLOCK:(c + 1) * MOBA_BLOCK], axis=0, keepdims=True) for c in range(nblk)],
        axis=0)
    sk_ref[...] = kr[:, MOBA_WIDTH:].astype(BF16)
    xqk_ref[...] = nat[:, KN_WIDTH:].astype(BF16)

    tr = _dot_nt(wtr_ref[...], hn)
    cost = cost_ref[...]
    sint = sint_ref[...]
    gq = jnp.concatenate([gq_ref[...]] * (tm // LANES_V7X), axis=1)
    half = HEAD_DIM // 2
    for h in range(QT_ROWS // HEAD_DIM):
        blk = tr[h * HEAD_DIM:(h + 1) * HEAD_DIM]
        is_swa = h >= MOBA_HEADS
        gain = gq[HEAD_DIM:] if is_swa else gq[:HEAD_DIM]
        msq = jnp.mean(blk * blk, axis=0, keepdims=True)
        qn = blk * lax.rsqrt(msq + NORM_EPS) * gain
        x1, x2 = qn[:half], qn[half:]
        rot = jnp.concatenate([x1 * cost - x2 * sint, x2 * cost + x1 * sint], axis=0).astype(BF16)
        if is_swa:
            r0 = (h - MOBA_HEADS) * HEAD_DIM
            sqt_ref[r0:r0 + HEAD_DIM, :] = rot
        else:
            mqt_ref[h * HEAD_DIM:(h + 1) * HEAD_DIM, :] = rot
    r = QT_ROWS
    mv = tr[r:r + MOBA_WIDTH].astype(BF16)
    ones = jnp.ones((MOBA_V_ROWS - HEAD_DIM, MOBA_BLOCK), BF16)
    for c in range(nblk):
        for h in range(MOBA_HEADS):
            r0 = h * MOBA_V_ROWS
            mvt_ref[c, r0:r0 + HEAD_DIM, :] = mv[h * HEAD_DIM:(h + 1) * HEAD_DIM,
                                                 c * MOBA_BLOCK:(c + 1) * MOBA_BLOCK]
            mvt_ref[c, r0 + HEAD_DIM:r0 + MOBA_V_ROWS, :] = ones
    r += MOBA_WIDTH
    svt_ref[...] = tr[r:r + SWA_KV_WIDTH].astype(BF16)
    r += SWA_KV_WIDTH
    xvt_ref[...] = tr[r:r + MLSTM_WIDTH].astype(BF16)
    r += MLSTM_WIDTH
    xot_ref[...] = tr[r:r + MLSTM_WIDTH].astype(BF16)
    r += MLSTM_WIDTH
    gbias = jnp.concatenate([gbias_ref[...]] * (tm // LANES_V7X), axis=1)
    gt_ref[...] = tr[r:r + NUM_GATES] + gbias


def _in_proj(xf, ln1, wnat, wtr, bd, gk, cosn, sinn, gq, cost, sint, gbias, seq):
    t = xf.shape[0]
    tm = IN_PROJ_ROWS
    steps = t // tm
    seq_steps = seq // tm
    nblk = tm // MOBA_BLOCK
    row = lambda w: pl.BlockSpec((tm, w), lambda i: (i, 0))
    col = lambda r: pl.BlockSpec((r, tm), lambda i: (0, i))
    in_specs = [
        row(D_MODEL),
        _const_spec((1, D_MODEL)),
        _const_spec(wnat.shape),
        _const_spec(wtr.shape),
        _const_spec(bd.shape),
        _const_spec(gk.shape),
        pl.BlockSpec((tm, LANES_V7X), lambda i: (i % seq_steps, 0)),
        pl.BlockSpec((tm, LANES_V7X), lambda i: (i % seq_steps, 0)),
        _const_spec(gq.shape),
        pl.BlockSpec((HEAD_DIM // 2, tm), lambda i: (0, i % seq_steps)),
        pl.BlockSpec((HEAD_DIM // 2, tm), lambda i: (0, i % seq_steps)),
        _const_spec(gbias.shape),
    ]
    out_shape = [
        jax.ShapeDtypeStruct((MOBA_WIDTH // HEAD_PAIR, t, HEAD_PAIR), BF16),
        jax.ShapeDtypeStruct((steps, nblk, MOBA_WIDTH), F32),
        jax.ShapeDtypeStruct((t, SWA_KV_WIDTH), BF16),
        jax.ShapeDtypeStruct((t, 2 * MLSTM_WIDTH), BF16),
        jax.ShapeDtypeStruct((MOBA_WIDTH, t), BF16),
        jax.ShapeDtypeStruct((SWA_Q_WIDTH, t), BF16),
        jax.ShapeDtypeStruct((t // MOBA_BLOCK, MOBA_HEADS * MOBA_V_ROWS, MOBA_BLOCK), BF16),
        jax.ShapeDtypeStruct((SWA_KV_WIDTH, t), BF16),
        jax.ShapeDtypeStruct((MLSTM_WIDTH, t), BF16),
        jax.ShapeDtypeStruct((MLSTM_WIDTH, t), BF16),
        jax.ShapeDtypeStruct((NUM_GATES, t), F32),
    ]
    out_specs = [
        pl.BlockSpec((MOBA_WIDTH // HEAD_PAIR, tm, HEAD_PAIR), lambda i: (0, i, 0)),
        pl.BlockSpec((1, nblk, MOBA_WIDTH), lambda i: (i, 0, 0)),
        row(SWA_KV_WIDTH),
        row(2 * MLSTM_WIDTH),
        col(MOBA_WIDTH),
        col(SWA_Q_WIDTH),
        pl.BlockSpec((nblk, MOBA_HEADS * MOBA_V_ROWS, MOBA_BLOCK), lambda i: (i, 0, 0)),
        col(SWA_KV_WIDTH),
        col(MLSTM_WIDTH),
        col(MLSTM_WIDTH),
        col(NUM_GATES),
    ]
    weights = 2 * (wnat.size + wtr.size)
    tiles = tm * (2 * 4 * D_MODEL + 3 * 4 * (NAT_WIDTH + TR_ROWS) + 2 * 2 * (NAT_WIDTH + TR_ROWS))
    return pl.pallas_call(
        _in_proj_kernel,
        grid=(steps,),
        in_specs=in_specs,
        out_specs=out_specs,
        out_shape=out_shape,
        compiler_params=pltpu.CompilerParams(
            dimension_semantics=("arbitrary",),
            vmem_limit_bytes=_vmem_limit(2 * weights + tiles + (8 << 20))),
        name="in_proj",
    )(xf, ln1, wnat, wtr, bd, gk, cosn, sinn, gq, cost, sint, gbias)


def _moba_kernel(bounded_ref, qt_ref, k_ref, vt_ref, km_ref, o_ref, sel_ref, s_ref, p_ref):
    nb = k_ref.shape[1]
    lq = MOBA_BLOCK
    nh = MOBA_Q_BLOCKS * MOBA_HEADS
    step = pl.program_id(1)
    row = lax.broadcasted_iota(jnp.int32, (HEAD_PAIR, lq), 0)
    own, qh = [], []
    for s in range(nh):
        qb, h = divmod(s, MOBA_HEADS)
        own.append(step * MOBA_Q_BLOCKS + qb)
        qt = qt_ref[(h // 2) * HEAD_PAIR:(h // 2 + 1) * HEAD_PAIR, qb * lq:(qb + 1) * lq]
        keep = (row < HEAD_DIM) if h % 2 == 0 else (row >= HEAD_DIM)
        qh.append(jnp.where(keep, qt, jnp.zeros_like(qt)))
    last_own = step * MOBA_Q_BLOCKS + (MOBA_Q_BLOCKS - 1)

    def k_group(blk0, count, s):
        g = (s % MOBA_HEADS) // 2
        return k_ref[g, pl.ds(blk0, count)].reshape(count * lq, HEAD_PAIR)

    def v_aug(j, s):
        h = s % MOBA_HEADS
        return vt_ref[j, h * MOBA_V_ROWS:(h + 1) * MOBA_V_ROWS, :]

    def select_blocks():
        gates = []
        for s in range(nh):
            g = (s % MOBA_HEADS) // 2
            km_hi, km_lo = _split_bf16(km_ref[:, g * HEAD_PAIR:(g + 1) * HEAD_PAIR])
            gates.append(_dot(km_hi, qh[s]) + _dot(km_lo, qh[s]))
        gate = jnp.concatenate(gates, axis=1)
        own_all = jnp.concatenate([jnp.full((1, lq), own[s], jnp.int32) for s in range(nh)], axis=1)
        blk_all = lax.broadcasted_iota(jnp.int32, gate.shape, 0)
        gate = jnp.where(blk_all < own_all, gate, NEG_INF)
        sel = jnp.full(gate.shape, NEG_INF, F32)
        for _ in range(MOBA_TOPK):
            mx = jnp.max(gate, axis=0, keepdims=True)
            idx = jnp.min(jnp.where(gate == mx, blk_all, nb), axis=0, keepdims=True)
            pick = blk_all == jnp.where(idx < own_all, idx, nb)
            sel = jnp.where(pick, 0.0, sel)
            gate = jnp.where(pick, NEG_INF, gate)
        for s in range(nh):
            sel_ref[s] = sel[:, s * lq:(s + 1) * lq]

    kpos = lax.broadcasted_iota(jnp.int32, (lq, lq), 0)
    qpos = lax.broadcasted_iota(jnp.int32, (lq, lq), 1)
    causal = kpos <= qpos

    def weighted_values(st, mu, v_rows):
        pv = _dot(v_rows, jnp.exp2(st - mu).astype(BF16))
        return pv[HEAD_DIM:HEAD_DIM + 1], pv[:HEAD_DIM]

    def finish(nums, dens):
        for s in range(0, nh, 2):
            qb, h = divmod(s, MOBA_HEADS)
            ot = jnp.concatenate([nums[s] / dens[s], nums[s + 1] / dens[s + 1]], axis=0)
            o_ref[qb * lq:(qb + 1) * lq, (h // 2) * HEAD_PAIR:(h // 2 + 1) * HEAD_PAIR] = ot.T.astype(BF16)

    trips = lax.shift_right_logical(last_own + (2 * MOBA_GROUP - 1), MOBA_GROUP.bit_length())
    units = nh * MOBA_GROUP
    own_slot = 2 * units

    @pl.when(bounded_ref[0] != 0)
    def _():
        select_blocks()
        for h in range(nh):
            st = jnp.where(causal, _dot(k_group(own[h], 1, h), qh[h]), NEG_INF)
            p_ref[own_slot + h] = jnp.exp2(st).astype(BF16)

        def stage_p(blk0, slot0):
            blk0 = jnp.minimum(blk0, nb - MOBA_GROUP)
            for h in range(nh):
                st = _dot(k_group(blk0, MOBA_GROUP, h), qh[h])
                sel = jnp.concatenate(
                    [jnp.broadcast_to(sel_ref[h, pl.ds(blk0 + u, 1), :], (lq, lq)) for u in range(MOBA_GROUP)],
                    axis=0)
                slot = slot0 + MOBA_GROUP * h
                p_ref[slot:slot + MOBA_GROUP] = jnp.exp2(st + sel).astype(BF16).reshape(MOBA_GROUP, lq, lq)

        def stage_v(blk0, slot0, acc):
            acc = list(acc)
            for h in range(nh):
                slot = slot0 + MOBA_GROUP * h
                p = p_ref[slot:slot + MOBA_GROUP].reshape(MOBA_GROUP * lq, lq)
                v = jnp.concatenate([v_aug(blk0 + u, h) for u in range(MOBA_GROUP)], axis=1)
                acc[h] = acc[h] + _dot(v, p)
            return acc

        def trip(t, acc, last):
            blk = t * (2 * MOBA_GROUP)
            stage_p(blk + MOBA_GROUP, units)
            acc = stage_v(blk, 0, acc)
            if not last:
                stage_p(blk + 2 * MOBA_GROUP, 0)
            return tuple(stage_v(blk + MOBA_GROUP, units, acc))

        stage_p(0, 0)
        acc = tuple(_dot(v_aug(own[h], h), p_ref[own_slot + h]) for h in range(nh))
        acc = lax.fori_loop(0, jnp.maximum(trips - 1, 0), lambda t, a: trip(t, a, False), acc)
        acc = lax.cond(trips > 0, lambda a: trip(trips - 1, a, True), lambda a: a, acc)
        finish([a[:HEAD_DIM] for a in acc], [a[HEAD_DIM:HEAD_DIM + 1] for a in acc])


    def stage_a(blk0, slot0):
        mus = []
        for u in range(MOBA_GROUP):
            for h in range(nh):
                st = _dot(k_group(jnp.minimum(blk0 + u, nb - 1), 1, h), qh[h])
                s_ref[slot0 + nh * u + h] = st
                mus.append(jnp.max(st, axis=0, keepdims=True))
        return mus

    def stage_b(blk0, slot0, mus, state):
        state = list(state)
        for u in range(MOBA_GROUP):
            j = blk0 + u
            for h in range(nh):
                mu = mus[nh * u + h]
                ls, pv = weighted_values(s_ref[slot0 + nh * u + h], mu, v_aug(j, h))
                m, l, acc = state[3 * h:3 * h + 3]
                mu_sel = mu + sel_ref[h, pl.ds(j, 1), :]
                m_new = jnp.maximum(m, mu_sel)
                f = jnp.exp2(mu_sel - m_new)
                c = jnp.exp2(m - m_new)
                state[3 * h:3 * h + 3] = [m_new, c * l + f * ls, c * acc + f * pv]
        return state

    def body(t, loop_carry):
        mus0, state = loop_carry[:units], loop_carry[units:]
        blk = t * (2 * MOBA_GROUP)
        mus1 = stage_a(blk + MOBA_GROUP, units)
        state = stage_b(blk, 0, mus0, state)
        mus0 = stage_a(blk + 2 * MOBA_GROUP, 0)
        state = stage_b(blk + MOBA_GROUP, units, mus1, state)
        return (*mus0, *state)

    @pl.when(bounded_ref[0] == 0)
    def _():
        state = []
        for h in range(nh):
            st = jnp.where(causal, _dot(k_group(own[h], 1, h), qh[h]), NEG_INF)
            mu = jnp.max(st, axis=0, keepdims=True)
            state += [mu, *weighted_values(st, mu, v_aug(own[h], h))]
        mus0 = stage_a(0, 0)
        select_blocks()
        state = lax.fori_loop(0, trips, body, (*mus0, *state))[units:]
        finish(state[2::3], state[1::3])


def _moba(bounded, mqt, mk, mvt, kmean, batch, seq):
    t = mk.shape[1]
    nb = seq // MOBA_BLOCK
    lq = MOBA_BLOCK
    pairs = MOBA_WIDTH // HEAD_PAIR
    k4 = mk.reshape(pairs, t // MOBA_BLOCK, MOBA_BLOCK, HEAD_PAIR)
    assert nb % (2 * MOBA_GROUP) == 0 and nb % MOBA_Q_BLOCKS == 0
    streams = MOBA_Q_BLOCKS * MOBA_HEADS
    units = streams * MOBA_GROUP
    steps = nb // MOBA_Q_BLOCKS
    kv_bytes = 2 * nb * MOBA_BLOCK * (MOBA_WIDTH + MOBA_HEADS * MOBA_V_ROWS)
    scratch_bytes = lq * lq * (4 * 2 * units + 2 * (2 * units + streams)) + 4 * streams * nb * lq
    resident = lambda shape, imap: pl.BlockSpec(shape, imap, pipeline_mode=pl.Buffered(1))
    return pl.pallas_call(
        _moba_kernel,
        grid=(batch, steps),
        in_specs=[
            pl.BlockSpec(memory_space=pltpu.SMEM),
            pl.BlockSpec((MOBA_WIDTH, MOBA_Q_BLOCKS * lq), lambda b, i: (0, b * steps + i)),
            resident((pairs, nb, MOBA_BLOCK, HEAD_PAIR), lambda b, i: (0, b, 0, 0)),
            resident((nb, MOBA_HEADS * MOBA_V_ROWS, MOBA_BLOCK), lambda b, i: (b, 0, 0)),
            resident((nb, MOBA_WIDTH), lambda b, i: (b, 0)),
        ],
        out_specs=pl.BlockSpec((MOBA_Q_BLOCKS * lq, MOBA_WIDTH), lambda b, i: (b * steps + i, 0)),
        out_shape=jax.ShapeDtypeStruct((t, MOBA_WIDTH), BF16),
        scratch_shapes=[pltpu.VMEM((streams, nb, lq), F32),
                        pltpu.VMEM((2 * units, lq, lq), F32),
                        pltpu.VMEM((2 * units + streams, lq, lq), BF16)],
        compiler_params=pltpu.CompilerParams(
            dimension_semantics=("arbitrary", "arbitrary"),
            vmem_limit_bytes=_vmem_limit(kv_bytes + scratch_bytes + (12 << 20))),
        name="moba",
    )(bounded, mqt, k4, mvt, kmean)


def _swa_kernel(bounded_ref, sink_ref, qt_ref, k_ref, kh_ref, vt_ref, vth_ref, o_ref, p_ref):
    tq = k_ref.shape[0]
    w = SWA_WINDOW
    i = pl.program_id(1)
    ncol = SWA_GROUP * w
    kpos = lax.broadcasted_iota(jnp.int32, (2 * w, ncol), 0)
    col = lax.broadcasted_iota(jnp.int32, (2 * w, ncol), 1)
    diff = w + (col % w) - kpos
    in_window = (diff >= 0) & (diff < w)
    zeros = jnp.zeros((HEAD_DIM, ncol), BF16)
    ones = jnp.ones((MOBA_V_ROWS - HEAD_DIM, 2 * w), BF16)

    def attend(shifted):
        sink_terms = []
        for r in range(tq // w):
            if r == 0:
                kcat = jnp.concatenate([kh_ref[...], k_ref[0:w, :]], axis=0)
                mask = in_window & ((kpos >= w) | (i > 0))
            else:
                kcat = k_ref[(r - 1) * w:(r + 1) * w, :]
                mask = in_window
            for g in range(SWA_KV_HEADS):
                qs = jnp.concatenate(
                    [qt_ref[(SWA_GROUP * g + a) * HEAD_DIM:(SWA_GROUP * g + a + 1) * HEAD_DIM, r * w:(r + 1) * w]
                     for a in range(SWA_GROUP)], axis=1)
                qpad = jnp.concatenate([qs, zeros] if g == 0 else [zeros, qs], axis=0)
                st = jnp.where(mask, _dot(kcat, qpad), NEG_INF)
                sink = jnp.concatenate(
                    [jnp.full((1, w), sink_ref[SWA_GROUP * g + a] * LOG2E, F32) for a in range(SWA_GROUP)],
                    axis=1)
                if shifted:
                    m = jnp.maximum(jnp.max(st, axis=0, keepdims=True), sink)
                    st, sink = st - m, sink - m
                p_ref[r * SWA_KV_HEADS + g] = jnp.exp2(st).astype(BF16)
                sink_terms.append(jnp.exp2(sink))
        for r in range(tq // w):
            if r == 0:
                vcat = jnp.concatenate([vth_ref[...], vt_ref[:, 0:w]], axis=1)
            else:
                vcat = vt_ref[:, (r - 1) * w:(r + 1) * w]
            pieces = []
            for g in range(SWA_KV_HEADS):
                v_aug = jnp.concatenate([vcat[g * HEAD_DIM:(g + 1) * HEAD_DIM, :], ones], axis=0)
                pv = _dot(v_aug, p_ref[r * SWA_KV_HEADS + g])
                ot = pv[:HEAD_DIM] / (pv[HEAD_DIM:HEAD_DIM + 1] + sink_terms[r * SWA_KV_HEADS + g])
                pieces += [ot[:, a * w:(a + 1) * w] for a in range(SWA_GROUP)]
            o_ref[r * w:(r + 1) * w, :] = jnp.concatenate(pieces, axis=0).T.astype(BF16)

    @pl.when(bounded_ref[0] != 0)
    def _():
        attend(False)

    @pl.when(bounded_ref[0] == 0)
    def _():
        attend(True)


def _swa(bounded, sinks, sqt, sk, svt, batch, seq):
    t = sk.shape[0]
    tq = SWA_ROWS
    w = SWA_WINDOW
    steps = seq // tq
    halo = lambda b, i: jnp.maximum((b * seq + i * tq) // w - 1, 0)
    return pl.pallas_call(
        _swa_kernel,
        grid=(batch, steps),
        in_specs=[
            pl.BlockSpec(memory_space=pltpu.SMEM),
            pl.BlockSpec(memory_space=pltpu.SMEM),
            pl.BlockSpec((SWA_Q_WIDTH, tq), lambda b, i: (0, b * steps + i)),
            pl.BlockSpec((tq, SWA_KV_WIDTH), lambda b, i: (b * steps + i, 0)),
            pl.BlockSpec((w, SWA_KV_WIDTH), lambda b, i: (halo(b, i), 0)),
            pl.BlockSpec((SWA_KV_WIDTH, tq), lambda b, i: (0, b * steps + i)),
            pl.BlockSpec((SWA_KV_WIDTH, w), lambda b, i: (0, halo(b, i))),
        ],
        out_specs=pl.BlockSpec((tq, SWA_Q_WIDTH), lambda b, i: (b * steps + i, 0)),
        out_shape=jax.ShapeDtypeStruct((t, SWA_Q_WIDTH), BF16),
        scratch_shapes=[pltpu.VMEM(((tq // w) * SWA_KV_HEADS, 2 * w, SWA_GROUP * w), BF16)],
        compiler_params=pltpu.CompilerParams(dimension_semantics=("arbitrary", "arbitrary")),
        name="swa",
    )(bounded, sinks, sqt, sk, sk, svt, svt)


def _log_sigmoid(v):
    return jnp.minimum(v, 0.0) - jnp.log1p(jnp.exp(-jnp.abs(v)))


def _mlstm_kernel(xqk_ref, halo_ref, xvt_ref, xot_ref, gt_ref, gc_ref, convw_ref, convb_ref,
                  gain_ref, cum_ref, cumt_ref, o_ref, ext_ref, c_ref, n_ref, m_ref,
                  qm_ref, k_ref, pv_ref, row_ref, kv_ref, nk_ref, cprev_ref, nprev_ref, ht_ref):
    tt = xqk_ref.shape[0]
    lc = MLSTM_CHUNK
    i = pl.program_id(1)

    @pl.when(i == 0)
    def _():
        c_ref[...] = jnp.zeros_like(c_ref)
        n_ref[...] = jnp.zeros_like(n_ref)
        m_ref[...] = jnp.zeros_like(m_ref)

    hist = halo_ref[...].astype(F32)
    ext_ref[0:CONV_HALO, :] = jnp.where(i > 0, hist, jnp.zeros_like(hist))
    ext_ref[CONV_HALO:CONV_HALO + tt, :] = xqk_ref[...].astype(F32)
    convw = convw_ref[...]
    conv = jnp.zeros((tt, 2 * MLSTM_WIDTH), F32) + convb_ref[...]
    for j in range(CONV_WIDTH):
        start = CONV_HALO - (CONV_WIDTH - 1) + j
        conv = conv + convw[j:j + 1, :] * ext_ref[start:start + tt, :]
    qk = conv * jax.nn.sigmoid(conv)
    q_all = qk[:, :MLSTM_WIDTH]
    lane_q = lax.broadcasted_iota(jnp.int32, q_all.shape, 1)
    even_head = (lane_q % HEAD_PAIR) < HEAD_DIM
    qm_ref[0] = jnp.where(even_head, q_all, 0.0).astype(BF16)
    qm_ref[1] = jnp.where(even_head, 0.0, q_all).astype(BF16)
    k_ref[...] = (qk[:, MLSTM_WIDTH:] * SM_SCALE).astype(BF16)

    g_row = gt_ref[...]
    g_col = gc_ref[...]
    cum = cum_ref[...]
    cumt = cumt_ref[...]
    b_row = sum(_dot(piece, cum) for piece in _split3_bf16(_log_sigmoid(g_row)))
    b_col = sum(_dot(cumt, piece) for piece in _split3_bf16(_log_sigmoid(g_col)))
    u_col = b_col[:, MLSTM_HEADS:] - g_col[:, :MLSTM_HEADS]

    lane = lax.broadcasted_iota(jnp.int32, (HEAD_DIM, HEAD_PAIR), 1)
    s_idx = lax.broadcasted_iota(jnp.int32, (lc, lc), 0)
    t_idx = lax.broadcasted_iota(jnp.int32, (lc, lc), 1)
    causal = s_idx <= t_idx
    nchunks = tt // lc
    heads = range(MLSTM_HEADS)
    in_head = [(lane < HEAD_DIM), (lane >= HEAD_DIM)]

    def operands(c, h):
        p, hh = divmod(h, 2)
        sl = slice(c * lc, (c + 1) * lc)
        qm = qm_ref[hh, sl, p * HEAD_PAIR:(p + 1) * HEAD_PAIR]
        kp = k_ref[sl, p * HEAD_PAIR:(p + 1) * HEAD_PAIR]
        return sl, hh, qm, kp

    m_state = [m_ref[h][0:1, 0:lc] for h in heads]
    for c in range(nchunks):
        for h in heads:
            u = c * MLSTM_HEADS + h
            sl, hh, qm, kp = operands(c, h)
            vt = xvt_ref[h * HEAD_DIM:(h + 1) * HEAD_DIM, sl]
            br = b_row[MLSTM_HEADS + h:MLSTM_HEADS + h + 1, sl]
            ir = g_row[h:h + 1, sl]
            uc = u_col[sl, h:h + 1]
            a = br[:, lc - 1:lc]
            dt = jnp.where(causal, br - uc, NEG_INF)
            inter = br + m_state[h]
            m_t = jnp.maximum(inter, jnp.max(dt, axis=0, keepdims=True))
            w_inter = jnp.exp(inter - m_t)
            qkt = _dot_nt(kp, qm) * jnp.exp(dt - m_t)
            pv_ref[u] = _dot(vt, qkt.astype(BF16))
            row_ref[u, 0:1, 0:lc] = w_inter
            row_ref[u, 1:2, 0:lc] = jnp.sum(qkt, axis=0, keepdims=True)
            row_ref[u, 2:3, 0:lc] = jnp.exp(-m_t)
            g_end = a - br + ir
            m_new = jnp.maximum(a + m_state[h], jnp.max(g_end, axis=1, keepdims=True))
            w_s = jnp.exp(g_end - m_new)
            decay = jnp.exp(a + m_state[h] - m_new)
            row_ref[u, 3:4, 0:lc] = decay
            vtw = (vt.astype(F32) * w_s).astype(BF16)
            kv_ref[u] = jnp.where(in_head[hh], _dot(vtw, kp), 0.0)
            w_rows = jnp.broadcast_to(w_s, (8, lc)).astype(BF16)
            nk_ref[u] = jnp.where(in_head[hh][0:8], _dot(w_rows, kp), 0.0)
            m_state[h] = m_new

    for h in heads:
        c_state = c_ref[h]
        n_state = n_ref[h]
        for c in range(nchunks):
            u = c * MLSTM_HEADS + h
            cprev_ref[u] = c_state.astype(BF16)
            nprev_ref[u] = n_state
            decay = row_ref[u, 3:4, 0:HEAD_PAIR]
            c_state = decay * c_state + kv_ref[u]
            n_state = decay * n_state + nk_ref[u]
        c_ref[h] = c_state
        n_ref[h] = n_state
        m_ref[h] = jnp.broadcast_to(m_state[h][:, 0:1], m_ref.shape[1:])

    for c in range(nchunks):
        for h in heads:
            u = c * MLSTM_HEADS + h
            sl, hh, qm, _ = operands(c, h)
            w_inter = row_ref[u, 0:1, 0:lc]
            num = w_inter * _dot_nt(cprev_ref[u], qm) + pv_ref[u]
            den = w_inter * _dot_nt(nprev_ref[u].astype(BF16), qm)[0:1, :] + row_ref[u, 1:2, 0:lc]
            ht_ref[h * HEAD_DIM:(h + 1) * HEAD_DIM, sl] = num / jnp.maximum(jnp.abs(den), row_ref[u, 2:3, 0:lc])

    gain = jnp.concatenate([gain_ref[...]] * (tt // LANES_V7X), axis=1)
    outs = []
    for h in heads:
        rows = slice(h * HEAD_DIM, (h + 1) * HEAD_DIM)
        hg = ht_ref[rows, :] * jax.nn.sigmoid(xot_ref[rows, :].astype(F32))
        msq = jnp.mean(hg * hg, axis=0, keepdims=True)
        outs.append(hg * lax.rsqrt(msq + NORM_EPS) * gain[rows])
    o_ref[...] = jnp.concatenate(outs, axis=0).T.astype(BF16)


def _mlstm(xqk, xvt, xot, gt, gc, convw, convb, gain, cum, batch, seq):
    t = xqk.shape[0]
    tt = MLSTM_ROWS
    steps = seq // tt
    units = MLSTM_HEADS * (tt // MLSTM_CHUNK)
    col = lambda r: pl.BlockSpec((r, tt), lambda b, i: (0, b * steps + i))
    return pl.pallas_call(
        _mlstm_kernel,
        grid=(batch, steps),
        in_specs=[
            pl.BlockSpec((tt, 2 * MLSTM_WIDTH), lambda b, i: (b * steps + i, 0)),
            pl.BlockSpec((CONV_HALO, 2 * MLSTM_WIDTH),
                         lambda b, i: (jnp.maximum((b * seq + i * tt) // CONV_HALO - 1, 0), 0)),
            col(MLSTM_WIDTH),
            col(MLSTM_WIDTH),
            col(NUM_GATES),
            pl.BlockSpec((tt, NUM_GATES), lambda b, i: (b * steps + i, 0)),
            _const_spec(convw.shape),
            _const_spec(convb.shape),
            _const_spec(gain.shape),
            _const_spec(cum.shape),
            _const_spec(cum.shape),
        ],
        out_specs=pl.BlockSpec((tt, MLSTM_WIDTH), lambda b, i: (b * steps + i, 0)),
        out_shape=jax.ShapeDtypeStruct((t, MLSTM_WIDTH), BF16),
        scratch_shapes=[
            pltpu.VMEM((CONV_HALO + tt, 2 * MLSTM_WIDTH), F32),
            pltpu.VMEM((MLSTM_HEADS, HEAD_DIM, HEAD_PAIR), F32),
            pltpu.VMEM((MLSTM_HEADS, 8, HEAD_PAIR), F32),
            pltpu.VMEM((MLSTM_HEADS, 8, MLSTM_CHUNK), F32),
            pltpu.VMEM((2, tt, MLSTM_WIDTH), BF16),
            pltpu.VMEM((tt, MLSTM_WIDTH), BF16),
            pltpu.VMEM((units, HEAD_DIM, MLSTM_CHUNK), F32),
            pltpu.VMEM((units, 8, MLSTM_CHUNK), F32),
            pltpu.VMEM((units, HEAD_DIM, HEAD_PAIR), F32),
            pltpu.VMEM((units, 8, HEAD_PAIR), F32),
            pltpu.VMEM((units, HEAD_DIM, HEAD_PAIR), BF16),
            pltpu.VMEM((units, 8, HEAD_PAIR), F32),
            pltpu.VMEM((MLSTM_WIDTH, tt), F32),
        ],
        compiler_params=pltpu.CompilerParams(dimension_semantics=("arbitrary", "arbitrary")),
        name="mlstm",
    )(xqk, xqk, xvt, xot, gt, gc, convw, convb, gain, cum, cum.T)


def _out_mlp_kernel(x_ref, ym_ref, yl_ref, ys_ref, wo_ref, ln2_ref, wup_ref, wdn_ref, o_ref):
    y = jnp.concatenate([ym_ref[...], yl_ref[...], ys_ref[...]], axis=1)
    x1 = x_ref[...] + _dot(y, wo_ref[...])
    ms = jnp.mean(x1 * x1, axis=-1, keepdims=True)
    hn = (x1 * lax.rsqrt(ms + NORM_EPS) * ln2_ref[...]).astype(BF16)
    o_ref[...] = x1
    for c in range(0, D_FF, MLP_FF_CHUNK):
        u = _dot(hn, wup_ref[:, c:c + MLP_FF_CHUNK])
        act = jnp.square(jnp.maximum(u, 0.0)).astype(BF16)
        o_ref[...] += _dot(act, wdn_ref[c:c + MLP_FF_CHUNK, :])


def _out_mlp(xf, ym, yl, ys, wo, ln2, wup, wdn):
    t = xf.shape[0]
    tm = OUT_MLP_ROWS
    row = lambda w: pl.BlockSpec((tm, w), lambda i: (i, 0))
    weights = 2 * (wo.size + wup.size + wdn.size)
    tiles = tm * (4 * 4 * D_MODEL + 2 * 2 * D_MODEL + 4 * 4 * D_MODEL + 6 * MLP_FF_CHUNK)
    return pl.pallas_call(
        _out_mlp_kernel,
        grid=(t // tm,),
        in_specs=[
            row(D_MODEL), row(MOBA_WIDTH), row(MLSTM_WIDTH), row(SWA_Q_WIDTH),
            _const_spec(wo.shape), _const_spec(ln2.shape), _const_spec(wup.shape), _const_spec(wdn.shape),
        ],
        out_specs=row(D_MODEL),
        out_shape=jax.ShapeDtypeStruct((t, D_MODEL), F32),
        compiler_params=pltpu.CompilerParams(
            dimension_semantics=("arbitrary",),
            vmem_limit_bytes=_vmem_limit(weights + tiles + (8 << 20))),
        name="out_mlp",
    )(xf, ym, yl, ys, wo, ln2, wup, wdn)


def _rope_tables(seq):
    inv = ROPE_THETA ** (-jnp.arange(0, HEAD_DIM, 2, dtype=F32) / HEAD_DIM)
    ang = jnp.arange(seq, dtype=F32)[:, None] * inv[None, :]
    cos, sin = jnp.cos(ang), jnp.sin(ang)
    cosn = jnp.concatenate([cos, cos, cos, cos], axis=1)
    sinn = jnp.concatenate([-sin, sin, -sin, sin], axis=1)
    return cosn, sinn, cos.T, sin.T


def _layer(xf, tables, consts, batch, seq, ln1, w_in, conv_w, conv_b, igate_b, fgate_b, mlstm_norm,
           moba_q_norm, moba_k_norm, swa_q_norm, swa_k_norm, swa_sinks, w_out, ln2, w_up, w_down):
    cosn, sinn, cost, sint = tables
    bd, cum = consts
    o = 0
    cols = {}
    for name, width in (("mq", MOBA_WIDTH), ("mk", MOBA_WIDTH), ("mv", MOBA_WIDTH), ("sq", SWA_Q_WIDTH),
                        ("sk", SWA_KV_WIDTH), ("sv", SWA_KV_WIDTH), ("xqk", 2 * MLSTM_WIDTH),
                        ("xv", MLSTM_WIDTH), ("xo", MLSTM_WIDTH), ("xi", MLSTM_HEADS), ("xf", MLSTM_HEADS)):
        cols[name] = w_in[:, o:o + width]
        o += width
    wnat = jnp.concatenate([cols["mk"], cols["sk"], cols["xqk"]], axis=1).astype(BF16)
    wtr = jnp.concatenate([cols["mq"], cols["sq"], cols["mv"], cols["sv"], cols["xv"], cols["xo"],
                           cols["xi"], cols["xf"]], axis=1).T.astype(BF16)
    gk = jnp.concatenate([jnp.tile(moba_k_norm, MOBA_HEADS), jnp.tile(swa_k_norm, SWA_KV_HEADS)])[None, :]
    gq = jnp.broadcast_to(
        (jnp.concatenate([moba_q_norm, swa_q_norm]) * (SM_SCALE * LOG2E))[:, None], (2 * HEAD_DIM, LANES_V7X))
    gbias = jnp.broadcast_to(jnp.concatenate([igate_b, fgate_b])[:, None], (NUM_GATES, LANES_V7X))

    (mk, kmean, sk, xqk, mqt, sqt, mvt, svt, xvt, xot, gt) = _in_proj(
        xf, ln1[None, :], wnat, wtr, bd, gk, cosn, sinn, gq, cost, sint, gbias, seq)

    kmean = kmean.reshape(-1, MOBA_WIDTH)
    def bounded(q_gain, k_gain, *extra):
        bound = (HEAD_DIM * SM_SCALE * LOG2E) * jnp.max(jnp.abs(q_gain)) * jnp.max(jnp.abs(k_gain))
        for e in extra:
            bound = jnp.maximum(bound, jnp.max(jnp.abs(e)) * LOG2E)
        return (bound <= MOBA_SAFE_LOG2).astype(jnp.int32).reshape(1)

    ym = _moba(bounded(moba_q_norm, moba_k_norm), mqt, mk, mvt, kmean, batch, seq)
    ys = _swa(bounded(swa_q_norm, swa_k_norm, swa_sinks), swa_sinks, sqt, sk, svt, batch, seq)
    gain = jnp.broadcast_to(mlstm_norm.reshape(MLSTM_WIDTH, 1), (MLSTM_WIDTH, LANES_V7X))
    yl = _mlstm(xqk, xvt, xot, gt, gt.T, conv_w, conv_b[None, :], gain, cum, batch, seq)

    return _out_mlp(xf, ym, yl, ys, w_out.astype(BF16), ln2[None, :], w_up.astype(BF16), w_down.astype(BF16))


def kernel(x, ln1, w_in, conv_w, conv_b, igate_b, fgate_b, mlstm_norm, moba_q_norm, moba_k_norm,
           swa_q_norm, swa_k_norm, swa_sinks, w_out, ln2, w_up, w_down):
    batch, seq, d = x.shape
    assert d == D_MODEL and seq % max(IN_PROJ_ROWS, SWA_ROWS, MLSTM_ROWS, MOBA_BLOCK) == 0
    depth = ln1.shape[0]
    tables = _rope_tables(seq)
    bw = 2 * LANES_V7X
    ids = jnp.arange(bw) // HEAD_DIM
    bd = jnp.where(ids[:, None] == ids[None, :], 1.0 / HEAD_DIM, 0.0).astype(BF16)
    tids = jnp.arange(MLSTM_ROWS)
    cum = ((tids[:, None] // MLSTM_CHUNK == tids[None, :] // MLSTM_CHUNK)
           & (tids[:, None] <= tids[None, :])).astype(BF16)
    xf = x.reshape(batch * seq, d)
    for l in range(depth):
        xf = _layer(xf, tables, (bd, cum), batch, seq, ln1[l], w_in[l], conv_w[l], conv_b[l], igate_b[l],
                    fgate_b[l], mlstm_norm[l], moba_q_norm[l], moba_k_norm[l], swa_q_norm[l],
                    swa_k_norm[l], swa_sinks[l], w_out[l], ln2[l], w_up[l], w_down[l])
    return xf.reshape(batch, seq, d)
```

```python
import functools

import jax
import jax.numpy as jnp
from jax import lax
from jax.experimental import pallas as pl
from jax.experimental.pallas import tpu as pltpu

F32 = jnp.float32
BF16 = jnp.bfloat16
NEG_INF = float("-inf")

D_MODEL = 1024
HEAD_DIM = 64
MOBA_HEADS = 6
MLSTM_HEADS = 4
SWA_Q_HEADS = 6
SWA_KV_HEADS = 2
SWA_GROUP = SWA_Q_HEADS // SWA_KV_HEADS
MOBA_WIDTH = MOBA_HEADS * HEAD_DIM
MLSTM_WIDTH = MLSTM_HEADS * HEAD_DIM
SWA_Q_WIDTH = SWA_Q_HEADS * HEAD_DIM
SWA_KV_WIDTH = SWA_KV_HEADS * HEAD_DIM
MOBA_BLOCK = 256
MOBA_TOPK = 3
MLSTM_CHUNK = 256
CONV_WIDTH = 4
SWA_WINDOW = 128
ROPE_THETA = 10000.0
D_FF = 4 * D_MODEL
NORM_EPS = 1e-6
SM_SCALE = HEAD_DIM ** -0.5
LOG2E = 1.4426950408889634
MOBA_V_ROWS = HEAD_DIM + 16
MOBA_SAFE_LOG2 = 60.0

LANES_V7X = 128
VMEM_BYTES_V7X = 64 * 1024 * 1024
HEAD_PAIR = 2 * HEAD_DIM
assert HEAD_PAIR == LANES_V7X

NAT_WIDTH = MOBA_WIDTH + SWA_KV_WIDTH + 2 * MLSTM_WIDTH
KN_WIDTH = MOBA_WIDTH + SWA_KV_WIDTH
QT_ROWS = MOBA_WIDTH + SWA_Q_WIDTH
TR_ROWS = QT_ROWS + MOBA_WIDTH + SWA_KV_WIDTH + 2 * MLSTM_WIDTH + 2 * MLSTM_HEADS
NUM_GATES = 2 * MLSTM_HEADS

IN_PROJ_ROWS = 1024
OUT_MLP_ROWS = 1024
MLP_FF_CHUNK = 1024
SWA_ROWS = 512
MLSTM_ROWS = 256
MOBA_GROUP = 2
MOBA_Q_BLOCKS = 1
CONV_HALO = 16


def _dot(a, b):
    return jnp.dot(a, b, preferred_element_type=F32)


def _dot_nt(a, b):
    return lax.dot_general(a, b, (((1,), (1,)), ((), ())), preferred_element_type=F32)


def _split_bf16(v):
    hi = v.astype(BF16)
    lo = (v - hi.astype(F32)).astype(BF16)
    return hi, lo


def _split3_bf16(v):
    hi = v.astype(BF16)
    rest = v - hi.astype(F32)
    mid = rest.astype(BF16)
    return hi, mid, (rest - mid.astype(F32)).astype(BF16)


def _vmem_limit(nbytes):
    return int(min(nbytes, VMEM_BYTES_V7X - 4 * 1024 * 1024))


def _const_spec(shape):
    nd = len(shape)
    return pl.BlockSpec(shape, lambda *_: (0,) * nd, pipeline_mode=pl.Buffered(1))


def _in_proj_kernel(x_ref, ln1_ref, wnat_ref, wtr_ref, bd_ref, gk_ref, cosn_ref, sinn_ref,
                    gq_ref, cost_ref, sint_ref, gbias_ref,
                    mk_ref, kmean_ref, sk_ref, xqk_ref, mqt_ref, sqt_ref, mvt_ref, svt_ref,
                    xvt_ref, xot_ref, gt_ref):
    tm = x_ref.shape[0]
    x = x_ref[...]
    ms = jnp.mean(x * x, axis=-1, keepdims=True)
    hn = (x * lax.rsqrt(ms + NORM_EPS) * ln1_ref[...]).astype(BF16)
    nat = _dot(hn, wnat_ref[...])

    kk = nat[:, :KN_WIDTH]
    hi, lo = _split_bf16(kk * kk)
    bd = bd_ref[...]
    bw = bd.shape[0]
    msk = jnp.concatenate(
        [_dot(hi[:, c:c + bw], bd) + _dot(lo[:, c:c + bw], bd) for c in range(0, KN_WIDTH, bw)],
        axis=1)
    kn = kk * lax.rsqrt(msk + NORM_EPS) * gk_ref[...]
    reps = KN_WIDTH // LANES_V7X
    cosn = jnp.concatenate([cosn_ref[...]] * reps, axis=1)
    sinn = jnp.concatenate([sinn_ref[...]] * reps, axis=1)
    lane = lax.broadcasted_iota(jnp.int32, kn.shape, 1)
    first_half = (lane % HEAD_DIM) < (HEAD_DIM // 2)
    swapped = jnp.where(first_half,
                        pltpu.roll(kn, KN_WIDTH - HEAD_DIM // 2, 1),
                        pltpu.roll(kn, HEAD_DIM // 2, 1))
    kr = kn * cosn + swapped * sinn
    mk = kr[:, :MOBA_WIDTH]
    mk_ref[...] = mk.astype(BF16)
    nblk = tm // MOBA_BLOCK
    kmean_ref[0] = jnp.concatenate(
        [jnp.mean(mk[c * MOBA_BLOCK:(c + 1) * MOBA_BLOCK], axis=0, keepdims=True) for c in range(nblk)],
        axis=0)
    sk_ref[...] = kr[:, MOBA_WIDTH:].astype(BF16)
    xqk_ref[...] = nat[:, KN_WIDTH:].astype(BF16)

    tr = _dot_nt(wtr_ref[...], hn)
    cost = cost_ref[...]
    sint = sint_ref[...]
    gq = jnp.concatenate([gq_ref[...]] * (tm // LANES_V7X), axis=1)
    half = HEAD_DIM // 2
    for h in range(QT_ROWS // HEAD_DIM):
        blk = tr[h * HEAD_DIM:(h + 1) * HEAD_DIM]
        is_swa = h >= MOBA_HEADS
        gain = gq[HEAD_DIM:] if is_swa else gq[:HEAD_DIM]
        msq = jnp.mean(blk * blk, axis=0, keepdims=True)
        qn = blk * lax.rsqrt(msq + NORM_EPS) * gain
        x1, x2 = qn[:half], qn[half:]
        rot = jnp.concatenate([x1 * cost - x2 * sint, x2 * cost + x1 * sint], axis=0).astype(BF16)
        if is_swa:
            r0 = (h - MOBA_HEADS) * HEAD_DIM
            sqt_ref[r0:r0 + HEAD_DIM, :] = rot
        else:
            mqt_ref[h * HEAD_DIM:(h + 1) * HEAD_DIM, :] = rot
    r = QT_ROWS
    mv = tr[r:r + MOBA_WIDTH].astype(BF16)
    ones = jnp.ones((MOBA_V_ROWS - HEAD_DIM, MOBA_BLOCK), BF16)
    for c in range(nblk):
        for h in range(MOBA_HEADS):
            r0 = h * MOBA_V_ROWS
            mvt_ref[c, r0:r0 + HEAD_DIM, :] = mv[h * HEAD_DIM:(h + 1) * HEAD_DIM,
                                                 c * MOBA_BLOCK:(c + 1) * MOBA_BLOCK]
            mvt_ref[c, r0 + HEAD_DIM:r0 + MOBA_V_ROWS, :] = ones
    r += MOBA_WIDTH
    svt_ref[...] = tr[r:r + SWA_KV_WIDTH].astype(BF16)
    r += SWA_KV_WIDTH
    xvt_ref[...] = tr[r:r + MLSTM_WIDTH].astype(BF16)
    r += MLSTM_WIDTH
    xot_ref[...] = tr[r:r + MLSTM_WIDTH].astype(BF16)
    r += MLSTM_WIDTH
    gbias = jnp.concatenate([gbias_ref[...]] * (tm // LANES_V7X), axis=1)
    gt_ref[...] = tr[r:r + NUM_GATES] + gbias


def _in_proj(xf, ln1, wnat, wtr, bd, gk, cosn, sinn, gq, cost, sint, gbias, seq):
    t = xf.shape[0]
    tm = IN_PROJ_ROWS
    steps = t // tm
    seq_steps = seq // tm
    nblk = tm // MOBA_BLOCK
    row = lambda w: pl.BlockSpec((tm, w), lambda i: (i, 0))
    col = lambda r: pl.BlockSpec((r, tm), lambda i: (0, i))
    in_specs = [
        row(D_MODEL),
        _const_spec((1, D_MODEL)),
        _const_spec(wnat.shape),
        _const_spec(wtr.shape),
        _const_spec(bd.shape),
        _const_spec(gk.shape),
        pl.BlockSpec((tm, LANES_V7X), lambda i: (i % seq_steps, 0)),
        pl.BlockSpec((tm, LANES_V7X), lambda i: (i % seq_steps, 0)),
        _const_spec(gq.shape),
        pl.BlockSpec((HEAD_DIM // 2, tm), lambda i: (0, i % seq_steps)),
        pl.BlockSpec((HEAD_DIM // 2, tm), lambda i: (0, i % seq_steps)),
        _const_spec(gbias.shape),
    ]
    out_shape = [
        jax.ShapeDtypeStruct((t, MOBA_WIDTH), BF16),
        jax.ShapeDtypeStruct((steps, nblk, MOBA_WIDTH), F32),
        jax.ShapeDtypeStruct((t, SWA_KV_WIDTH), BF16),
        jax.ShapeDtypeStruct((t, 2 * MLSTM_WIDTH), BF16),
        jax.ShapeDtypeStruct((MOBA_WIDTH, t), BF16),
        jax.ShapeDtypeStruct((SWA_Q_WIDTH, t), BF16),
        jax.ShapeDtypeStruct((t // MOBA_BLOCK, MOBA_HEADS * MOBA_V_ROWS, MOBA_BLOCK), BF16),
        jax.ShapeDtypeStruct((SWA_KV_WIDTH, t), BF16),
        jax.ShapeDtypeStruct((MLSTM_WIDTH, t), BF16),
        jax.ShapeDtypeStruct((MLSTM_WIDTH, t), BF16),
        jax.ShapeDtypeStruct((NUM_GATES, t), F32),
    ]
    out_specs = [
        row(MOBA_WIDTH),
        pl.BlockSpec((1, nblk, MOBA_WIDTH), lambda i: (i, 0, 0)),
        row(SWA_KV_WIDTH),
        row(2 * MLSTM_WIDTH),
        col(MOBA_WIDTH),
        col(SWA_Q_WIDTH),
        pl.BlockSpec((nblk, MOBA_HEADS * MOBA_V_ROWS, MOBA_BLOCK), lambda i: (i, 0, 0)),
        col(SWA_KV_WIDTH),
        col(MLSTM_WIDTH),
        col(MLSTM_WIDTH),
        col(NUM_GATES),
    ]
    weights = 2 * (wnat.size + wtr.size)
    tiles = tm * (2 * 4 * D_MODEL + 3 * 4 * (NAT_WIDTH + TR_ROWS) + 2 * 2 * (NAT_WIDTH + TR_ROWS))
    return pl.pallas_call(
        _in_proj_kernel,
        grid=(steps,),
        in_specs=in_specs,
        out_specs=out_specs,
        out_shape=out_shape,
        compiler_params=pltpu.CompilerParams(
            dimension_semantics=("arbitrary",),
            vmem_limit_bytes=_vmem_limit(2 * weights + tiles + (8 << 20))),
        name="in_proj",
    )(xf, ln1, wnat, wtr, bd, gk, cosn, sinn, gq, cost, sint, gbias)


def _moba_kernel(bounded_ref, qt_ref, k_ref, vt_ref, km_ref, o_ref, sel_ref, s_ref, p_ref):
    nb = k_ref.shape[0]
    lq = MOBA_BLOCK
    nh = MOBA_Q_BLOCKS * MOBA_HEADS
    step = pl.program_id(1)
    row = lax.broadcasted_iota(jnp.int32, (HEAD_PAIR, lq), 0)
    own, qh = [], []
    for s in range(nh):
        qb, h = divmod(s, MOBA_HEADS)
        own.append(step * MOBA_Q_BLOCKS + qb)
        qt = qt_ref[(h // 2) * HEAD_PAIR:(h // 2 + 1) * HEAD_PAIR, qb * lq:(qb + 1) * lq]
        keep = (row < HEAD_DIM) if h % 2 == 0 else (row >= HEAD_DIM)
        qh.append(jnp.where(keep, qt, jnp.zeros_like(qt)))
    last_own = step * MOBA_Q_BLOCKS + (MOBA_Q_BLOCKS - 1)

    def k_group(blk0, count, s):
        g = (s % MOBA_HEADS) // 2
        return k_ref[pl.ds(blk0, count), :, g * HEAD_PAIR:(g + 1) * HEAD_PAIR].reshape(count * lq, HEAD_PAIR)

    def v_aug(j, s):
        h = s % MOBA_HEADS
        return vt_ref[j, h * MOBA_V_ROWS:(h + 1) * MOBA_V_ROWS, :]

    def select_blocks():
        gates = []
        for s in range(nh):
            g = (s % MOBA_HEADS) // 2
            km_hi, km_lo = _split_bf16(km_ref[:, g * HEAD_PAIR:(g + 1) * HEAD_PAIR])
            gates.append(_dot(km_hi, qh[s]) + _dot(km_lo, qh[s]))
        gate = jnp.concatenate(gates, axis=1)
        own_all = jnp.concatenate([jnp.full((1, lq), own[s], jnp.int32) for s in range(nh)], axis=1)
        blk_all = lax.broadcasted_iota(jnp.int32, gate.shape, 0)
        gate = jnp.where(blk_all < own_all, gate, NEG_INF)
        sel = jnp.full(gate.shape, NEG_INF, F32)
        for _ in range(MOBA_TOPK):
            mx = jnp.max(gate, axis=0, keepdims=True)
            idx = jnp.min(jnp.where(gate == mx, blk_all, nb), axis=0, keepdims=True)
            pick = blk_all == jnp.where(idx < own_all, idx, nb)
            sel = jnp.where(pick, 0.0, sel)
            gate = jnp.where(pick, NEG_INF, gate)
        for s in range(nh):
            sel_ref[s] = sel[:, s * lq:(s + 1) * lq]

    kpos = lax.broadcasted_iota(jnp.int32, (lq, lq), 0)
    qpos = lax.broadcasted_iota(jnp.int32, (lq, lq), 1)
    causal = kpos <= qpos

    def weighted_values(st, mu, v_rows):
        pv = _dot(v_rows, jnp.exp2(st - mu).astype(BF16))
        return pv[HEAD_DIM:HEAD_DIM + 1], pv[:HEAD_DIM]

    def finish(nums, dens):
        for s in range(0, nh, 2):
            qb, h = divmod(s, MOBA_HEADS)
            ot = jnp.concatenate([nums[s] / dens[s], nums[s + 1] / dens[s + 1]], axis=0)
            o_ref[qb * lq:(qb + 1) * lq, (h // 2) * HEAD_PAIR:(h // 2 + 1) * HEAD_PAIR] = ot.T.astype(BF16)

    trips = lax.shift_right_logical(last_own + (2 * MOBA_GROUP - 1), MOBA_GROUP.bit_length())
    units = nh * MOBA_GROUP
    own_slot = 2 * units

    @pl.when(bounded_ref[0] != 0)
    def _():
        select_blocks()
        for h in range(nh):
            st = jnp.where(causal, _dot(k_group(own[h], 1, h), qh[h]), NEG_INF)
            p_ref[own_slot + h, 0:lq, :] = jnp.exp2(st).astype(BF16)

        def stage_p(blk0, slot0):
            blk0 = jnp.minimum(blk0, nb - MOBA_GROUP)
            for h in range(nh):
                st = _dot(k_group(blk0, MOBA_GROUP, h), qh[h])
                sel = jnp.concatenate(
                    [jnp.broadcast_to(sel_ref[h, pl.ds(blk0 + u, 1), :], (lq, lq)) for u in range(MOBA_GROUP)],
                    axis=0)
                slot = slot0 + MOBA_GROUP * h
                p_ref[slot:slot + MOBA_GROUP, 0:lq, :] = (
                    jnp.exp2(st + sel).astype(BF16).reshape(MOBA_GROUP, lq, lq))

        def stage_v(blk0, slot0, acc):
            acc = list(acc)
            for h in range(nh):
                slot = slot0 + MOBA_GROUP * h
                p = p_ref[slot:slot + MOBA_GROUP, 0:lq, :].reshape(MOBA_GROUP * lq, lq)
                v = jnp.concatenate([v_aug(blk0 + u, h) for u in range(MOBA_GROUP)], axis=1)
                acc[h] = acc[h] + _dot(v, p)
            return acc

        def trip(t, acc, last):
            blk = t * (2 * MOBA_GROUP)
            stage_p(blk + MOBA_GROUP, units)
            acc = stage_v(blk, 0, acc)
            if not last:
                stage_p(blk + 2 * MOBA_GROUP, 0)
            return tuple(stage_v(blk + MOBA_GROUP, units, acc))

        stage_p(0, 0)
        acc = tuple(_dot(v_aug(own[h], h), p_ref[own_slot + h, 0:lq, :]) for h in range(nh))
        acc = lax.fori_loop(0, jnp.maximum(trips - 1, 0), lambda t, a: trip(t, a, False), acc)
        acc = lax.cond(trips > 0, lambda a: trip(trips - 1, a, True), lambda a: a, acc)
        finish([a[:HEAD_DIM] for a in acc], [a[HEAD_DIM:HEAD_DIM + 1] for a in acc])


    def stage_a(blk0, slot0):
        mus = []
        for u in range(MOBA_GROUP):
            for h in range(nh):
                st = _dot(k_group(jnp.minimum(blk0 + u, nb - 1), 1, h), qh[h])
                s_ref[slot0 + nh * u + h] = st
                mus.append(jnp.max(st, axis=0, keepdims=True))
        return mus

    def stage_b(blk0, slot0, mus, state):
        state = list(state)
        for u in range(MOBA_GROUP):
            j = blk0 + u
            for h in range(nh):
                mu = mus[nh * u + h]
                ls, pv = weighted_values(s_ref[slot0 + nh * u + h], mu, v_aug(j, h))
                m, l, acc = state[3 * h:3 * h + 3]
                mu_sel = mu + sel_ref[h, pl.ds(j, 1), :]
                m_new = jnp.maximum(m, mu_sel)
                f = jnp.exp2(mu_sel - m_new)
                c = jnp.exp2(m - m_new)
                state[3 * h:3 * h + 3] = [m_new, c * l + f * ls, c * acc + f * pv]
        return state

    def body(t, loop_carry):
        mus0, state = loop_carry[:units], loop_carry[units:]
        blk = t * (2 * MOBA_GROUP)
        mus1 = stage_a(blk + MOBA_GROUP, units)
        state = stage_b(blk, 0, mus0, state)
        mus0 = stage_a(blk + 2 * MOBA_GROUP, 0)
        state = stage_b(blk + MOBA_GROUP, units, mus1, state)
        return (*mus0, *state)

    @pl.when(bounded_ref[0] == 0)
    def _():
        state = []
        for h in range(nh):
            st = jnp.where(causal, _dot(k_group(own[h], 1, h), qh[h]), NEG_INF)
            mu = jnp.max(st, axis=0, keepdims=True)
            state += [mu, *weighted_values(st, mu, v_aug(own[h], h))]
        mus0 = stage_a(0, 0)
        select_blocks()
        state = lax.fori_loop(0, trips, body, (*mus0, *state))[units:]
        finish(state[2::3], state[1::3])


def _moba(bounded, mqt, mk, mvt, kmean, batch, seq):
    t = mk.shape[0]
    nb = seq // MOBA_BLOCK
    lq = MOBA_BLOCK
    k3 = mk.reshape(t // MOBA_BLOCK, MOBA_BLOCK, MOBA_WIDTH)
    assert nb % (2 * MOBA_GROUP) == 0 and nb % MOBA_Q_BLOCKS == 0
    streams = MOBA_Q_BLOCKS * MOBA_HEADS
    units = streams * MOBA_GROUP
    steps = nb // MOBA_Q_BLOCKS
    kv_bytes = 2 * nb * MOBA_BLOCK * (MOBA_WIDTH + MOBA_HEADS * MOBA_V_ROWS)
    scratch_bytes = lq * lq * (4 * 2 * units + 2 * (2 * units + streams)) + 4 * streams * nb * lq
    resident = lambda shape, imap: pl.BlockSpec(shape, imap, pipeline_mode=pl.Buffered(1))
    return pl.pallas_call(
        _moba_kernel,
        grid=(batch, steps),
        in_specs=[
            pl.BlockSpec(memory_space=pltpu.SMEM),
            pl.BlockSpec((MOBA_WIDTH, MOBA_Q_BLOCKS * lq), lambda b, i: (0, b * steps + i)),
            resident((nb, MOBA_BLOCK, MOBA_WIDTH), lambda b, i: (b, 0, 0)),
            resident((nb, MOBA_HEADS * MOBA_V_ROWS, MOBA_BLOCK), lambda b, i: (b, 0, 0)),
            resident((nb, MOBA_WIDTH), lambda b, i: (b, 0)),
        ],
        out_specs=pl.BlockSpec((MOBA_Q_BLOCKS * lq, MOBA_WIDTH), lambda b, i: (b * steps + i, 0)),
        out_shape=jax.ShapeDtypeStruct((t, MOBA_WIDTH), BF16),
        scratch_shapes=[pltpu.VMEM((streams, nb, lq), F32),
                        pltpu.VMEM((2 * units, lq, lq), F32),
                        pltpu.VMEM((2 * units + streams, lq + 16, lq), BF16)],
        compiler_params=pltpu.CompilerParams(
            dimension_semantics=("arbitrary", "arbitrary"),
            vmem_limit_bytes=_vmem_limit(kv_bytes + scratch_bytes + (12 << 20))),
        name="moba",
    )(bounded, mqt, k3, mvt, kmean)


def _swa_kernel(bounded_ref, sink_ref, qt_ref, k_ref, kh_ref, vt_ref, vth_ref, o_ref, p_ref):
    tq = k_ref.shape[0]
    w = SWA_WINDOW
    i = pl.program_id(1)
    ncol = SWA_GROUP * w
    kpos = lax.broadcasted_iota(jnp.int32, (2 * w, ncol), 0)
    col = lax.broadcasted_iota(jnp.int32, (2 * w, ncol), 1)
    diff = w + (col % w) - kpos
    in_window = (diff >= 0) & (diff < w)
    zeros = jnp.zeros((HEAD_DIM, ncol), BF16)
    ones = jnp.ones((MOBA_V_ROWS - HEAD_DIM, 2 * w), BF16)

    def attend(shifted):
        sink_terms = []
        for r in range(tq // w):
            if r == 0:
                kcat = jnp.concatenate([kh_ref[...], k_ref[0:w, :]], axis=0)
                mask = in_window & ((kpos >= w) | (i > 0))
            else:
                kcat = k_ref[(r - 1) * w:(r + 1) * w, :]
                mask = in_window
            for g in range(SWA_KV_HEADS):
                qs = jnp.concatenate(
                    [qt_ref[(SWA_GROUP * g + a) * HEAD_DIM:(SWA_GROUP * g + a + 1) * HEAD_DIM, r * w:(r + 1) * w]
                     for a in range(SWA_GROUP)], axis=1)
                qpad = jnp.concatenate([qs, zeros] if g == 0 else [zeros, qs], axis=0)
                st = jnp.where(mask, _dot(kcat, qpad), NEG_INF)
                sink = jnp.concatenate(
                    [jnp.full((1, w), sink_ref[SWA_GROUP * g + a] * LOG2E, F32) for a in range(SWA_GROUP)],
                    axis=1)
                if shifted:
                    m = jnp.maximum(jnp.max(st, axis=0, keepdims=True), sink)
                    st, sink = st - m, sink - m
                p_ref[r * SWA_KV_HEADS + g] = jnp.exp2(st).astype(BF16)
                sink_terms.append(jnp.exp2(sink))
        for r in range(tq // w):
            if r == 0:
                vcat = jnp.concatenate([vth_ref[...], vt_ref[:, 0:w]], axis=1)
            else:
                vcat = vt_ref[:, (r - 1) * w:(r + 1) * w]
            pieces = []
            for g in range(SWA_KV_HEADS):
                v_aug = jnp.concatenate([vcat[g * HEAD_DIM:(g + 1) * HEAD_DIM, :], ones], axis=0)
                pv = _dot(v_aug, p_ref[r * SWA_KV_HEADS + g])
                ot = pv[:HEAD_DIM] / (pv[HEAD_DIM:HEAD_DIM + 1] + sink_terms[r * SWA_KV_HEADS + g])
                pieces += [ot[:, a * w:(a + 1) * w] for a in range(SWA_GROUP)]
            o_ref[r * w:(r + 1) * w, :] = jnp.concatenate(pieces, axis=0).T.astype(BF16)

    @pl.when(bounded_ref[0] != 0)
    def _():
        attend(False)

    @pl.when(bounded_ref[0] == 0)
    def _():
        attend(True)


def _swa(bounded, sinks, sqt, sk, svt, batch, seq):
    t = sk.shape[0]
    tq = SWA_ROWS
    w = SWA_WINDOW
    steps = seq // tq
    halo = lambda b, i: jnp.maximum((b * seq + i * tq) // w - 1, 0)
    return pl.pallas_call(
        _swa_kernel,
        grid=(batch, steps),
        in_specs=[
            pl.BlockSpec(memory_space=pltpu.SMEM),
            pl.BlockSpec(memory_space=pltpu.SMEM),
            pl.BlockSpec((SWA_Q_WIDTH, tq), lambda b, i: (0, b * steps + i)),
            pl.BlockSpec((tq, SWA_KV_WIDTH), lambda b, i: (b * steps + i, 0)),
            pl.BlockSpec((w, SWA_KV_WIDTH), lambda b, i: (halo(b, i), 0)),
            pl.BlockSpec((SWA_KV_WIDTH, tq), lambda b, i: (0, b * steps + i)),
            pl.BlockSpec((SWA_KV_WIDTH, w), lambda b, i: (0, halo(b, i))),
        ],
        out_specs=pl.BlockSpec((tq, SWA_Q_WIDTH), lambda b, i: (b * steps + i, 0)),
        out_shape=jax.ShapeDtypeStruct((t, SWA_Q_WIDTH), BF16),
        scratch_shapes=[pltpu.VMEM(((tq // w) * SWA_KV_HEADS, 2 * w, SWA_GROUP * w), BF16)],
        compiler_params=pltpu.CompilerParams(dimension_semantics=("arbitrary", "arbitrary")),
        name="swa",
    )(bounded, sinks, sqt, sk, sk, svt, svt)


def _log_sigmoid(v):
    return jnp.minimum(v, 0.0) - jnp.log1p(jnp.exp(-jnp.abs(v)))


def _mlstm_kernel(xqk_ref, halo_ref, xvt_ref, xot_ref, gt_ref, gc_ref, convw_ref, convb_ref,
                  gain_ref, cum_ref, cumt_ref, o_ref, ext_ref, c_ref, n_ref, m_ref,
                  qm_ref, k_ref, pv_ref, row_ref, kv_ref, nk_ref, cprev_ref, nprev_ref, ht_ref):
    tt = xqk_ref.shape[0]
    lc = MLSTM_CHUNK
    i = pl.program_id(1)

    @pl.when(i == 0)
    def _():
        c_ref[...] = jnp.zeros_like(c_ref)
        n_ref[...] = jnp.zeros_like(n_ref)
        m_ref[...] = jnp.zeros_like(m_ref)

    hist = halo_ref[...].astype(F32)
    ext_ref[0:CONV_HALO, :] = jnp.where(i > 0, hist, jnp.zeros_like(hist))
    ext_ref[CONV_HALO:CONV_HALO + tt, :] = xqk_ref[...].astype(F32)
    convw = convw_ref[...]
    conv = jnp.zeros((tt, 2 * MLSTM_WIDTH), F32) + convb_ref[...]
    for j in range(CONV_WIDTH):
        start = CONV_HALO - (CONV_WIDTH - 1) + j
        conv = conv + convw[j:j + 1, :] * ext_ref[start:start + tt, :]
    qk = conv * jax.nn.sigmoid(conv)
    q_all = qk[:, :MLSTM_WIDTH]
    lane_q = lax.broadcasted_iota(jnp.int32, q_all.shape, 1)
    even_head = (lane_q % HEAD_PAIR) < HEAD_DIM
    qm_ref[0] = jnp.where(even_head, q_all, 0.0).astype(BF16)
    qm_ref[1] = jnp.where(even_head, 0.0, q_all).astype(BF16)
    k_ref[...] = (qk[:, MLSTM_WIDTH:] * SM_SCALE).astype(BF16)

    g_row = gt_ref[...]
    g_col = gc_ref[...]
    cum = cum_ref[...]
    cumt = cumt_ref[...]
    b_row = sum(_dot(piece, cum) for piece in _split3_bf16(_log_sigmoid(g_row)))
    b_col = sum(_dot(cumt, piece) for piece in _split3_bf16(_log_sigmoid(g_col)))
    u_col = b_col[:, MLSTM_HEADS:] - g_col[:, :MLSTM_HEADS]

    lane = lax.broadcasted_iota(jnp.int32, (HEAD_DIM, HEAD_PAIR), 1)
    s_idx = lax.broadcasted_iota(jnp.int32, (lc, lc), 0)
    t_idx = lax.broadcasted_iota(jnp.int32, (lc, lc), 1)
    causal = s_idx <= t_idx
    nchunks = tt // lc
    heads = range(MLSTM_HEADS)
    in_head = [(lane < HEAD_DIM), (lane >= HEAD_DIM)]

    def operands(c, h):
        p, hh = divmod(h, 2)
        sl = slice(c * lc, (c + 1) * lc)
        qm = qm_ref[hh, sl, p * HEAD_PAIR:(p + 1) * HEAD_PAIR]
        kp = k_ref[sl, p * HEAD_PAIR:(p + 1) * HEAD_PAIR]
        return sl, hh, qm, kp

    m_state = [m_ref[h][0:1, 0:lc] for h in heads]
    for c in range(nchunks):
        for h in heads:
            u = c * MLSTM_HEADS + h
            sl, hh, qm, kp = operands(c, h)
            vt = xvt_ref[h * HEAD_DIM:(h + 1) * HEAD_DIM, sl]
            br = b_row[MLSTM_HEADS + h:MLSTM_HEADS + h + 1, sl]
            ir = g_row[h:h + 1, sl]
            uc = u_col[sl, h:h + 1]
            a = br[:, lc - 1:lc]
            dt = jnp.where(causal, br - uc, NEG_INF)
            inter = br + m_state[h]
            m_t = jnp.maximum(inter, jnp.max(dt, axis=0, keepdims=True))
            w_inter = jnp.exp(inter - m_t)
            qkt = _dot_nt(kp, qm) * jnp.exp(dt - m_t)
            pv_ref[u] = _dot(vt, qkt.astype(BF16))
            row_ref[u, 0:1, 0:lc] = w_inter
            row_ref[u, 1:2, 0:lc] = jnp.sum(qkt, axis=0, keepdims=True)
            row_ref[u, 2:3, 0:lc] = jnp.exp(-m_t)
            g_end = a - br + ir
            m_new = jnp.maximum(a + m_state[h], jnp.max(g_end, axis=1, keepdims=True))
            w_s = jnp.exp(g_end - m_new)
            decay = jnp.exp(a + m_state[h] - m_new)
            row_ref[u, 3:4, 0:lc] = decay
            vtw = (vt.astype(F32) * w_s).astype(BF16)
            kv_ref[u] = jnp.where(in_head[hh], _dot(vtw, kp), 0.0)
            w_rows = jnp.broadcast_to(w_s, (8, lc)).astype(BF16)
            nk_ref[u] = jnp.where(in_head[hh][0:8], _dot(w_rows, kp), 0.0)
            m_state[h] = m_new

    for h in heads:
        c_state = c_ref[h]
        n_state = n_ref[h]
        for c in range(nchunks):
            u = c * MLSTM_HEADS + h
            cprev_ref[u] = c_state.astype(BF16)
            nprev_ref[u] = n_state
            decay = row_ref[u, 3:4, 0:HEAD_PAIR]
            c_state = decay * c_state + kv_ref[u]
            n_state = decay * n_state + nk_ref[u]
        c_ref[h] = c_state
        n_ref[h] = n_state
        m_ref[h] = jnp.broadcast_to(m_state[h][:, 0:1], m_ref.shape[1:])

    for c in range(nchunks):
        for h in heads:
            u = c * MLSTM_HEADS + h
            sl, hh, qm, _ = operands(c, h)
            w_inter = row_ref[u, 0:1, 0:lc]
            num = w_inter * _dot_nt(cprev_ref[u], qm) + pv_ref[u]
            den = w_inter * _dot_nt(nprev_ref[u].astype(BF16), qm)[0:1, :] + row_ref[u, 1:2, 0:lc]
            ht_ref[h * HEAD_DIM:(h + 1) * HEAD_DIM, sl] = num / jnp.maximum(jnp.abs(den), row_ref[u, 2:3, 0:lc])

    gain = jnp.concatenate([gain_ref[...]] * (tt // LANES_V7X), axis=1)
    outs = []
    for h in heads:
        rows = slice(h * HEAD_DIM, (h + 1) * HEAD_DIM)
        hg = ht_ref[rows, :] * jax.nn.sigmoid(xot_ref[rows, :].astype(F32))
        msq = jnp.mean(hg * hg, axis=0, keepdims=True)
        outs.append(hg * lax.rsqrt(msq + NORM_EPS) * gain[rows])
    o_ref[...] = jnp.concatenate(outs, axis=0).T.astype(BF16)


def _mlstm(xqk, xvt, xot, gt, gc, convw, convb, gain, cum, batch, seq):
    t = xqk.shape[0]
    tt = MLSTM_ROWS
    steps = seq // tt
    units = MLSTM_HEADS * (tt // MLSTM_CHUNK)
    col = lambda r: pl.BlockSpec((r, tt), lambda b, i: (0, b * steps + i))
    return pl.pallas_call(
        _mlstm_kernel,
        grid=(batch, steps),
        in_specs=[
            pl.BlockSpec((tt, 2 * MLSTM_WIDTH), lambda b, i: (b * steps + i, 0)),
            pl.BlockSpec((CONV_HALO, 2 * MLSTM_WIDTH),
                         lambda b, i: (jnp.maximum((b * seq + i * tt) // CONV_HALO - 1, 0), 0)),
            col(MLSTM_WIDTH),
            col(MLSTM_WIDTH),
            col(NUM_GATES),
            pl.BlockSpec((tt, NUM_GATES), lambda b, i: (b * steps + i, 0)),
            _const_spec(convw.shape),
            _const_spec(convb.shape),
            _const_spec(gain.shape),
            _const_spec(cum.shape),
            _const_spec(cum.shape),
        ],
        out_specs=pl.BlockSpec((tt, MLSTM_WIDTH), lambda b, i: (b * steps + i, 0)),
        out_shape=jax.ShapeDtypeStruct((t, MLSTM_WIDTH), BF16),
        scratch_shapes=[
            pltpu.VMEM((CONV_HALO + tt, 2 * MLSTM_WIDTH), F32),
            pltpu.VMEM((MLSTM_HEADS, HEAD_DIM, HEAD_PAIR), F32),
            pltpu.VMEM((MLSTM_HEADS, 8, HEAD_PAIR), F32),
            pltpu.VMEM((MLSTM_HEADS, 8, MLSTM_CHUNK), F32),
            pltpu.VMEM((2, tt, MLSTM_WIDTH), BF16),
            pltpu.VMEM((tt, MLSTM_WIDTH), BF16),
            pltpu.VMEM((units, HEAD_DIM, MLSTM_CHUNK), F32),
            pltpu.VMEM((units, 8, MLSTM_CHUNK), F32),
            pltpu.VMEM((units, HEAD_DIM, HEAD_PAIR), F32),
            pltpu.VMEM((units, 8, HEAD_PAIR), F32),
            pltpu.VMEM((units, HEAD_DIM, HEAD_PAIR), BF16),
            pltpu.VMEM((units, 8, HEAD_PAIR), F32),
            pltpu.VMEM((MLSTM_WIDTH, tt), F32),
        ],
        compiler_params=pltpu.CompilerParams(dimension_semantics=("arbitrary", "arbitrary")),
        name="mlstm",
    )(xqk, xqk, xvt, xot, gt, gc, convw, convb, gain, cum, cum.T)


def _out_mlp_kernel(x_ref, ym_ref, yl_ref, ys_ref, wo_ref, ln2_ref, wup_ref, wdn_ref, o_ref):
    y = jnp.concatenate([ym_ref[...], yl_ref[...], ys_ref[...]], axis=1)
    x1 = x_ref[...] + _dot(y, wo_ref[...])
    ms = jnp.mean(x1 * x1, axis=-1, keepdims=True)
    hn = (x1 * lax.rsqrt(ms + NORM_EPS) * ln2_ref[...]).astype(BF16)
    o_ref[...] = x1
    for c in range(0, D_FF, MLP_FF_CHUNK):
        u = _dot(hn, wup_ref[:, c:c + MLP_FF_CHUNK])
        act = jnp.square(jnp.maximum(u, 0.0)).astype(BF16)
        o_ref[...] += _dot(act, wdn_ref[c:c + MLP_FF_CHUNK, :])


def _out_mlp(xf, ym, yl, ys, wo, ln2, wup, wdn):
    t = xf.shape[0]
    tm = OUT_MLP_ROWS
    row = lambda w: pl.BlockSpec((tm, w), lambda i: (i, 0))
    weights = 2 * (wo.size + wup.size + wdn.size)
    tiles = tm * (4 * 4 * D_MODEL + 2 * 2 * D_MODEL + 4 * 4 * D_MODEL + 6 * MLP_FF_CHUNK)
    return pl.pallas_call(
        _out_mlp_kernel,
        grid=(t // tm,),
        in_specs=[
            row(D_MODEL), row(MOBA_WIDTH), row(MLSTM_WIDTH), row(SWA_Q_WIDTH),
            _const_spec(wo.shape), _const_spec(ln2.shape), _const_spec(wup.shape), _const_spec(wdn.shape),
        ],
        out_specs=row(D_MODEL),
        out_shape=jax.ShapeDtypeStruct((t, D_MODEL), F32),
        compiler_params=pltpu.CompilerParams(
            dimension_semantics=("arbitrary",),
            vmem_limit_bytes=_vmem_limit(weights + tiles + (8 << 20))),
        name="out_mlp",
    )(xf, ym, yl, ys, wo, ln2, wup, wdn)


def _rope_tables(seq):
    inv = ROPE_THETA ** (-jnp.arange(0, HEAD_DIM, 2, dtype=F32) / HEAD_DIM)
    ang = jnp.arange(seq, dtype=F32)[:, None] * inv[None, :]
    cos, sin = jnp.cos(ang), jnp.sin(ang)
    cosn = jnp.concatenate([cos, cos, cos, cos], axis=1)
    sinn = jnp.concatenate([-sin, sin, -sin, sin], axis=1)
    return cosn, sinn, cos.T, sin.T


def _layer(xf, tables, consts, batch, seq, ln1, w_in, conv_w, conv_b, igate_b, fgate_b, mlstm_norm,
           moba_q_norm, moba_k_norm, swa_q_norm, swa_k_norm, swa_sinks, w_out, ln2, w_up, w_down):
    cosn, sinn, cost, sint = tables
    bd, cum = consts
    o = 0
    cols = {}
    for name, width in (("mq", MOBA_WIDTH), ("mk", MOBA_WIDTH), ("mv", MOBA_WIDTH), ("sq", SWA_Q_WIDTH),
                        ("sk", SWA_KV_WIDTH), ("sv", SWA_KV_WIDTH), ("xqk", 2 * MLSTM_WIDTH),
                        ("xv", MLSTM_WIDTH), ("xo", MLSTM_WIDTH), ("xi", MLSTM_HEADS), ("xf", MLSTM_HEADS)):
        cols[name] = w_in[:, o:o + width]
        o += width
    wnat = jnp.concatenate([cols["mk"], cols["sk"], cols["xqk"]], axis=1).astype(BF16)
    wtr = jnp.concatenate([cols["mq"], cols["sq"], cols["mv"], cols["sv"], cols["xv"], cols["xo"],
                           cols["xi"], cols["xf"]], axis=1).T.astype(BF16)
    gk = jnp.concatenate([jnp.tile(moba_k_norm, MOBA_HEADS), jnp.tile(swa_k_norm, SWA_KV_HEADS)])[None, :]
    gq = jnp.broadcast_to(
        (jnp.concatenate([moba_q_norm, swa_q_norm]) * (SM_SCALE * LOG2E))[:, None], (2 * HEAD_DIM, LANES_V7X))
    gbias = jnp.broadcast_to(jnp.concatenate([igate_b, fgate_b])[:, None], (NUM_GATES, LANES_V7X))

    (mk, kmean, sk, xqk, mqt, sqt, mvt, svt, xvt, xot, gt) = _in_proj(
        xf, ln1[None, :], wnat, wtr, bd, gk, cosn, sinn, gq, cost, sint, gbias, seq)

    kmean = kmean.reshape(-1, MOBA_WIDTH)
    def bounded(q_gain, k_gain, *extra):
        bound = (HEAD_DIM * SM_SCALE * LOG2E) * jnp.max(jnp.abs(q_gain)) * jnp.max(jnp.abs(k_gain))
        for e in extra:
            bound = jnp.maximum(bound, jnp.max(jnp.abs(e)) * LOG2E)
        return (bound <= MOBA_SAFE_LOG2).astype(jnp.int32).reshape(1)

    ym = _moba(bounded(moba_q_norm, moba_k_norm), mqt, mk, mvt, kmean, batch, seq)
    ys = _swa(bounded(swa_q_norm, swa_k_norm, swa_sinks), swa_sinks, sqt, sk, svt, batch, seq)
    gain = jnp.broadcast_to(mlstm_norm.reshape(MLSTM_WIDTH, 1), (MLSTM_WIDTH, LANES_V7X))
    yl = _mlstm(xqk, xvt, xot, gt, gt.T, conv_w, conv_b[None, :], gain, cum, batch, seq)

    return _out_mlp(xf, ym, yl, ys, w_out.astype(BF16), ln2[None, :], w_up.astype(BF16), w_down.astype(BF16))


def kernel(x, ln1, w_in, conv_w, conv_b, igate_b, fgate_b, mlstm_norm, moba_q_norm, moba_k_norm,
           swa_q_norm, swa_k_norm, swa_sinks, w_out, ln2, w_up, w_down):
    batch, seq, d = x.shape
    assert d == D_MODEL and seq % max(IN_PROJ_ROWS, SWA_ROWS, MLSTM_ROWS, MOBA_BLOCK) == 0
    depth = ln1.shape[0]
    tables = _rope_tables(seq)
    bw = 2 * LANES_V7X
    ids = jnp.arange(bw) // HEAD_DIM
    bd = jnp.where(ids[:, None] == ids[None, :], 1.0 / HEAD_DIM, 0.0).astype(BF16)
    tids = jnp.arange(MLSTM_ROWS)
    cum = ((tids[:, None] // MLSTM_CHUNK == tids[None, :] // MLSTM_CHUNK)
           & (tids[:, None] <= tids[None, :])).astype(BF16)
    xf = x.reshape(batch * seq, d)
    for l in range(depth):
        xf = _layer(xf, tables, (bd, cum), batch, seq, ln1[l], w_in[l], conv_w[l], conv_b[l], igate_b[l],
                    fgate_b[l], mlstm_norm[l], moba_q_norm[l], moba_k_norm[l], swa_q_norm[l],
                    swa_k_norm[l], swa_sinks[l], w_out[l], ln2[l], w_up[l], w_down[l])
    return xf.reshape(batch, seq, d)
```

```python
import jax
import jax.numpy as jnp
from jax import lax
from jax.experimental import pallas as pl
from jax.experimental.pallas import tpu as pltpu

F32 = jnp.float32
BF16 = jnp.bfloat16
NEG_INF = float("-inf")

D_MODEL = 1024
HEAD_DIM = 64
MOBA_HEADS = 6
MLSTM_HEADS = 4
SWA_Q_HEADS = 6
SWA_KV_HEADS = 2
SWA_GROUP = SWA_Q_HEADS // SWA_KV_HEADS
MOBA_WIDTH = MOBA_HEADS * HEAD_DIM
MLSTM_WIDTH = MLSTM_HEADS * HEAD_DIM
SWA_Q_WIDTH = SWA_Q_HEADS * HEAD_DIM
SWA_KV_WIDTH = SWA_KV_HEADS * HEAD_DIM
MOBA_BLOCK = 256
MOBA_TOPK = 3
MLSTM_CHUNK = 256
CONV_WIDTH = 4
SWA_WINDOW = 128
ROPE_THETA = 10000.0
D_FF = 4 * D_MODEL
NORM_EPS = 1e-6
SM_SCALE = HEAD_DIM ** -0.5
LOG2E = 1.4426950408889634
MOBA_SAFE_LOG2 = 60.0

LANES_V7X = 128
BF16_TILE_ROWS_V7X = 16
VMEM_BYTES_V7X = 64 * 1024 * 1024
VMEM_HEADROOM_BYTES = 4 * 1024 * 1024
VMEM_TEMPS_BYTES = 8 * 1024 * 1024
HEAD_PAIR = 2 * HEAD_DIM
assert HEAD_PAIR == LANES_V7X
MOBA_V_ROWS = HEAD_DIM + BF16_TILE_ROWS_V7X

NAT_WIDTH = MOBA_WIDTH + SWA_KV_WIDTH + 2 * MLSTM_WIDTH
KN_WIDTH = MOBA_WIDTH + SWA_KV_WIDTH
QT_ROWS = MOBA_WIDTH + SWA_Q_WIDTH
TR_ROWS = QT_ROWS + MOBA_WIDTH + SWA_KV_WIDTH + 2 * MLSTM_WIDTH + 2 * MLSTM_HEADS
NUM_GATES = 2 * MLSTM_HEADS

IN_PROJ_ROWS = 1024
OUT_MLP_ROWS = 1024
MLP_FF_CHUNK = 1024
SWA_ROWS = 512
MLSTM_ROWS = 256
MOBA_GROUP = 2
MOBA_Q_BLOCKS = 1
CONV_HALO = BF16_TILE_ROWS_V7X


def _dot(a, b):
    return jnp.dot(a, b, preferred_element_type=F32)


def _dot_nt(a, b):
    return lax.dot_general(a, b, (((1,), (1,)), ((), ())), preferred_element_type=F32)


def _split_bf16(v):
    hi = v.astype(BF16)
    lo = (v - hi.astype(F32)).astype(BF16)
    return hi, lo


def _split3_bf16(v):
    hi = v.astype(BF16)
    rest = v - hi.astype(F32)
    mid = rest.astype(BF16)
    return hi, mid, (rest - mid.astype(F32)).astype(BF16)


def _vmem_limit(nbytes):
    return int(min(nbytes + VMEM_TEMPS_BYTES, VMEM_BYTES_V7X - VMEM_HEADROOM_BYTES))


def _const_spec(shape):
    nd = len(shape)
    return pl.BlockSpec(shape, lambda *_: (0,) * nd, pipeline_mode=pl.Buffered(1))


def _in_proj_kernel(x_ref, ln1_ref, wnat_ref, wtr_ref, bd_ref, gk_ref, cosn_ref, sinn_ref,
                    gq_ref, cost_ref, sint_ref, gbias_ref,
                    mk_ref, kmean_ref, sk_ref, xqk_ref, mqt_ref, sqt_ref, mvt_ref, svt_ref,
                    xvt_ref, xot_ref, gt_ref):
    tm = x_ref.shape[0]
    x = x_ref[...]
    ms = jnp.mean(x * x, axis=-1, keepdims=True)
    hn = (x * lax.rsqrt(ms + NORM_EPS) * ln1_ref[...]).astype(BF16)
    nat = _dot(hn, wnat_ref[...])

    kk = nat[:, :KN_WIDTH]
    hi, lo = _split_bf16(kk * kk)
    bd = bd_ref[...]
    bw = bd.shape[0]
    msk = jnp.concatenate(
        [_dot(hi[:, c:c + bw], bd) + _dot(lo[:, c:c + bw], bd) for c in range(0, KN_WIDTH, bw)],
        axis=1)
    kn = kk * lax.rsqrt(msk + NORM_EPS) * gk_ref[...]
    reps = KN_WIDTH // LANES_V7X
    cosn = jnp.concatenate([cosn_ref[...]] * reps, axis=1)
    sinn = jnp.concatenate([sinn_ref[...]] * reps, axis=1)
    lane = lax.broadcasted_iota(jnp.int32, kn.shape, 1)
    first_half = (lane % HEAD_DIM) < (HEAD_DIM // 2)
    swapped = jnp.where(first_half,
                        pltpu.roll(kn, KN_WIDTH - HEAD_DIM // 2, 1),
                        pltpu.roll(kn, HEAD_DIM // 2, 1))
    kr = kn * cosn + swapped * sinn
    mk = kr[:, :MOBA_WIDTH]
    mk_ref[...] = mk.astype(BF16)
    nblk = tm // MOBA_BLOCK
    kmean_ref[0] = jnp.concatenate(
        [jnp.mean(mk[c * MOBA_BLOCK:(c + 1) * MOBA_BLOCK], axis=0, keepdims=True) for c in range(nblk)],
        axis=0)
    sk_ref[...] = kr[:, MOBA_WIDTH:].astype(BF16)
    xqk_ref[...] = nat[:, KN_WIDTH:].astype(BF16)

    tr = _dot_nt(wtr_ref[...], hn)
    cost = cost_ref[...]
    sint = sint_ref[...]
    gq = jnp.concatenate([gq_ref[...]] * (tm // LANES_V7X), axis=1)
    half = HEAD_DIM // 2
    for h in range(QT_ROWS // HEAD_DIM):
        blk = tr[h * HEAD_DIM:(h + 1) * HEAD_DIM]
        is_swa = h >= MOBA_HEADS
        gain = gq[HEAD_DIM:] if is_swa else gq[:HEAD_DIM]
        msq = jnp.mean(blk * blk, axis=0, keepdims=True)
        qn = blk * lax.rsqrt(msq + NORM_EPS) * gain
        x1, x2 = qn[:half], qn[half:]
        rot = jnp.concatenate([x1 * cost - x2 * sint, x2 * cost + x1 * sint], axis=0).astype(BF16)
        if is_swa:
            r0 = (h - MOBA_HEADS) * HEAD_DIM
            sqt_ref[r0:r0 + HEAD_DIM, :] = rot
        else:
            mqt_ref[h * HEAD_DIM:(h + 1) * HEAD_DIM, :] = rot
    r = QT_ROWS
    mv = tr[r:r + MOBA_WIDTH].astype(BF16)
    ones = jnp.ones((MOBA_V_ROWS - HEAD_DIM, MOBA_BLOCK), BF16)
    for c in range(nblk):
        for h in range(MOBA_HEADS):
            r0 = h * MOBA_V_ROWS
            mvt_ref[c, r0:r0 + HEAD_DIM, :] = mv[h * HEAD_DIM:(h + 1) * HEAD_DIM,
                                                 c * MOBA_BLOCK:(c + 1) * MOBA_BLOCK]
            mvt_ref[c, r0 + HEAD_DIM:r0 + MOBA_V_ROWS, :] = ones
    r += MOBA_WIDTH
    svt_ref[...] = tr[r:r + SWA_KV_WIDTH].astype(BF16)
    r += SWA_KV_WIDTH
    xvt_ref[...] = tr[r:r + MLSTM_WIDTH].astype(BF16)
    r += MLSTM_WIDTH
    xot_ref[...] = tr[r:r + MLSTM_WIDTH].astype(BF16)
    r += MLSTM_WIDTH
    gbias = jnp.concatenate([gbias_ref[...]] * (tm // LANES_V7X), axis=1)
    gt_ref[...] = tr[r:r + NUM_GATES] + gbias


def _in_proj(xf, ln1, wnat, wtr, bd, gk, cosn, sinn, gq, cost, sint, gbias, seq):
    t = xf.shape[0]
    tm = IN_PROJ_ROWS
    steps = t // tm
    seq_steps = seq // tm
    nblk = tm // MOBA_BLOCK
    row = lambda w: pl.BlockSpec((tm, w), lambda i: (i, 0))
    col = lambda r: pl.BlockSpec((r, tm), lambda i: (0, i))
    in_specs = [
        row(D_MODEL),
        _const_spec((1, D_MODEL)),
        _const_spec(wnat.shape),
        _const_spec(wtr.shape),
        _const_spec(bd.shape),
        _const_spec(gk.shape),
        pl.BlockSpec((tm, LANES_V7X), lambda i: (i % seq_steps, 0)),
        pl.BlockSpec((tm, LANES_V7X), lambda i: (i % seq_steps, 0)),
        _const_spec(gq.shape),
        pl.BlockSpec((HEAD_DIM // 2, tm), lambda i: (0, i % seq_steps)),
        pl.BlockSpec((HEAD_DIM // 2, tm), lambda i: (0, i % seq_steps)),
        _const_spec(gbias.shape),
    ]
    out_shape = [
        jax.ShapeDtypeStruct((t, MOBA_WIDTH), BF16),
        jax.ShapeDtypeStruct((steps, nblk, MOBA_WIDTH), F32),
        jax.ShapeDtypeStruct((t, SWA_KV_WIDTH), BF16),
        jax.ShapeDtypeStruct((t, 2 * MLSTM_WIDTH), BF16),
        jax.ShapeDtypeStruct((MOBA_WIDTH, t), BF16),
        jax.ShapeDtypeStruct((SWA_Q_WIDTH, t), BF16),
        jax.ShapeDtypeStruct((t // MOBA_BLOCK, MOBA_HEADS * MOBA_V_ROWS, MOBA_BLOCK), BF16),
        jax.ShapeDtypeStruct((SWA_KV_WIDTH, t), BF16),
        jax.ShapeDtypeStruct((MLSTM_WIDTH, t), BF16),
        jax.ShapeDtypeStruct((MLSTM_WIDTH, t), BF16),
        jax.ShapeDtypeStruct((NUM_GATES, t), F32),
    ]
    out_specs = [
        row(MOBA_WIDTH),
        pl.BlockSpec((1, nblk, MOBA_WIDTH), lambda i: (i, 0, 0)),
        row(SWA_KV_WIDTH),
        row(2 * MLSTM_WIDTH),
        col(MOBA_WIDTH),
        col(SWA_Q_WIDTH),
        pl.BlockSpec((nblk, MOBA_HEADS * MOBA_V_ROWS, MOBA_BLOCK), lambda i: (i, 0, 0)),
        col(SWA_KV_WIDTH),
        col(MLSTM_WIDTH),
        col(MLSTM_WIDTH),
        col(NUM_GATES),
    ]
    weights = 2 * (wnat.size + wtr.size)
    tiles = tm * (2 * 4 * D_MODEL + 3 * 4 * (NAT_WIDTH + TR_ROWS) + 2 * 2 * (NAT_WIDTH + TR_ROWS))
    return pl.pallas_call(
        _in_proj_kernel,
        grid=(steps,),
        in_specs=in_specs,
        out_specs=out_specs,
        out_shape=out_shape,
        compiler_params=pltpu.CompilerParams(
            dimension_semantics=("arbitrary",),
            vmem_limit_bytes=_vmem_limit(2 * weights + tiles)),
        name="in_proj",
    )(xf, ln1, wnat, wtr, bd, gk, cosn, sinn, gq, cost, sint, gbias)


def _moba_kernel(bounded_ref, qt_ref, k_ref, vt_ref, km_ref, o_ref, sel_ref, s_ref, p_ref):
    nb = k_ref.shape[0]
    lq = MOBA_BLOCK
    nh = MOBA_Q_BLOCKS * MOBA_HEADS
    step = pl.program_id(1)
    row = lax.broadcasted_iota(jnp.int32, (HEAD_PAIR, lq), 0)
    own, qh = [], []
    for s in range(nh):
        qb, h = divmod(s, MOBA_HEADS)
        own.append(step * MOBA_Q_BLOCKS + qb)
        qt = qt_ref[(h // 2) * HEAD_PAIR:(h // 2 + 1) * HEAD_PAIR, qb * lq:(qb + 1) * lq]
        keep = (row < HEAD_DIM) if h % 2 == 0 else (row >= HEAD_DIM)
        qh.append(jnp.where(keep, qt, jnp.zeros_like(qt)))
    last_own = step * MOBA_Q_BLOCKS + (MOBA_Q_BLOCKS - 1)

    def k_group(blk0, count, s):
        g = (s % MOBA_HEADS) // 2
        return k_ref[pl.ds(blk0, count), :, g * HEAD_PAIR:(g + 1) * HEAD_PAIR].reshape(count * lq, HEAD_PAIR)

    def v_aug(j, s):
        h = s % MOBA_HEADS
        return vt_ref[j, h * MOBA_V_ROWS:(h + 1) * MOBA_V_ROWS, :]

    def select_blocks():
        gates = []
        for s in range(nh):
            g = (s % MOBA_HEADS) // 2
            km_hi, km_lo = _split_bf16(km_ref[:, g * HEAD_PAIR:(g + 1) * HEAD_PAIR])
            gates.append(_dot(km_hi, qh[s]) + _dot(km_lo, qh[s]))
        gate = jnp.concatenate(gates, axis=1)
        own_all = jnp.concatenate([jnp.full((1, lq), own[s], jnp.int32) for s in range(nh)], axis=1)
        blk_all = lax.broadcasted_iota(jnp.int32, gate.shape, 0)
        gate = jnp.where(blk_all < own_all, gate, NEG_INF)
        sel = jnp.full(gate.shape, NEG_INF, F32)
        for _ in range(MOBA_TOPK):
            mx = jnp.max(gate, axis=0, keepdims=True)
            idx = jnp.min(jnp.where(gate == mx, blk_all, nb), axis=0, keepdims=True)
            pick = blk_all == jnp.where(idx < own_all, idx, nb)
            sel = jnp.where(pick, 0.0, sel)
            gate = jnp.where(pick, NEG_INF, gate)
        for s in range(nh):
            sel_ref[s] = sel[:, s * lq:(s + 1) * lq]

    kpos = lax.broadcasted_iota(jnp.int32, (lq, lq), 0)
    qpos = lax.broadcasted_iota(jnp.int32, (lq, lq), 1)
    causal = kpos <= qpos

    def weighted_values(st, mu, v_rows):
        pv = _dot(v_rows, jnp.exp2(st - mu).astype(BF16))
        return pv[HEAD_DIM:HEAD_DIM + 1], pv[:HEAD_DIM]

    def finish(nums, dens):
        for s in range(0, nh, 2):
            qb, h = divmod(s, MOBA_HEADS)
            ot = jnp.concatenate([nums[s] / dens[s], nums[s + 1] / dens[s + 1]], axis=0)
            o_ref[qb * lq:(qb + 1) * lq, (h // 2) * HEAD_PAIR:(h // 2 + 1) * HEAD_PAIR] = ot.T.astype(BF16)

    trips = lax.shift_right_logical(last_own + (2 * MOBA_GROUP - 1), MOBA_GROUP.bit_length())
    units = nh * MOBA_GROUP
    own_slot = 2 * units

    @pl.when(bounded_ref[0] != 0)
    def _():
        select_blocks()
        for h in range(nh):
            st = jnp.where(causal, _dot(k_group(own[h], 1, h), qh[h]), NEG_INF)
            p_ref[own_slot + h] = jnp.exp2(st).astype(BF16)

        def stage_p(blk0, slot0):
            blk0 = jnp.minimum(blk0, nb - MOBA_GROUP)
            for h in range(nh):
                st = _dot(k_group(blk0, MOBA_GROUP, h), qh[h])
                sel = jnp.concatenate(
                    [jnp.broadcast_to(sel_ref[h, pl.ds(blk0 + u, 1), :], (lq, lq)) for u in range(MOBA_GROUP)],
                    axis=0)
                slot = slot0 + MOBA_GROUP * h
                p_ref[slot:slot + MOBA_GROUP] = jnp.exp2(st + sel).astype(BF16).reshape(MOBA_GROUP, lq, lq)

        def stage_v(blk0, slot0, acc):
            acc = list(acc)
            for h in range(nh):
                slot = slot0 + MOBA_GROUP * h
                p = p_ref[slot:slot + MOBA_GROUP].reshape(MOBA_GROUP * lq, lq)
                v = jnp.concatenate([v_aug(blk0 + u, h) for u in range(MOBA_GROUP)], axis=1)
                acc[h] = acc[h] + _dot(v, p)
            return acc

        def trip(t, acc, last):
            blk = t * (2 * MOBA_GROUP)
            stage_p(blk + MOBA_GROUP, units)
            acc = stage_v(blk, 0, acc)
            if not last:
                stage_p(blk + 2 * MOBA_GROUP, 0)
            return tuple(stage_v(blk + MOBA_GROUP, units, acc))

        stage_p(0, 0)
        acc = tuple(_dot(v_aug(own[h], h), p_ref[own_slot + h]) for h in range(nh))
        acc = lax.fori_loop(0, jnp.maximum(trips - 1, 0), lambda t, a: trip(t, a, False), acc)
        acc = lax.cond(trips > 0, lambda a: trip(trips - 1, a, True), lambda a: a, acc)
        finish([a[:HEAD_DIM] for a in acc], [a[HEAD_DIM:HEAD_DIM + 1] for a in acc])


    def stage_a(blk0, slot0):
        mus = []
        for u in range(MOBA_GROUP):
            for h in range(nh):
                st = _dot(k_group(jnp.minimum(blk0 + u, nb - 1), 1, h), qh[h])
                s_ref[slot0 + nh * u + h] = st
                mus.append(jnp.max(st, axis=0, keepdims=True))
        return mus

    def stage_b(blk0, slot0, mus, state):
        state = list(state)
        for u in range(MOBA_GROUP):
            j = blk0 + u
            for h in range(nh):
                mu = mus[nh * u + h]
                ls, pv = weighted_values(s_ref[slot0 + nh * u + h], mu, v_aug(j, h))
                m, l, acc = state[3 * h:3 * h + 3]
                mu_sel = mu + sel_ref[h, pl.ds(j, 1), :]
                m_new = jnp.maximum(m, mu_sel)
                f = jnp.exp2(mu_sel - m_new)
                c = jnp.exp2(m - m_new)
                state[3 * h:3 * h + 3] = [m_new, c * l + f * ls, c * acc + f * pv]
        return state

    def body(t, loop_carry):
        mus0, state = loop_carry[:units], loop_carry[units:]
        blk = t * (2 * MOBA_GROUP)
        mus1 = stage_a(blk + MOBA_GROUP, units)
        state = stage_b(blk, 0, mus0, state)
        mus0 = stage_a(blk + 2 * MOBA_GROUP, 0)
        state = stage_b(blk + MOBA_GROUP, units, mus1, state)
        return (*mus0, *state)

    @pl.when(bounded_ref[0] == 0)
    def _():
        state = []
        for h in range(nh):
            st = jnp.where(causal, _dot(k_group(own[h], 1, h), qh[h]), NEG_INF)
            mu = jnp.max(st, axis=0, keepdims=True)
            state += [mu, *weighted_values(st, mu, v_aug(own[h], h))]
        mus0 = stage_a(0, 0)
        select_blocks()
        state = lax.fori_loop(0, trips, body, (*mus0, *state))[units:]
        finish(state[2::3], state[1::3])


def _moba(bounded, mqt, mk, mvt, kmean, batch, seq):
    t = mk.shape[0]
    nb = seq // MOBA_BLOCK
    lq = MOBA_BLOCK
    k3 = mk.reshape(t // MOBA_BLOCK, MOBA_BLOCK, MOBA_WIDTH)
    assert nb % (2 * MOBA_GROUP) == 0 and nb % MOBA_Q_BLOCKS == 0
    streams = MOBA_Q_BLOCKS * MOBA_HEADS
    units = streams * MOBA_GROUP
    steps = nb // MOBA_Q_BLOCKS
    kv_bytes = 2 * nb * MOBA_BLOCK * (MOBA_WIDTH + MOBA_HEADS * MOBA_V_ROWS)
    scratch_bytes = lq * lq * (4 * 2 * units + 2 * (2 * units + streams)) + 4 * streams * nb * lq
    resident = lambda shape, imap: pl.BlockSpec(shape, imap, pipeline_mode=pl.Buffered(1))
    return pl.pallas_call(
        _moba_kernel,
        grid=(batch, steps),
        in_specs=[
            pl.BlockSpec(memory_space=pltpu.SMEM),
            pl.BlockSpec((MOBA_WIDTH, MOBA_Q_BLOCKS * lq), lambda b, i: (0, b * steps + i)),
            resident((nb, MOBA_BLOCK, MOBA_WIDTH), lambda b, i: (b, 0, 0)),
            resident((nb, MOBA_HEADS * MOBA_V_ROWS, MOBA_BLOCK), lambda b, i: (b, 0, 0)),
            resident((nb, MOBA_WIDTH), lambda b, i: (b, 0)),
        ],
        out_specs=pl.BlockSpec((MOBA_Q_BLOCKS * lq, MOBA_WIDTH), lambda b, i: (b * steps + i, 0)),
        out_shape=jax.ShapeDtypeStruct((t, MOBA_WIDTH), BF16),
        scratch_shapes=[pltpu.VMEM((streams, nb, lq), F32),
                        pltpu.VMEM((2 * units, lq, lq), F32),
                        pltpu.VMEM((2 * units + streams, lq, lq), BF16)],
        compiler_params=pltpu.CompilerParams(
            dimension_semantics=("arbitrary", "arbitrary"),
            vmem_limit_bytes=_vmem_limit(kv_bytes + scratch_bytes + VMEM_HEADROOM_BYTES)),
        name="moba",
    )(bounded, mqt, k3, mvt, kmean)


def _swa_kernel(bounded_ref, sink_ref, qt_ref, k_ref, kh_ref, vt_ref, vth_ref, o_ref, p_ref):
    tq = k_ref.shape[0]
    w = SWA_WINDOW
    i = pl.program_id(1)
    ncol = SWA_GROUP * w
    kpos = lax.broadcasted_iota(jnp.int32, (2 * w, ncol), 0)
    col = lax.broadcasted_iota(jnp.int32, (2 * w, ncol), 1)
    diff = w + (col % w) - kpos
    in_window = (diff >= 0) & (diff < w)
    zeros = jnp.zeros((HEAD_DIM, ncol), BF16)
    ones = jnp.ones((MOBA_V_ROWS - HEAD_DIM, 2 * w), BF16)

    def attend(shifted):
        sink_terms = []
        for r in range(tq // w):
            if r == 0:
                kcat = jnp.concatenate([kh_ref[...], k_ref[0:w, :]], axis=0)
                mask = in_window & ((kpos >= w) | (i > 0))
            else:
                kcat = k_ref[(r - 1) * w:(r + 1) * w, :]
                mask = in_window
            for g in range(SWA_KV_HEADS):
                qs = jnp.concatenate(
                    [qt_ref[(SWA_GROUP * g + a) * HEAD_DIM:(SWA_GROUP * g + a + 1) * HEAD_DIM, r * w:(r + 1) * w]
                     for a in range(SWA_GROUP)], axis=1)
                qpad = jnp.concatenate([qs, zeros] if g == 0 else [zeros, qs], axis=0)
                st = jnp.where(mask, _dot(kcat, qpad), NEG_INF)
                sink = jnp.concatenate(
                    [jnp.full((1, w), sink_ref[SWA_GROUP * g + a] * LOG2E, F32) for a in range(SWA_GROUP)],
                    axis=1)
                if shifted:
                    m = jnp.maximum(jnp.max(st, axis=0, keepdims=True), sink)
                    st, sink = st - m, sink - m
                p_ref[r * SWA_KV_HEADS + g] = jnp.exp2(st).astype(BF16)
                sink_terms.append(jnp.exp2(sink))
        for r in range(tq // w):
            if r == 0:
                vcat = jnp.concatenate([vth_ref[...], vt_ref[:, 0:w]], axis=1)
            else:
                vcat = vt_ref[:, (r - 1) * w:(r + 1) * w]
            pieces = []
            for g in range(SWA_KV_HEADS):
                v_aug = jnp.concatenate([vcat[g * HEAD_DIM:(g + 1) * HEAD_DIM, :], ones], axis=0)
                pv = _dot(v_aug, p_ref[r * SWA_KV_HEADS + g])
                ot = pv[:HEAD_DIM] / (pv[HEAD_DIM:HEAD_DIM + 1] + sink_terms[r * SWA_KV_HEADS + g])
                pieces += [ot[:, a * w:(a + 1) * w] for a in range(SWA_GROUP)]
            o_ref[r * w:(r + 1) * w, :] = jnp.concatenate(pieces, axis=0).T.astype(BF16)

    @pl.when(bounded_ref[0] != 0)
    def _():
        attend(False)

    @pl.when(bounded_ref[0] == 0)
    def _():
        attend(True)


def _swa(bounded, sinks, sqt, sk, svt, batch, seq):
    t = sk.shape[0]
    tq = SWA_ROWS
    w = SWA_WINDOW
    steps = seq // tq
    halo = lambda b, i: jnp.maximum((b * seq + i * tq) // w - 1, 0)
    return pl.pallas_call(
        _swa_kernel,
        grid=(batch, steps),
        in_specs=[
            pl.BlockSpec(memory_space=pltpu.SMEM),
            pl.BlockSpec(memory_space=pltpu.SMEM),
            pl.BlockSpec((SWA_Q_WIDTH, tq), lambda b, i: (0, b * steps + i)),
            pl.BlockSpec((tq, SWA_KV_WIDTH), lambda b, i: (b * steps + i, 0)),
            pl.BlockSpec((w, SWA_KV_WIDTH), lambda b, i: (halo(b, i), 0)),
            pl.BlockSpec((SWA_KV_WIDTH, tq), lambda b, i: (0, b * steps + i)),
            pl.BlockSpec((SWA_KV_WIDTH, w), lambda b, i: (0, halo(b, i))),
        ],
        out_specs=pl.BlockSpec((tq, SWA_Q_WIDTH), lambda b, i: (b * steps + i, 0)),
        out_shape=jax.ShapeDtypeStruct((t, SWA_Q_WIDTH), BF16),
        scratch_shapes=[pltpu.VMEM(((tq // w) * SWA_KV_HEADS, 2 * w, SWA_GROUP * w), BF16)],
        compiler_params=pltpu.CompilerParams(dimension_semantics=("arbitrary", "arbitrary")),
        name="swa",
    )(bounded, sinks, sqt, sk, sk, svt, svt)


def _log_sigmoid(v):
    return jnp.minimum(v, 0.0) - jnp.log1p(jnp.exp(-jnp.abs(v)))


def _mlstm_kernel(xqk_ref, halo_ref, xvt_ref, xot_ref, gt_ref, gc_ref, convw_ref, convb_ref,
                  gain_ref, cum_ref, cumt_ref, o_ref, ext_ref, c_ref, n_ref, m_ref,
                  qm_ref, k_ref, pv_ref, row_ref, kv_ref, nk_ref, cprev_ref, nprev_ref, ht_ref):
    tt = xqk_ref.shape[0]
    lc = MLSTM_CHUNK
    i = pl.program_id(1)

    @pl.when(i == 0)
    def _():
        c_ref[...] = jnp.zeros_like(c_ref)
        n_ref[...] = jnp.zeros_like(n_ref)
        m_ref[...] = jnp.zeros_like(m_ref)

    hist = halo_ref[...].astype(F32)
    ext_ref[0:CONV_HALO, :] = jnp.where(i > 0, hist, jnp.zeros_like(hist))
    ext_ref[CONV_HALO:CONV_HALO + tt, :] = xqk_ref[...].astype(F32)
    convw = convw_ref[...]
    conv = jnp.zeros((tt, 2 * MLSTM_WIDTH), F32) + convb_ref[...]
    for j in range(CONV_WIDTH):
        start = CONV_HALO - (CONV_WIDTH - 1) + j
        conv = conv + convw[j:j + 1, :] * ext_ref[start:start + tt, :]
    qk = conv * jax.nn.sigmoid(conv)
    q_all = qk[:, :MLSTM_WIDTH]
    lane_q = lax.broadcasted_iota(jnp.int32, q_all.shape, 1)
    even_head = (lane_q % HEAD_PAIR) < HEAD_DIM
    qm_ref[0] = jnp.where(even_head, q_all, 0.0).astype(BF16)
    qm_ref[1] = jnp.where(even_head, 0.0, q_all).astype(BF16)
    k_ref[...] = (qk[:, MLSTM_WIDTH:] * SM_SCALE).astype(BF16)

    g_row = gt_ref[...]
    g_col = gc_ref[...]
    cum = cum_ref[...]
    cumt = cumt_ref[...]
    b_row = sum(_dot(piece, cum) for piece in _split3_bf16(_log_sigmoid(g_row)))
    b_col = sum(_dot(cumt, piece) for piece in _split3_bf16(_log_sigmoid(g_col)))
    u_col = b_col[:, MLSTM_HEADS:] - g_col[:, :MLSTM_HEADS]

    lane = lax.broadcasted_iota(jnp.int32, (HEAD_DIM, HEAD_PAIR), 1)
    s_idx = lax.broadcasted_iota(jnp.int32, (lc, lc), 0)
    t_idx = lax.broadcasted_iota(jnp.int32, (lc, lc), 1)
    causal = s_idx <= t_idx
    nchunks = tt // lc
    heads = range(MLSTM_HEADS)
    in_head = [(lane < HEAD_DIM), (lane >= HEAD_DIM)]

    def operands(c, h):
        p, hh = divmod(h, 2)
        sl = slice(c * lc, (c + 1) * lc)
        qm = qm_ref[hh, sl, p * HEAD_PAIR:(p + 1) * HEAD_PAIR]
        kp = k_ref[sl, p * HEAD_PAIR:(p + 1) * HEAD_PAIR]
        return sl, hh, qm, kp

    m_state = [m_ref[h][0:1, 0:lc] for h in heads]
    for c in range(nchunks):
        for h in heads:
            u = c * MLSTM_HEADS + h
            sl, hh, qm, kp = operands(c, h)
            vt = xvt_ref[h * HEAD_DIM:(h + 1) * HEAD_DIM, sl]
            br = b_row[MLSTM_HEADS + h:MLSTM_HEADS + h + 1, sl]
            ir = g_row[h:h + 1, sl]
            uc = u_col[sl, h:h + 1]
            a = br[:, lc - 1:lc]
            dt = jnp.where(causal, br - uc, NEG_INF)
            inter = br + m_state[h]
            m_t = jnp.maximum(inter, jnp.max(dt, axis=0, keepdims=True))
            w_inter = jnp.exp(inter - m_t)
            qkt = _dot_nt(kp, qm) * jnp.exp(dt - m_t)
            pv_ref[u] = _dot(vt, qkt.astype(BF16))
            row_ref[u, 0:1, 0:lc] = w_inter
            row_ref[u, 1:2, 0:lc] = jnp.sum(qkt, axis=0, keepdims=True)
            row_ref[u, 2:3, 0:lc] = jnp.exp(-m_t)
            g_end = a - br + ir
            m_new = jnp.maximum(a + m_state[h], jnp.max(g_end, axis=1, keepdims=True))
            w_s = jnp.exp(g_end - m_new)
            decay = jnp.exp(a + m_state[h] - m_new)
            row_ref[u, 3:4, 0:lc] = decay
            vtw = (vt.astype(F32) * w_s).astype(BF16)
            kv_ref[u] = jnp.where(in_head[hh], _dot(vtw, kp), 0.0)
            w_rows = jnp.broadcast_to(w_s, (8, lc)).astype(BF16)
            nk_ref[u] = jnp.where(in_head[hh][0:8], _dot(w_rows, kp), 0.0)
            m_state[h] = m_new

    for h in heads:
        c_state = c_ref[h]
        n_state = n_ref[h]
        for c in range(nchunks):
            u = c * MLSTM_HEADS + h
            cprev_ref[u] = c_state.astype(BF16)
            nprev_ref[u] = n_state
            decay = row_ref[u, 3:4, 0:HEAD_PAIR]
            c_state = decay * c_state + kv_ref[u]
            n_state = decay * n_state + nk_ref[u]
        c_ref[h] = c_state
        n_ref[h] = n_state
        m_ref[h] = jnp.broadcast_to(m_state[h][:, 0:1], m_ref.shape[1:])

    for c in range(nchunks):
        for h in heads:
            u = c * MLSTM_HEADS + h
            sl, hh, qm, _ = operands(c, h)
            w_inter = row_ref[u, 0:1, 0:lc]
            num = w_inter * _dot_nt(cprev_ref[u], qm) + pv_ref[u]
            den = w_inter * _dot_nt(nprev_ref[u].astype(BF16), qm)[0:1, :] + row_ref[u, 1:2, 0:lc]
            ht_ref[h * HEAD_DIM:(h + 1) * HEAD_DIM, sl] = num / jnp.maximum(jnp.abs(den), row_ref[u, 2:3, 0:lc])

    gain = jnp.concatenate([gain_ref[...]] * (tt // LANES_V7X), axis=1)
    outs = []
    for h in heads:
        rows = slice(h * HEAD_DIM, (h + 1) * HEAD_DIM)
        hg = ht_ref[rows, :] * jax.nn.sigmoid(xot_ref[rows, :].astype(F32))
        msq = jnp.mean(hg * hg, axis=0, keepdims=True)
        outs.append(hg * lax.rsqrt(msq + NORM_EPS) * gain[rows])
    o_ref[...] = jnp.concatenate(outs, axis=0).T.astype(BF16)


def _mlstm(xqk, xvt, xot, gt, gc, convw, convb, gain, cum, batch, seq):
    t = xqk.shape[0]
    tt = MLSTM_ROWS
    steps = seq // tt
    units = MLSTM_HEADS * (tt // MLSTM_CHUNK)
    col = lambda r: pl.BlockSpec((r, tt), lambda b, i: (0, b * steps + i))
    return pl.pallas_call(
        _mlstm_kernel,
        grid=(batch, steps),
        in_specs=[
            pl.BlockSpec((tt, 2 * MLSTM_WIDTH), lambda b, i: (b * steps + i, 0)),
            pl.BlockSpec((CONV_HALO, 2 * MLSTM_WIDTH),
                         lambda b, i: (jnp.maximum((b * seq + i * tt) // CONV_HALO - 1, 0), 0)),
            col(MLSTM_WIDTH),
            col(MLSTM_WIDTH),
            col(NUM_GATES),
            pl.BlockSpec((tt, NUM_GATES), lambda b, i: (b * steps + i, 0)),
            _const_spec(convw.shape),
            _const_spec(convb.shape),
            _const_spec(gain.shape),
            _const_spec(cum.shape),
            _const_spec(cum.shape),
        ],
        out_specs=pl.BlockSpec((tt, MLSTM_WIDTH), lambda b, i: (b * steps + i, 0)),
        out_shape=jax.ShapeDtypeStruct((t, MLSTM_WIDTH), BF16),
        scratch_shapes=[
            pltpu.VMEM((CONV_HALO + tt, 2 * MLSTM_WIDTH), F32),
            pltpu.VMEM((MLSTM_HEADS, HEAD_DIM, HEAD_PAIR), F32),
            pltpu.VMEM((MLSTM_HEADS, 8, HEAD_PAIR), F32),
            pltpu.VMEM((MLSTM_HEADS, 8, MLSTM_CHUNK), F32),
            pltpu.VMEM((2, tt, MLSTM_WIDTH), BF16),
            pltpu.VMEM((tt, MLSTM_WIDTH), BF16),
            pltpu.VMEM((units, HEAD_DIM, MLSTM_CHUNK), F32),
            pltpu.VMEM((units, 8, MLSTM_CHUNK), F32),
            pltpu.VMEM((units, HEAD_DIM, HEAD_PAIR), F32),
            pltpu.VMEM((units, 8, HEAD_PAIR), F32),
            pltpu.VMEM((units, HEAD_DIM, HEAD_PAIR), BF16),
            pltpu.VMEM((units, 8, HEAD_PAIR), F32),
            pltpu.VMEM((MLSTM_WIDTH, tt), F32),
        ],
        compiler_params=pltpu.CompilerParams(dimension_semantics=("arbitrary", "arbitrary")),
        name="mlstm",
    )(xqk, xqk, xvt, xot, gt, gc, convw, convb, gain, cum, cum.T)


def _out_mlp_kernel(x_ref, ym_ref, yl_ref, ys_ref, wo_ref, ln2_ref, wup_ref, wdn_ref, o_ref):
    y = jnp.concatenate([ym_ref[...], yl_ref[...], ys_ref[...]], axis=1)
    x1 = x_ref[...] + _dot(y, wo_ref[...])
    ms = jnp.mean(x1 * x1, axis=-1, keepdims=True)
    hn = (x1 * lax.rsqrt(ms + NORM_EPS) * ln2_ref[...]).astype(BF16)
    o_ref[...] = x1
    for c in range(0, D_FF, MLP_FF_CHUNK):
        u = _dot(hn, wup_ref[:, c:c + MLP_FF_CHUNK])
        act = jnp.square(jnp.maximum(u, 0.0)).astype(BF16)
        o_ref[...] += _dot(act, wdn_ref[c:c + MLP_FF_CHUNK, :])


def _out_mlp(xf, ym, yl, ys, wo, ln2, wup, wdn):
    t = xf.shape[0]
    tm = OUT_MLP_ROWS
    row = lambda w: pl.BlockSpec((tm, w), lambda i: (i, 0))
    weights = 2 * (wo.size + wup.size + wdn.size)
    tiles = tm * (4 * 4 * D_MODEL + 2 * 2 * D_MODEL + 4 * 4 * D_MODEL + 6 * MLP_FF_CHUNK)
    return pl.pallas_call(
        _out_mlp_kernel,
        grid=(t // tm,),
        in_specs=[
            row(D_MODEL), row(MOBA_WIDTH), row(MLSTM_WIDTH), row(SWA_Q_WIDTH),
            _const_spec(wo.shape), _const_spec(ln2.shape), _const_spec(wup.shape), _const_spec(wdn.shape),
        ],
        out_specs=row(D_MODEL),
        out_shape=jax.ShapeDtypeStruct((t, D_MODEL), F32),
        compiler_params=pltpu.CompilerParams(
            dimension_semantics=("arbitrary",),
            vmem_limit_bytes=_vmem_limit(weights + tiles)),
        name="out_mlp",
    )(xf, ym, yl, ys, wo, ln2, wup, wdn)


def _rope_tables(seq):
    inv = ROPE_THETA ** (-jnp.arange(0, HEAD_DIM, 2, dtype=F32) / HEAD_DIM)
    ang = jnp.arange(seq, dtype=F32)[:, None] * inv[None, :]
    cos, sin = jnp.cos(ang), jnp.sin(ang)
    cosn = jnp.concatenate([cos, cos, cos, cos], axis=1)
    sinn = jnp.concatenate([-sin, sin, -sin, sin], axis=1)
    return cosn, sinn, cos.T, sin.T


def _layer(xf, tables, consts, batch, seq, ln1, w_in, conv_w, conv_b, igate_b, fgate_b, mlstm_norm,
           moba_q_norm, moba_k_norm, swa_q_norm, swa_k_norm, swa_sinks, w_out, ln2, w_up, w_down):
    cosn, sinn, cost, sint = tables
    bd, cum = consts
    o = 0
    cols = {}
    for name, width in (("mq", MOBA_WIDTH), ("mk", MOBA_WIDTH), ("mv", MOBA_WIDTH), ("sq", SWA_Q_WIDTH),
                        ("sk", SWA_KV_WIDTH), ("sv", SWA_KV_WIDTH), ("xqk", 2 * MLSTM_WIDTH),
                        ("xv", MLSTM_WIDTH), ("xo", MLSTM_WIDTH), ("xi", MLSTM_HEADS), ("xf", MLSTM_HEADS)):
        cols[name] = w_in[:, o:o + width]
        o += width
    wnat = jnp.concatenate([cols["mk"], cols["sk"], cols["xqk"]], axis=1).astype(BF16)
    wtr = jnp.concatenate([cols["mq"], cols["sq"], cols["mv"], cols["sv"], cols["xv"], cols["xo"],
                           cols["xi"], cols["xf"]], axis=1).T.astype(BF16)
    gk = jnp.concatenate([jnp.tile(moba_k_norm, MOBA_HEADS), jnp.tile(swa_k_norm, SWA_KV_HEADS)])[None, :]
    gq = jnp.broadcast_to(
        (jnp.concatenate([moba_q_norm, swa_q_norm]) * (SM_SCALE * LOG2E))[:, None], (2 * HEAD_DIM, LANES_V7X))
    gbias = jnp.broadcast_to(jnp.concatenate([igate_b, fgate_b])[:, None], (NUM_GATES, LANES_V7X))

    (mk, kmean, sk, xqk, mqt, sqt, mvt, svt, xvt, xot, gt) = _in_proj(
        xf, ln1[None, :], wnat, wtr, bd, gk, cosn, sinn, gq, cost, sint, gbias, seq)

    kmean = kmean.reshape(-1, MOBA_WIDTH)
    def bounded(q_gain, k_gain, *extra):
        bound = (HEAD_DIM * SM_SCALE * LOG2E) * jnp.max(jnp.abs(q_gain)) * jnp.max(jnp.abs(k_gain))
        for e in extra:
            bound = jnp.maximum(bound, jnp.max(jnp.abs(e)) * LOG2E)
        return (bound <= MOBA_SAFE_LOG2).astype(jnp.int32).reshape(1)

    ym = _moba(bounded(moba_q_norm, moba_k_norm), mqt, mk, mvt, kmean, batch, seq)
    ys = _swa(bounded(swa_q_norm, swa_k_norm, swa_sinks), swa_sinks, sqt, sk, svt, batch, seq)
    gain = jnp.broadcast_to(mlstm_norm.reshape(MLSTM_WIDTH, 1), (MLSTM_WIDTH, LANES_V7X))
    yl = _mlstm(xqk, xvt, xot, gt, gt.T, conv_w, conv_b[None, :], gain, cum, batch, seq)

    return _out_mlp(xf, ym, yl, ys, w_out.astype(BF16), ln2[None, :], w_up.astype(BF16), w_down.astype(BF16))


def kernel(x, ln1, w_in, conv_w, conv_b, igate_b, fgate_b, mlstm_norm, moba_q_norm, moba_k_norm,
           swa_q_norm, swa_k_norm, swa_sinks, w_out, ln2, w_up, w_down):
    batch, seq, d = x.shape
    assert d == D_MODEL and seq % max(IN_PROJ_ROWS, SWA_ROWS, MLSTM_ROWS, MOBA_BLOCK) == 0
    depth = ln1.shape[0]
    tables = _rope_tables(seq)
    bw = 2 * LANES_V7X
    ids = jnp.arange(bw) // HEAD_DIM
    bd = jnp.where(ids[:, None] == ids[None, :], 1.0 / HEAD_DIM, 0.0).astype(BF16)
    tids = jnp.arange(MLSTM_ROWS)
    cum = ((tids[:, None] // MLSTM_CHUNK == tids[None, :] // MLSTM_CHUNK)
           & (tids[:, None] <= tids[None, :])).astype(BF16)
    xf = x.reshape(batch * seq, d)
    for l in range(depth):
        xf = _layer(xf, tables, (bd, cum), batch, seq, ln1[l], w_in[l], conv_w[l], conv_b[l], igate_b[l],
                    fgate_b[l], mlstm_norm[l], moba_q_norm[l], moba_k_norm[l], swa_q_norm[l],
                    swa_k_norm[l], swa_sinks[l], w_out[l], ln2[l], w_up[l], w_down[l])
    return xf.reshape(batch, seq, d)
```

```python
import jax
import jax.numpy as jnp
from jax import lax
from jax.experimental import pallas as pl
from jax.experimental.pallas import tpu as pltpu

F32 = jnp.float32
BF16 = jnp.bfloat16
NEG_INF = float("-inf")

D_MODEL = 1024
HEAD_DIM = 64
MOBA_HEADS = 6
MLSTM_HEADS = 4
SWA_Q_HEADS = 6
SWA_KV_HEADS = 2
SWA_GROUP = SWA_Q_HEADS // SWA_KV_HEADS
MOBA_WIDTH = MOBA_HEADS * HEAD_DIM
MLSTM_WIDTH = MLSTM_HEADS * HEAD_DIM
SWA_Q_WIDTH = SWA_Q_HEADS * HEAD_DIM
SWA_KV_WIDTH = SWA_KV_HEADS * HEAD_DIM
MOBA_BLOCK = 256
MOBA_TOPK = 3
MLSTM_CHUNK = 256
CONV_WIDTH = 4
SWA_WINDOW = 128
ROPE_THETA = 10000.0
D_FF = 4 * D_MODEL
NORM_EPS = 1e-6
SM_SCALE = HEAD_DIM ** -0.5
LOG2E = 1.4426950408889634
MOBA_SAFE_LOG2 = 60.0

LANES_V7X = 128
BF16_TILE_ROWS_V7X = 16
VMEM_BYTES_V7X = 64 * 1024 * 1024
VMEM_HEADROOM_BYTES = 4 * 1024 * 1024
VMEM_TEMPS_BYTES = 8 * 1024 * 1024
HEAD_PAIR = 2 * HEAD_DIM
assert HEAD_PAIR == LANES_V7X
MOBA_V_ROWS = HEAD_DIM + BF16_TILE_ROWS_V7X

NAT_WIDTH = MOBA_WIDTH + SWA_KV_WIDTH + 2 * MLSTM_WIDTH
KN_WIDTH = MOBA_WIDTH + SWA_KV_WIDTH
QT_ROWS = MOBA_WIDTH + SWA_Q_WIDTH
TR_ROWS = QT_ROWS + MOBA_WIDTH + SWA_KV_WIDTH + 2 * MLSTM_WIDTH + 2 * MLSTM_HEADS
NUM_GATES = 2 * MLSTM_HEADS

IN_PROJ_ROWS = 1024
OUT_MLP_ROWS = 1024
MLP_FF_CHUNK = 1024
SWA_ROWS = 1024
MLSTM_ROWS = 256
MOBA_GROUP = 2
MOBA_Q_BLOCKS = 1
CONV_HALO = BF16_TILE_ROWS_V7X


def _dot(a, b):
    return jnp.dot(a, b, preferred_element_type=F32)


def _dot_nt(a, b):
    return lax.dot_general(a, b, (((1,), (1,)), ((), ())), preferred_element_type=F32)


def _split_bf16(v):
    hi = v.astype(BF16)
    lo = (v - hi.astype(F32)).astype(BF16)
    return hi, lo


def _split3_bf16(v):
    hi = v.astype(BF16)
    rest = v - hi.astype(F32)
    mid = rest.astype(BF16)
    return hi, mid, (rest - mid.astype(F32)).astype(BF16)


def _vmem_limit(nbytes):
    return int(min(nbytes + VMEM_TEMPS_BYTES, VMEM_BYTES_V7X - VMEM_HEADROOM_BYTES))


def _const_spec(shape):
    nd = len(shape)
    return pl.BlockSpec(shape, lambda *_: (0,) * nd, pipeline_mode=pl.Buffered(1))


def _in_proj_kernel(x_ref, ln1_ref, wnat_ref, wtr_ref, bd_ref, gk_ref, cosn_ref, sinn_ref,
                    gq_ref, cost_ref, sint_ref, gbias_ref,
                    mk_ref, kmean_ref, sk_ref, xqk_ref, mqt_ref, sqt_ref, mvt_ref, svt_ref,
                    xvt_ref, xot_ref, gt_ref):
    tm = x_ref.shape[0]
    x = x_ref[...]
    ms = jnp.mean(x * x, axis=-1, keepdims=True)
    hn = (x * lax.rsqrt(ms + NORM_EPS) * ln1_ref[...]).astype(BF16)
    nat = _dot(hn, wnat_ref[...])

    kk = nat[:, :KN_WIDTH]
    hi, lo = _split_bf16(kk * kk)
    bd = bd_ref[...]
    bw = bd.shape[0]
    msk = jnp.concatenate(
        [_dot(hi[:, c:c + bw], bd) + _dot(lo[:, c:c + bw], bd) for c in range(0, KN_WIDTH, bw)],
        axis=1)
    kn = kk * lax.rsqrt(msk + NORM_EPS) * gk_ref[...]
    reps = KN_WIDTH // LANES_V7X
    cosn = jnp.concatenate([cosn_ref[...]] * reps, axis=1)
    sinn = jnp.concatenate([sinn_ref[...]] * reps, axis=1)
    lane = lax.broadcasted_iota(jnp.int32, kn.shape, 1)
    first_half = (lane % HEAD_DIM) < (HEAD_DIM // 2)
    swapped = jnp.where(first_half,
                        pltpu.roll(kn, KN_WIDTH - HEAD_DIM // 2, 1),
                        pltpu.roll(kn, HEAD_DIM // 2, 1))
    kr = kn * cosn + swapped * sinn
    mk = kr[:, :MOBA_WIDTH]
    mk_ref[...] = mk.astype(BF16)
    nblk = tm // MOBA_BLOCK
    kmean_ref[0] = jnp.concatenate(
        [jnp.mean(mk[c * MOBA_BLOCK:(c + 1) * MOBA_BLOCK], axis=0, keepdims=True) for c in range(nblk)],
        axis=0)
    sk_ref[...] = kr[:, MOBA_WIDTH:].astype(BF16)
    xqk_ref[...] = nat[:, KN_WIDTH:].astype(BF16)

    tr = _dot_nt(wtr_ref[...], hn)
    cost = cost_ref[...]
    sint = sint_ref[...]
    gq = jnp.concatenate([gq_ref[...]] * (tm // LANES_V7X), axis=1)
    half = HEAD_DIM // 2
    for h in range(QT_ROWS // HEAD_DIM):
        blk = tr[h * HEAD_DIM:(h + 1) * HEAD_DIM]
        is_swa = h >= MOBA_HEADS
        gain = gq[HEAD_DIM:] if is_swa else gq[:HEAD_DIM]
        msq = jnp.mean(blk * blk, axis=0, keepdims=True)
        qn = blk * lax.rsqrt(msq + NORM_EPS) * gain
        x1, x2 = qn[:half], qn[half:]
        rot = jnp.concatenate([x1 * cost - x2 * sint, x2 * cost + x1 * sint], axis=0).astype(BF16)
        if is_swa:
            r0 = (h - MOBA_HEADS) * HEAD_DIM
            sqt_ref[r0:r0 + HEAD_DIM, :] = rot
        else:
            mqt_ref[h * HEAD_DIM:(h + 1) * HEAD_DIM, :] = rot
    r = QT_ROWS
    mv = tr[r:r + MOBA_WIDTH].astype(BF16)
    ones = jnp.ones((MOBA_V_ROWS - HEAD_DIM, MOBA_BLOCK), BF16)
    for c in range(nblk):
        for h in range(MOBA_HEADS):
            r0 = h * MOBA_V_ROWS
            mvt_ref[c, r0:r0 + HEAD_DIM, :] = mv[h * HEAD_DIM:(h + 1) * HEAD_DIM,
                                                 c * MOBA_BLOCK:(c + 1) * MOBA_BLOCK]
            mvt_ref[c, r0 + HEAD_DIM:r0 + MOBA_V_ROWS, :] = ones
    r += MOBA_WIDTH
    svt_ref[...] = tr[r:r + SWA_KV_WIDTH].astype(BF16)
    r += SWA_KV_WIDTH
    xvt_ref[...] = tr[r:r + MLSTM_WIDTH].astype(BF16)
    r += MLSTM_WIDTH
    xot_ref[...] = tr[r:r + MLSTM_WIDTH].astype(BF16)
    r += MLSTM_WIDTH
    gbias = jnp.concatenate([gbias_ref[...]] * (tm // LANES_V7X), axis=1)
    gt_ref[...] = tr[r:r + NUM_GATES] + gbias


def _in_proj(xf, ln1, wnat, wtr, bd, gk, cosn, sinn, gq, cost, sint, gbias, seq):
    t = xf.shape[0]
    tm = IN_PROJ_ROWS
    steps = t // tm
    seq_steps = seq // tm
    nblk = tm // MOBA_BLOCK
    row = lambda w: pl.BlockSpec((tm, w), lambda i: (i, 0))
    col = lambda r: pl.BlockSpec((r, tm), lambda i: (0, i))
    in_specs = [
        row(D_MODEL),
        _const_spec((1, D_MODEL)),
        _const_spec(wnat.shape),
        _const_spec(wtr.shape),
        _const_spec(bd.shape),
        _const_spec(gk.shape),
        pl.BlockSpec((tm, LANES_V7X), lambda i: (i % seq_steps, 0)),
        pl.BlockSpec((tm, LANES_V7X), lambda i: (i % seq_steps, 0)),
        _const_spec(gq.shape),
        pl.BlockSpec((HEAD_DIM // 2, tm), lambda i: (0, i % seq_steps)),
        pl.BlockSpec((HEAD_DIM // 2, tm), lambda i: (0, i % seq_steps)),
        _const_spec(gbias.shape),
    ]
    out_shape = [
        jax.ShapeDtypeStruct((t, MOBA_WIDTH), BF16),
        jax.ShapeDtypeStruct((steps, nblk, MOBA_WIDTH), F32),
        jax.ShapeDtypeStruct((t, SWA_KV_WIDTH), BF16),
        jax.ShapeDtypeStruct((t, 2 * MLSTM_WIDTH), BF16),
        jax.ShapeDtypeStruct((MOBA_WIDTH, t), BF16),
        jax.ShapeDtypeStruct((SWA_Q_WIDTH, t), BF16),
        jax.ShapeDtypeStruct((t // MOBA_BLOCK, MOBA_HEADS * MOBA_V_ROWS, MOBA_BLOCK), BF16),
        jax.ShapeDtypeStruct((SWA_KV_WIDTH, t), BF16),
        jax.ShapeDtypeStruct((MLSTM_WIDTH, t), BF16),
        jax.ShapeDtypeStruct((MLSTM_WIDTH, t), BF16),
        jax.ShapeDtypeStruct((NUM_GATES, t), F32),
    ]
    out_specs = [
        row(MOBA_WIDTH),
        pl.BlockSpec((1, nblk, MOBA_WIDTH), lambda i: (i, 0, 0)),
        row(SWA_KV_WIDTH),
        row(2 * MLSTM_WIDTH),
        col(MOBA_WIDTH),
        col(SWA_Q_WIDTH),
        pl.BlockSpec((nblk, MOBA_HEADS * MOBA_V_ROWS, MOBA_BLOCK), lambda i: (i, 0, 0)),
        col(SWA_KV_WIDTH),
        col(MLSTM_WIDTH),
        col(MLSTM_WIDTH),
        col(NUM_GATES),
    ]
    weights = 2 * (wnat.size + wtr.size)
    tiles = tm * (2 * 4 * D_MODEL + 3 * 4 * (NAT_WIDTH + TR_ROWS) + 2 * 2 * (NAT_WIDTH + TR_ROWS))
    return pl.pallas_call(
        _in_proj_kernel,
        grid=(steps,),
        in_specs=in_specs,
        out_specs=out_specs,
        out_shape=out_shape,
        compiler_params=pltpu.CompilerParams(
            dimension_semantics=("arbitrary",),
            vmem_limit_bytes=_vmem_limit(2 * weights + tiles)),
        name="in_proj",
    )(xf, ln1, wnat, wtr, bd, gk, cosn, sinn, gq, cost, sint, gbias)


def _moba_kernel(bounded_ref, qt_ref, k_ref, vt_ref, km_ref, o_ref, sel_ref, s_ref, p_ref):
    nb = k_ref.shape[0]
    lq = MOBA_BLOCK
    nh = MOBA_Q_BLOCKS * MOBA_HEADS
    step = pl.program_id(1)
    row = lax.broadcasted_iota(jnp.int32, (HEAD_PAIR, lq), 0)
    own, qh = [], []
    for s in range(nh):
        qb, h = divmod(s, MOBA_HEADS)
        own.append(step * MOBA_Q_BLOCKS + qb)
        qt = qt_ref[(h // 2) * HEAD_PAIR:(h // 2 + 1) * HEAD_PAIR, qb * lq:(qb + 1) * lq]
        keep = (row < HEAD_DIM) if h % 2 == 0 else (row >= HEAD_DIM)
        qh.append(jnp.where(keep, qt, jnp.zeros_like(qt)))
    last_own = step * MOBA_Q_BLOCKS + (MOBA_Q_BLOCKS - 1)

    def k_group(blk0, count, s):
        g = (s % MOBA_HEADS) // 2
        return k_ref[pl.ds(blk0, count), :, g * HEAD_PAIR:(g + 1) * HEAD_PAIR].reshape(count * lq, HEAD_PAIR)

    def v_aug(j, s):
        h = s % MOBA_HEADS
        return vt_ref[j, h * MOBA_V_ROWS:(h + 1) * MOBA_V_ROWS, :]

    def select_blocks():
        gates = []
        for s in range(nh):
            g = (s % MOBA_HEADS) // 2
            km_hi, km_lo = _split_bf16(km_ref[:, g * HEAD_PAIR:(g + 1) * HEAD_PAIR])
            gates.append(_dot(km_hi, qh[s]) + _dot(km_lo, qh[s]))
        gate = jnp.concatenate(gates, axis=1)
        own_all = jnp.concatenate([jnp.full((1, lq), own[s], jnp.int32) for s in range(nh)], axis=1)
        blk_all = lax.broadcasted_iota(jnp.int32, gate.shape, 0)
        gate = jnp.where(blk_all < own_all, gate, NEG_INF)
        sel = jnp.full(gate.shape, NEG_INF, F32)
        for _ in range(MOBA_TOPK):
            mx = jnp.max(gate, axis=0, keepdims=True)
            idx = jnp.min(jnp.where(gate == mx, blk_all, nb), axis=0, keepdims=True)
            pick = blk_all == jnp.where(idx < own_all, idx, nb)
            sel = jnp.where(pick, 0.0, sel)
            gate = jnp.where(pick, NEG_INF, gate)
        for s in range(nh):
            sel_ref[s] = sel[:, s * lq:(s + 1) * lq]

    kpos = lax.broadcasted_iota(jnp.int32, (lq, lq), 0)
    qpos = lax.broadcasted_iota(jnp.int32, (lq, lq), 1)
    causal = kpos <= qpos

    def weighted_values(st, mu, v_rows):
        pv = _dot(v_rows, jnp.exp2(st - mu).astype(BF16))
        return pv[HEAD_DIM:HEAD_DIM + 1], pv[:HEAD_DIM]

    def finish(nums, dens):
        for s in range(0, nh, 2):
            qb, h = divmod(s, MOBA_HEADS)
            ot = jnp.concatenate([nums[s] / dens[s], nums[s + 1] / dens[s + 1]], axis=0)
            o_ref[qb * lq:(qb + 1) * lq, (h // 2) * HEAD_PAIR:(h // 2 + 1) * HEAD_PAIR] = ot.T.astype(BF16)

    trips = lax.shift_right_logical(last_own + (2 * MOBA_GROUP - 1), MOBA_GROUP.bit_length())
    units = nh * MOBA_GROUP
    own_slot = 2 * units

    @pl.when(bounded_ref[0] != 0)
    def _():
        select_blocks()
        for h in range(nh):
            st = jnp.where(causal, _dot(k_group(own[h], 1, h), qh[h]), NEG_INF)
            p_ref[own_slot + h] = jnp.exp2(st).astype(BF16)

        def stage_p(blk0, slot0):
            blk0 = jnp.minimum(blk0, nb - MOBA_GROUP)
            for h in range(nh):
                st = _dot(k_group(blk0, MOBA_GROUP, h), qh[h])
                sel = jnp.concatenate(
                    [jnp.broadcast_to(sel_ref[h, pl.ds(blk0 + u, 1), :], (lq, lq)) for u in range(MOBA_GROUP)],
                    axis=0)
                slot = slot0 + MOBA_GROUP * h
                p_ref[slot:slot + MOBA_GROUP] = jnp.exp2(st + sel).astype(BF16).reshape(MOBA_GROUP, lq, lq)

        def stage_v(blk0, slot0, acc):
            acc = list(acc)
            for h in range(nh):
                slot = slot0 + MOBA_GROUP * h
                p = p_ref[slot:slot + MOBA_GROUP].reshape(MOBA_GROUP * lq, lq)
                v = jnp.concatenate([v_aug(blk0 + u, h) for u in range(MOBA_GROUP)], axis=1)
                acc[h] = acc[h] + _dot(v, p)
            return acc

        def trip(t, acc, last):
            blk = t * (2 * MOBA_GROUP)
            stage_p(blk + MOBA_GROUP, units)
            acc = stage_v(blk, 0, acc)
            if not last:
                stage_p(blk + 2 * MOBA_GROUP, 0)
            return tuple(stage_v(blk + MOBA_GROUP, units, acc))

        stage_p(0, 0)
        acc = tuple(_dot(v_aug(own[h], h), p_ref[own_slot + h]) for h in range(nh))
        acc = lax.fori_loop(0, jnp.maximum(trips - 1, 0), lambda t, a: trip(t, a, False), acc)
        acc = lax.cond(trips > 0, lambda a: trip(trips - 1, a, True), lambda a: a, acc)
        finish([a[:HEAD_DIM] for a in acc], [a[HEAD_DIM:HEAD_DIM + 1] for a in acc])


    def stage_a(blk0, slot0):
        mus = []
        for u in range(MOBA_GROUP):
            for h in range(nh):
                st = _dot(k_group(jnp.minimum(blk0 + u, nb - 1), 1, h), qh[h])
                s_ref[slot0 + nh * u + h] = st
                mus.append(jnp.max(st, axis=0, keepdims=True))
        return mus

    def stage_b(blk0, slot0, mus, state):
        state = list(state)
        for u in range(MOBA_GROUP):
            j = blk0 + u
            for h in range(nh):
                mu = mus[nh * u + h]
                ls, pv = weighted_values(s_ref[slot0 + nh * u + h], mu, v_aug(j, h))
                m, l, acc = state[3 * h:3 * h + 3]
                mu_sel = mu + sel_ref[h, pl.ds(j, 1), :]
                m_new = jnp.maximum(m, mu_sel)
                f = jnp.exp2(mu_sel - m_new)
                c = jnp.exp2(m - m_new)
                state[3 * h:3 * h + 3] = [m_new, c * l + f * ls, c * acc + f * pv]
        return state

    def body(t, loop_carry):
        mus0, state = loop_carry[:units], loop_carry[units:]
        blk = t * (2 * MOBA_GROUP)
        mus1 = stage_a(blk + MOBA_GROUP, units)
        state = stage_b(blk, 0, mus0, state)
        mus0 = stage_a(blk + 2 * MOBA_GROUP, 0)
        state = stage_b(blk + MOBA_GROUP, units, mus1, state)
        return (*mus0, *state)

    @pl.when(bounded_ref[0] == 0)
    def _():
        state = []
        for h in range(nh):
            st = jnp.where(causal, _dot(k_group(own[h], 1, h), qh[h]), NEG_INF)
            mu = jnp.max(st, axis=0, keepdims=True)
            state += [mu, *weighted_values(st, mu, v_aug(own[h], h))]
        mus0 = stage_a(0, 0)
        select_blocks()
        state = lax.fori_loop(0, trips, body, (*mus0, *state))[units:]
        finish(state[2::3], state[1::3])


def _moba(bounded, mqt, mk, mvt, kmean, batch, seq):
    t = mk.shape[0]
    nb = seq // MOBA_BLOCK
    lq = MOBA_BLOCK
    k3 = mk.reshape(t // MOBA_BLOCK, MOBA_BLOCK, MOBA_WIDTH)
    assert nb % (2 * MOBA_GROUP) == 0 and nb % MOBA_Q_BLOCKS == 0
    streams = MOBA_Q_BLOCKS * MOBA_HEADS
    units = streams * MOBA_GROUP
    steps = nb // MOBA_Q_BLOCKS
    kv_bytes = 2 * nb * MOBA_BLOCK * (MOBA_WIDTH + MOBA_HEADS * MOBA_V_ROWS)
    scratch_bytes = lq * lq * (4 * 2 * units + 2 * (2 * units + streams)) + 4 * streams * nb * lq
    resident = lambda shape, imap: pl.BlockSpec(shape, imap, pipeline_mode=pl.Buffered(1))
    return pl.pallas_call(
        _moba_kernel,
        grid=(batch, steps),
        in_specs=[
            pl.BlockSpec(memory_space=pltpu.SMEM),
            pl.BlockSpec((MOBA_WIDTH, MOBA_Q_BLOCKS * lq), lambda b, i: (0, b * steps + i)),
            resident((nb, MOBA_BLOCK, MOBA_WIDTH), lambda b, i: (b, 0, 0)),
            resident((nb, MOBA_HEADS * MOBA_V_ROWS, MOBA_BLOCK), lambda b, i: (b, 0, 0)),
            resident((nb, MOBA_WIDTH), lambda b, i: (b, 0)),
        ],
        out_specs=pl.BlockSpec((MOBA_Q_BLOCKS * lq, MOBA_WIDTH), lambda b, i: (b * steps + i, 0)),
        out_shape=jax.ShapeDtypeStruct((t, MOBA_WIDTH), BF16),
        scratch_shapes=[pltpu.VMEM((streams, nb, lq), F32),
                        pltpu.VMEM((2 * units, lq, lq), F32),
                        pltpu.VMEM((2 * units + streams, lq, lq), BF16)],
        compiler_params=pltpu.CompilerParams(
            dimension_semantics=("arbitrary", "arbitrary"),
            vmem_limit_bytes=_vmem_limit(kv_bytes + scratch_bytes + VMEM_HEADROOM_BYTES)),
        name="moba",
    )(bounded, mqt, k3, mvt, kmean)


def _swa_kernel(bounded_ref, sink_ref, qt_ref, k_ref, kh_ref, vt_ref, vth_ref, o_ref, p_ref):
    tq = k_ref.shape[0]
    w = SWA_WINDOW
    i = pl.program_id(1)
    ncol = SWA_GROUP * w
    kpos = lax.broadcasted_iota(jnp.int32, (2 * w, ncol), 0)
    col = lax.broadcasted_iota(jnp.int32, (2 * w, ncol), 1)
    diff = w + (col % w) - kpos
    in_window = (diff >= 0) & (diff < w)
    zeros = jnp.zeros((HEAD_DIM, ncol), BF16)
    ones = jnp.ones((MOBA_V_ROWS - HEAD_DIM, 2 * w), BF16)

    def attend(shifted):
        sink_terms = []
        for r in range(tq // w):
            if r == 0:
                kcat = jnp.concatenate([kh_ref[...], k_ref[0:w, :]], axis=0)
                mask = in_window & ((kpos >= w) | (i > 0))
            else:
                kcat = k_ref[(r - 1) * w:(r + 1) * w, :]
                mask = in_window
            for g in range(SWA_KV_HEADS):
                qs = jnp.concatenate(
                    [qt_ref[(SWA_GROUP * g + a) * HEAD_DIM:(SWA_GROUP * g + a + 1) * HEAD_DIM, r * w:(r + 1) * w]
                     for a in range(SWA_GROUP)], axis=1)
                qpad = jnp.concatenate([qs, zeros] if g == 0 else [zeros, qs], axis=0)
                st = jnp.where(mask, _dot(kcat, qpad), NEG_INF)
                sink = jnp.concatenate(
                    [jnp.full((1, w), sink_ref[SWA_GROUP * g + a] * LOG2E, F32) for a in range(SWA_GROUP)],
                    axis=1)
                if shifted:
                    m = jnp.maximum(jnp.max(st, axis=0, keepdims=True), sink)
                    st, sink = st - m, sink - m
                p_ref[r * SWA_KV_HEADS + g] = jnp.exp2(st).astype(BF16)
                sink_terms.append(jnp.exp2(sink))
        for r in range(tq // w):
            if r == 0:
                vcat = jnp.concatenate([vth_ref[...], vt_ref[:, 0:w]], axis=1)
            else:
                vcat = vt_ref[:, (r - 1) * w:(r + 1) * w]
            pieces = []
            for g in range(SWA_KV_HEADS):
                v_aug = jnp.concatenate([vcat[g * HEAD_DIM:(g + 1) * HEAD_DIM, :], ones], axis=0)
                pv = _dot(v_aug, p_ref[r * SWA_KV_HEADS + g])
                ot = pv[:HEAD_DIM] / (pv[HEAD_DIM:HEAD_DIM + 1] + sink_terms[r * SWA_KV_HEADS + g])
                pieces += [ot[:, a * w:(a + 1) * w] for a in range(SWA_GROUP)]
            o_ref[r * w:(r + 1) * w, :] = jnp.concatenate(pieces, axis=0).T.astype(BF16)

    @pl.when(bounded_ref[0] != 0)
    def _():
        attend(False)

    @pl.when(bounded_ref[0] == 0)
    def _():
        attend(True)


def _swa(bounded, sinks, sqt, sk, svt, batch, seq):
    t = sk.shape[0]
    tq = SWA_ROWS
    w = SWA_WINDOW
    steps = seq // tq
    halo = lambda b, i: jnp.maximum((b * seq + i * tq) // w - 1, 0)
    return pl.pallas_call(
        _swa_kernel,
        grid=(batch, steps),
        in_specs=[
            pl.BlockSpec(memory_space=pltpu.SMEM),
            pl.BlockSpec(memory_space=pltpu.SMEM),
            pl.BlockSpec((SWA_Q_WIDTH, tq), lambda b, i: (0, b * steps + i)),
            pl.BlockSpec((tq, SWA_KV_WIDTH), lambda b, i: (b * steps + i, 0)),
            pl.BlockSpec((w, SWA_KV_WIDTH), lambda b, i: (halo(b, i), 0)),
            pl.BlockSpec((SWA_KV_WIDTH, tq), lambda b, i: (0, b * steps + i)),
            pl.BlockSpec((SWA_KV_WIDTH, w), lambda b, i: (0, halo(b, i))),
        ],
        out_specs=pl.BlockSpec((tq, SWA_Q_WIDTH), lambda b, i: (b * steps + i, 0)),
        out_shape=jax.ShapeDtypeStruct((t, SWA_Q_WIDTH), BF16),
        scratch_shapes=[pltpu.VMEM(((tq // w) * SWA_KV_HEADS, 2 * w, SWA_GROUP * w), BF16)],
        compiler_params=pltpu.CompilerParams(dimension_semantics=("arbitrary", "arbitrary")),
        name="swa",
    )(bounded, sinks, sqt, sk, sk, svt, svt)


def _log_sigmoid(v):
    return jnp.minimum(v, 0.0) - jnp.log1p(jnp.exp(-jnp.abs(v)))


def _mlstm_kernel(xqk_ref, halo_ref, xvt_ref, xot_ref, gt_ref, gc_ref, convw_ref, convb_ref,
                  gain_ref, cum_ref, cumt_ref, o_ref, ext_ref, c_ref, n_ref, m_ref,
                  qm_ref, k_ref, pv_ref, row_ref, kv_ref, nk_ref, cprev_ref, nprev_ref, ht_ref,
                  st_ref, qkt_ref, vtw_ref):
    tt = xqk_ref.shape[0]
    lc = MLSTM_CHUNK
    i = pl.program_id(1)

    @pl.when(i == 0)
    def _():
        c_ref[...] = jnp.zeros_like(c_ref)
        n_ref[...] = jnp.zeros_like(n_ref)
        m_ref[...] = jnp.zeros_like(m_ref)

    hist = halo_ref[...].astype(F32)
    ext_ref[0:CONV_HALO, :] = jnp.where(i > 0, hist, jnp.zeros_like(hist))
    ext_ref[CONV_HALO:CONV_HALO + tt, :] = xqk_ref[...].astype(F32)
    convw = convw_ref[...]
    conv = jnp.zeros((tt, 2 * MLSTM_WIDTH), F32) + convb_ref[...]
    for j in range(CONV_WIDTH):
        start = CONV_HALO - (CONV_WIDTH - 1) + j
        conv = conv + convw[j:j + 1, :] * ext_ref[start:start + tt, :]
    qk = conv * jax.nn.sigmoid(conv)
    q_all = qk[:, :MLSTM_WIDTH]
    lane_q = lax.broadcasted_iota(jnp.int32, q_all.shape, 1)
    even_head = (lane_q % HEAD_PAIR) < HEAD_DIM
    qm_ref[0] = jnp.where(even_head, q_all, 0.0).astype(BF16)
    qm_ref[1] = jnp.where(even_head, 0.0, q_all).astype(BF16)
    k_ref[...] = (qk[:, MLSTM_WIDTH:] * SM_SCALE).astype(BF16)

    g_row = gt_ref[...]
    g_col = gc_ref[...]
    cum = cum_ref[...]
    cumt = cumt_ref[...]
    b_row = sum(_dot(piece, cum) for piece in _split3_bf16(_log_sigmoid(g_row)))
    b_col = sum(_dot(cumt, piece) for piece in _split3_bf16(_log_sigmoid(g_col)))
    u_col = b_col[:, MLSTM_HEADS:] - g_col[:, :MLSTM_HEADS]

    lane = lax.broadcasted_iota(jnp.int32, (HEAD_DIM, HEAD_PAIR), 1)
    s_idx = lax.broadcasted_iota(jnp.int32, (lc, lc), 0)
    t_idx = lax.broadcasted_iota(jnp.int32, (lc, lc), 1)
    causal = s_idx <= t_idx
    nchunks = tt // lc
    heads = range(MLSTM_HEADS)
    in_head = [(lane < HEAD_DIM), (lane >= HEAD_DIM)]

    def operands(c, h):
        p, hh = divmod(h, 2)
        sl = slice(c * lc, (c + 1) * lc)
        qm = qm_ref[hh, sl, p * HEAD_PAIR:(p + 1) * HEAD_PAIR]
        kp = k_ref[sl, p * HEAD_PAIR:(p + 1) * HEAD_PAIR]
        return sl, hh, qm, kp

    m_state = [m_ref[h][0:1, 0:lc] for h in heads]
    for c in range(nchunks):
        for h in heads:
            _, _, qm, kp = operands(c, h)
            st_ref[c * MLSTM_HEADS + h] = _dot_nt(kp, qm)
    for c in range(nchunks):
        for h in heads:
            u = c * MLSTM_HEADS + h
            sl = slice(c * lc, (c + 1) * lc)
            vt = xvt_ref[h * HEAD_DIM:(h + 1) * HEAD_DIM, sl]
            br = b_row[MLSTM_HEADS + h:MLSTM_HEADS + h + 1, sl]
            ir = g_row[h:h + 1, sl]
            uc = u_col[sl, h:h + 1]
            a = br[:, lc - 1:lc]
            dt = jnp.where(causal, br - uc, NEG_INF)
            inter = br + m_state[h]
            m_t = jnp.maximum(inter, jnp.max(dt, axis=0, keepdims=True))
            w_inter = jnp.exp(inter - m_t)
            qkt = st_ref[u] * jnp.exp(dt - m_t)
            qkt_ref[u] = qkt.astype(BF16)
            row_ref[u, 0:1, 0:lc] = w_inter
            row_ref[u, 1:2, 0:lc] = jnp.sum(qkt, axis=0, keepdims=True)
            row_ref[u, 2:3, 0:lc] = jnp.exp(-m_t)
            g_end = a - br + ir
            m_new = jnp.maximum(a + m_state[h], jnp.max(g_end, axis=1, keepdims=True))
            w_s = jnp.exp(g_end - m_new)
            decay = jnp.exp(a + m_state[h] - m_new)
            row_ref[u, 3:4, 0:lc] = decay
            row_ref[u, 4:5, 0:lc] = w_s
            vtw_ref[u] = (vt.astype(F32) * w_s).astype(BF16)
            m_state[h] = m_new
    for c in range(nchunks):
        for h in heads:
            u = c * MLSTM_HEADS + h
            sl, hh, _, kp = operands(c, h)
            vt = xvt_ref[h * HEAD_DIM:(h + 1) * HEAD_DIM, sl]
            pv_ref[u] = _dot(vt, qkt_ref[u])
            kv_ref[u] = jnp.where(in_head[hh], _dot(vtw_ref[u], kp), 0.0)
            w_rows = jnp.broadcast_to(row_ref[u, 4:5, 0:lc], (8, lc)).astype(BF16)
            nk_ref[u] = jnp.where(in_head[hh][0:8], _dot(w_rows, kp), 0.0)

    for h in heads:
        c_state = c_ref[h]
        n_state = n_ref[h]
        for c in range(nchunks):
            u = c * MLSTM_HEADS + h
            cprev_ref[u] = c_state.astype(BF16)
            nprev_ref[u] = n_state
            decay = row_ref[u, 3:4, 0:HEAD_PAIR]
            c_state = decay * c_state + kv_ref[u]
            n_state = decay * n_state + nk_ref[u]
        c_ref[h] = c_state
        n_ref[h] = n_state
        m_ref[h] = jnp.broadcast_to(m_state[h][:, 0:1], m_ref.shape[1:])

    for c in range(nchunks):
        for h in heads:
            u = c * MLSTM_HEADS + h
            sl, hh, qm, _ = operands(c, h)
            w_inter = row_ref[u, 0:1, 0:lc]
            num = w_inter * _dot_nt(cprev_ref[u], qm) + pv_ref[u]
            den = w_inter * _dot_nt(nprev_ref[u].astype(BF16), qm)[0:1, :] + row_ref[u, 1:2, 0:lc]
            ht_ref[h * HEAD_DIM:(h + 1) * HEAD_DIM, sl] = num / jnp.maximum(jnp.abs(den), row_ref[u, 2:3, 0:lc])

    gain = jnp.concatenate([gain_ref[...]] * (tt // LANES_V7X), axis=1)
    outs = []
    for h in heads:
        rows = slice(h * HEAD_DIM, (h + 1) * HEAD_DIM)
        hg = ht_ref[rows, :] * jax.nn.sigmoid(xot_ref[rows, :].astype(F32))
        msq = jnp.mean(hg * hg, axis=0, keepdims=True)
        outs.append(hg * lax.rsqrt(msq + NORM_EPS) * gain[rows])
    o_ref[...] = jnp.concatenate(outs, axis=0).T.astype(BF16)


def _mlstm(xqk, xvt, xot, gt, gc, convw, convb, gain, cum, batch, seq):
    t = xqk.shape[0]
    tt = MLSTM_ROWS
    steps = seq // tt
    units = MLSTM_HEADS * (tt // MLSTM_CHUNK)
    col = lambda r: pl.BlockSpec((r, tt), lambda b, i: (0, b * steps + i))
    return pl.pallas_call(
        _mlstm_kernel,
        grid=(batch, steps),
        in_specs=[
            pl.BlockSpec((tt, 2 * MLSTM_WIDTH), lambda b, i: (b * steps + i, 0)),
            pl.BlockSpec((CONV_HALO, 2 * MLSTM_WIDTH),
                         lambda b, i: (jnp.maximum((b * seq + i * tt) // CONV_HALO - 1, 0), 0)),
            col(MLSTM_WIDTH),
            col(MLSTM_WIDTH),
            col(NUM_GATES),
            pl.BlockSpec((tt, NUM_GATES), lambda b, i: (b * steps + i, 0)),
            _const_spec(convw.shape),
            _const_spec(convb.shape),
            _const_spec(gain.shape),
            _const_spec(cum.shape),
            _const_spec(cum.shape),
        ],
        out_specs=pl.BlockSpec((tt, MLSTM_WIDTH), lambda b, i: (b * steps + i, 0)),
        out_shape=jax.ShapeDtypeStruct((t, MLSTM_WIDTH), BF16),
        scratch_shapes=[
            pltpu.VMEM((CONV_HALO + tt, 2 * MLSTM_WIDTH), F32),
            pltpu.VMEM((MLSTM_HEADS, HEAD_DIM, HEAD_PAIR), F32),
            pltpu.VMEM((MLSTM_HEADS, 8, HEAD_PAIR), F32),
            pltpu.VMEM((MLSTM_HEADS, 8, MLSTM_CHUNK), F32),
            pltpu.VMEM((2, tt, MLSTM_WIDTH), BF16),
            pltpu.VMEM((tt, MLSTM_WIDTH), BF16),
            pltpu.VMEM((units, HEAD_DIM, MLSTM_CHUNK), F32),
            pltpu.VMEM((units, 8, MLSTM_CHUNK), F32),
            pltpu.VMEM((units, HEAD_DIM, HEAD_PAIR), F32),
            pltpu.VMEM((units, 8, HEAD_PAIR), F32),
            pltpu.VMEM((units, HEAD_DIM, HEAD_PAIR), BF16),
            pltpu.VMEM((units, 8, HEAD_PAIR), F32),
            pltpu.VMEM((MLSTM_WIDTH, tt), F32),
            pltpu.VMEM((units, MLSTM_CHUNK, MLSTM_CHUNK), F32),
            pltpu.VMEM((units, MLSTM_CHUNK, MLSTM_CHUNK), BF16),
            pltpu.VMEM((units, HEAD_DIM, MLSTM_CHUNK), BF16),
        ],
        compiler_params=pltpu.CompilerParams(dimension_semantics=("arbitrary", "arbitrary")),
        name="mlstm",
    )(xqk, xqk, xvt, xot, gt, gc, convw, convb, gain, cum, cum.T)


def _out_mlp_kernel(x_ref, ym_ref, yl_ref, ys_ref, wo_ref, ln2_ref, wup_ref, wdn_ref, o_ref):
    y = jnp.concatenate([ym_ref[...], yl_ref[...], ys_ref[...]], axis=1)
    x1 = x_ref[...] + _dot(y, wo_ref[...])
    ms = jnp.mean(x1 * x1, axis=-1, keepdims=True)
    hn = (x1 * lax.rsqrt(ms + NORM_EPS) * ln2_ref[...]).astype(BF16)
    o_ref[...] = x1
    for c in range(0, D_FF, MLP_FF_CHUNK):
        u = _dot(hn, wup_ref[:, c:c + MLP_FF_CHUNK])
        act = jnp.square(jnp.maximum(u, 0.0)).astype(BF16)
        o_ref[...] += _dot(act, wdn_ref[c:c + MLP_FF_CHUNK, :])


def _out_mlp(xf, ym, yl, ys, wo, ln2, wup, wdn):
    t = xf.shape[0]
    tm = OUT_MLP_ROWS
    row = lambda w: pl.BlockSpec((tm, w), lambda i: (i, 0))
    weights = 2 * (wo.size + wup.size + wdn.size)
    tiles = tm * (4 * 4 * D_MODEL + 2 * 2 * D_MODEL + 4 * 4 * D_MODEL + 6 * MLP_FF_CHUNK)
    return pl.pallas_call(
        _out_mlp_kernel,
        grid=(t // tm,),
        in_specs=[
            row(D_MODEL), row(MOBA_WIDTH), row(MLSTM_WIDTH), row(SWA_Q_WIDTH),
            _const_spec(wo.shape), _const_spec(ln2.shape), _const_spec(wup.shape), _const_spec(wdn.shape),
        ],
        out_specs=row(D_MODEL),
        out_shape=jax.ShapeDtypeStruct((t, D_MODEL), F32),
        compiler_params=pltpu.CompilerParams(
            dimension_semantics=("arbitrary",),
            vmem_limit_bytes=_vmem_limit(weights + tiles)),
        name="out_mlp",
    )(xf, ym, yl, ys, wo, ln2, wup, wdn)


def _rope_tables(seq):
    inv = ROPE_THETA ** (-jnp.arange(0, HEAD_DIM, 2, dtype=F32) / HEAD_DIM)
    ang = jnp.arange(seq, dtype=F32)[:, None] * inv[None, :]
    cos, sin = jnp.cos(ang), jnp.sin(ang)
    cosn = jnp.concatenate([cos, cos, cos, cos], axis=1)
    sinn = jnp.concatenate([-sin, sin, -sin, sin], axis=1)
    return cosn, sinn, cos.T, sin.T


def _layer(xf, tables, consts, batch, seq, ln1, w_in, conv_w, conv_b, igate_b, fgate_b, mlstm_norm,
           moba_q_norm, moba_k_norm, swa_q_norm, swa_k_norm, swa_sinks, w_out, ln2, w_up, w_down):
    cosn, sinn, cost, sint = tables
    bd, cum = consts
    o = 0
    cols = {}
    for name, width in (("mq", MOBA_WIDTH), ("mk", MOBA_WIDTH), ("mv", MOBA_WIDTH), ("sq", SWA_Q_WIDTH),
                        ("sk", SWA_KV_WIDTH), ("sv", SWA_KV_WIDTH), ("xqk", 2 * MLSTM_WIDTH),
                        ("xv", MLSTM_WIDTH), ("xo", MLSTM_WIDTH), ("xi", MLSTM_HEADS), ("xf", MLSTM_HEADS)):
        cols[name] = w_in[:, o:o + width]
        o += width
    wnat = jnp.concatenate([cols["mk"], cols["sk"], cols["xqk"]], axis=1).astype(BF16)
    wtr = jnp.concatenate([cols["mq"], cols["sq"], cols["mv"], cols["sv"], cols["xv"], cols["xo"],
                           cols["xi"], cols["xf"]], axis=1).T.astype(BF16)
    gk = jnp.concatenate([jnp.tile(moba_k_norm, MOBA_HEADS), jnp.tile(swa_k_norm, SWA_KV_HEADS)])[None, :]
    gq = jnp.broadcast_to(
        (jnp.concatenate([moba_q_norm, swa_q_norm]) * (SM_SCALE * LOG2E))[:, None], (2 * HEAD_DIM, LANES_V7X))
    gbias = jnp.broadcast_to(jnp.concatenate([igate_b, fgate_b])[:, None], (NUM_GATES, LANES_V7X))

    (mk, kmean, sk, xqk, mqt, sqt, mvt, svt, xvt, xot, gt) = _in_proj(
        xf, ln1[None, :], wnat, wtr, bd, gk, cosn, sinn, gq, cost, sint, gbias, seq)

    kmean = kmean.reshape(-1, MOBA_WIDTH)
    def bounded(q_gain, k_gain, *extra):
        bound = (HEAD_DIM * SM_SCALE * LOG2E) * jnp.max(jnp.abs(q_gain)) * jnp.max(jnp.abs(k_gain))
        for e in extra:
            bound = jnp.maximum(bound, jnp.max(jnp.abs(e)) * LOG2E)
        return (bound <= MOBA_SAFE_LOG2).astype(jnp.int32).reshape(1)

    ym = _moba(bounded(moba_q_norm, moba_k_norm), mqt, mk, mvt, kmean, batch, seq)
    ys = _swa(bounded(swa_q_norm, swa_k_norm, swa_sinks), swa_sinks, sqt, sk, svt, batch, seq)
    gain = jnp.broadcast_to(mlstm_norm.reshape(MLSTM_WIDTH, 1), (MLSTM_WIDTH, LANES_V7X))
    yl = _mlstm(xqk, xvt, xot, gt, gt.T, conv_w, conv_b[None, :], gain, cum, batch, seq)

    return _out_mlp(xf, ym, yl, ys, w_out.astype(BF16), ln2[None, :], w_up.astype(BF16), w_down.astype(BF16))


def kernel(x, ln1, w_in, conv_w, conv_b, igate_b, fgate_b, mlstm_norm, moba_q_norm, moba_k_norm,
           swa_q_norm, swa_k_norm, swa_sinks, w_out, ln2, w_up, w_down):
    batch, seq, d = x.shape
    assert d == D_MODEL and seq % max(IN_PROJ_ROWS, SWA_ROWS, MLSTM_ROWS, MOBA_BLOCK) == 0
    depth = ln1.shape[0]
    tables = _rope_tables(seq)
    bw = 2 * LANES_V7X
    ids = jnp.arange(bw) // HEAD_DIM
    bd = jnp.where(ids[:, None] == ids[None, :], 1.0 / HEAD_DIM, 0.0).astype(BF16)
    tids = jnp.arange(MLSTM_ROWS)
    cum = ((tids[:, None] // MLSTM_CHUNK == tids[None, :] // MLSTM_CHUNK)
           & (tids[:, None] <= tids[None, :])).astype(BF16)
    xf = x.reshape(batch * seq, d)
    for l in range(depth):
        xf = _layer(xf, tables, (bd, cum), batch, seq, ln1[l], w_in[l], conv_w[l], conv_b[l], igate_b[l],
                    fgate_b[l], mlstm_norm[l], moba_q_norm[l], moba_k_norm[l], swa_q_norm[l],
                    swa_k_norm[l], swa_sinks[l], w_out[l], ln2[l], w_up[l], w_down[l])
    return xf.reshape(batch, seq, d)
```

```python
import jax
import jax.numpy as jnp
from jax import lax
from jax.experimental import pallas as pl
from jax.experimental.pallas import tpu as pltpu

F32 = jnp.float32
BF16 = jnp.bfloat16
NEG_INF = float("-inf")

D_MODEL = 1024
HEAD_DIM = 64
MOBA_HEADS = 6
MLSTM_HEADS = 4
SWA_Q_HEADS = 6
SWA_KV_HEADS = 2
SWA_GROUP = SWA_Q_HEADS // SWA_KV_HEADS
MOBA_WIDTH = MOBA_HEADS * HEAD_DIM
MLSTM_WIDTH = MLSTM_HEADS * HEAD_DIM
SWA_Q_WIDTH = SWA_Q_HEADS * HEAD_DIM
SWA_KV_WIDTH = SWA_KV_HEADS * HEAD_DIM
MOBA_BLOCK = 256
MOBA_TOPK = 3
MLSTM_CHUNK = 256
CONV_WIDTH = 4
SWA_WINDOW = 128
ROPE_THETA = 10000.0
D_FF = 4 * D_MODEL
NORM_EPS = 1e-6
SM_SCALE = HEAD_DIM ** -0.5
LOG2E = 1.4426950408889634
MOBA_SAFE_LOG2 = 60.0

LANES_V7X = 128
BF16_TILE_ROWS_V7X = 16
VMEM_BYTES_V7X = 64 * 1024 * 1024
VMEM_HEADROOM_BYTES = 4 * 1024 * 1024
VMEM_TEMPS_BYTES = 8 * 1024 * 1024
HEAD_PAIR = 2 * HEAD_DIM
assert HEAD_PAIR == LANES_V7X
MOBA_V_ROWS = HEAD_DIM + BF16_TILE_ROWS_V7X

NAT_WIDTH = MOBA_WIDTH + SWA_KV_WIDTH + 2 * MLSTM_WIDTH
KN_WIDTH = MOBA_WIDTH + SWA_KV_WIDTH
QT_ROWS = MOBA_WIDTH + SWA_Q_WIDTH
TR_ROWS = QT_ROWS + MOBA_WIDTH + SWA_KV_WIDTH + 2 * MLSTM_WIDTH + 2 * MLSTM_HEADS
NUM_GATES = 2 * MLSTM_HEADS

IN_PROJ_ROWS = 1024
OUT_MLP_ROWS = 1024
MLP_FF_CHUNK = 1024
SWA_ROWS = 1024
MLSTM_ROWS = 256
MOBA_GROUP = 2
MOBA_Q_BLOCKS = 1
CONV_HALO = BF16_TILE_ROWS_V7X


def _dot(a, b):
    return jnp.dot(a, b, preferred_element_type=F32)


def _dot_nt(a, b):
    return lax.dot_general(a, b, (((1,), (1,)), ((), ())), preferred_element_type=F32)


def _split_bf16(v):
    hi = v.astype(BF16)
    lo = (v - hi.astype(F32)).astype(BF16)
    return hi, lo


def _split3_bf16(v):
    hi = v.astype(BF16)
    rest = v - hi.astype(F32)
    mid = rest.astype(BF16)
    return hi, mid, (rest - mid.astype(F32)).astype(BF16)


def _vmem_limit(nbytes):
    return int(min(nbytes + VMEM_TEMPS_BYTES, VMEM_BYTES_V7X - VMEM_HEADROOM_BYTES))


def _const_spec(shape):
    nd = len(shape)
    return pl.BlockSpec(shape, lambda *_: (0,) * nd, pipeline_mode=pl.Buffered(1))


def _in_proj_kernel(x_ref, ln1_ref, wnat_ref, wtr_ref, bd_ref, gk_ref, cosn_ref, sinn_ref,
                    gq_ref, cost_ref, sint_ref, gbias_ref,
                    mk_ref, kmean_ref, sk_ref, xqk_ref, mqt_ref, sqt_ref, mvt_ref, svt_ref,
                    xvt_ref, xot_ref, gt_ref):
    tm = x_ref.shape[0]
    x = x_ref[...]
    ms = jnp.mean(x * x, axis=-1, keepdims=True)
    hn = (x * lax.rsqrt(ms + NORM_EPS) * ln1_ref[...]).astype(BF16)
    nat = _dot(hn, wnat_ref[...])

    kk = nat[:, :KN_WIDTH]
    hi, lo = _split_bf16(kk * kk)
    bd = bd_ref[...]
    bw = bd.shape[0]
    msk = jnp.concatenate(
        [_dot(hi[:, c:c + bw], bd) + _dot(lo[:, c:c + bw], bd) for c in range(0, KN_WIDTH, bw)],
        axis=1)
    kn = kk * lax.rsqrt(msk + NORM_EPS) * gk_ref[...]
    reps = KN_WIDTH // LANES_V7X
    cosn = jnp.concatenate([cosn_ref[...]] * reps, axis=1)
    sinn = jnp.concatenate([sinn_ref[...]] * reps, axis=1)
    lane = lax.broadcasted_iota(jnp.int32, kn.shape, 1)
    first_half = (lane % HEAD_DIM) < (HEAD_DIM // 2)
    swapped = jnp.where(first_half,
                        pltpu.roll(kn, KN_WIDTH - HEAD_DIM // 2, 1),
                        pltpu.roll(kn, HEAD_DIM // 2, 1))
    kr = kn * cosn + swapped * sinn
    mk = kr[:, :MOBA_WIDTH]
    mk_ref[...] = mk.astype(BF16)
    nblk = tm // MOBA_BLOCK
    kmean_ref[0] = jnp.concatenate(
        [jnp.mean(mk[c * MOBA_BLOCK:(c + 1) * MOBA_BLOCK], axis=0, keepdims=True) for c in range(nblk)],
        axis=0)
    sk_ref[...] = kr[:, MOBA_WIDTH:].astype(BF16)
    xqk_ref[...] = nat[:, KN_WIDTH:].astype(BF16)

    tr = _dot_nt(wtr_ref[...], hn)
    cost = cost_ref[...]
    sint = sint_ref[...]
    gq = jnp.concatenate([gq_ref[...]] * (tm // LANES_V7X), axis=1)
    half = HEAD_DIM // 2
    for h in range(QT_ROWS // HEAD_DIM):
        blk = tr[h * HEAD_DIM:(h + 1) * HEAD_DIM]
        is_swa = h >= MOBA_HEADS
        gain = gq[HEAD_DIM:] if is_swa else gq[:HEAD_DIM]
        msq = jnp.mean(blk * blk, axis=0, keepdims=True)
        qn = blk * lax.rsqrt(msq + NORM_EPS) * gain
        x1, x2 = qn[:half], qn[half:]
        rot = jnp.concatenate([x1 * cost - x2 * sint, x2 * cost + x1 * sint], axis=0).astype(BF16)
        if is_swa:
            r0 = (h - MOBA_HEADS) * HEAD_DIM
            sqt_ref[r0:r0 + HEAD_DIM, :] = rot
        else:
            mqt_ref[h * HEAD_DIM:(h + 1) * HEAD_DIM, :] = rot
    r = QT_ROWS
    mv = tr[r:r + MOBA_WIDTH].astype(BF16)
    ones = jnp.ones((MOBA_V_ROWS - HEAD_DIM, MOBA_BLOCK), BF16)
    for c in range(nblk):
        for h in range(MOBA_HEADS):
            r0 = h * MOBA_V_ROWS
            mvt_ref[c, r0:r0 + HEAD_DIM, :] = mv[h * HEAD_DIM:(h + 1) * HEAD_DIM,
                                                 c * MOBA_BLOCK:(c + 1) * MOBA_BLOCK]
            mvt_ref[c, r0 + HEAD_DIM:r0 + MOBA_V_ROWS, :] = ones
    r += MOBA_WIDTH
    svt_ref[...] = tr[r:r + SWA_KV_WIDTH].astype(BF16)
    r += SWA_KV_WIDTH
    xvt_ref[...] = tr[r:r + MLSTM_WIDTH].astype(BF16)
    r += MLSTM_WIDTH
    xot_ref[...] = tr[r:r + MLSTM_WIDTH].astype(BF16)
    r += MLSTM_WIDTH
    gbias = jnp.concatenate([gbias_ref[...]] * (tm // LANES_V7X), axis=1)
    gt_ref[...] = tr[r:r + NUM_GATES] + gbias


def _in_proj(xf, ln1, wnat, wtr, bd, gk, cosn, sinn, gq, cost, sint, gbias, seq):
    t = xf.shape[0]
    tm = IN_PROJ_ROWS
    steps = t // tm
    seq_steps = seq // tm
    nblk = tm // MOBA_BLOCK
    row = lambda w: pl.BlockSpec((tm, w), lambda i: (i, 0))
    col = lambda r: pl.BlockSpec((r, tm), lambda i: (0, i))
    in_specs = [
        row(D_MODEL),
        _const_spec((1, D_MODEL)),
        _const_spec(wnat.shape),
        _const_spec(wtr.shape),
        _const_spec(bd.shape),
        _const_spec(gk.shape),
        pl.BlockSpec((tm, LANES_V7X), lambda i: (i % seq_steps, 0)),
        pl.BlockSpec((tm, LANES_V7X), lambda i: (i % seq_steps, 0)),
        _const_spec(gq.shape),
        pl.BlockSpec((HEAD_DIM // 2, tm), lambda i: (0, i % seq_steps)),
        pl.BlockSpec((HEAD_DIM // 2, tm), lambda i: (0, i % seq_steps)),
        _const_spec(gbias.shape),
    ]
    out_shape = [
        jax.ShapeDtypeStruct((t, MOBA_WIDTH), BF16),
        jax.ShapeDtypeStruct((steps, nblk, MOBA_WIDTH), F32),
        jax.ShapeDtypeStruct((t, SWA_KV_WIDTH), BF16),
        jax.ShapeDtypeStruct((t, 2 * MLSTM_WIDTH), BF16),
        jax.ShapeDtypeStruct((MOBA_WIDTH, t), BF16),
        jax.ShapeDtypeStruct((SWA_Q_WIDTH, t), BF16),
        jax.ShapeDtypeStruct((t // MOBA_BLOCK, MOBA_HEADS * MOBA_V_ROWS, MOBA_BLOCK), BF16),
        jax.ShapeDtypeStruct((SWA_KV_WIDTH, t), BF16),
        jax.ShapeDtypeStruct((MLSTM_WIDTH, t), BF16),
        jax.ShapeDtypeStruct((MLSTM_WIDTH, t), BF16),
        jax.ShapeDtypeStruct((NUM_GATES, t), F32),
    ]
    out_specs = [
        row(MOBA_WIDTH),
        pl.BlockSpec((1, nblk, MOBA_WIDTH), lambda i: (i, 0, 0)),
        row(SWA_KV_WIDTH),
        row(2 * MLSTM_WIDTH),
        col(MOBA_WIDTH),
        col(SWA_Q_WIDTH),
        pl.BlockSpec((nblk, MOBA_HEADS * MOBA_V_ROWS, MOBA_BLOCK), lambda i: (i, 0, 0)),
        col(SWA_KV_WIDTH),
        col(MLSTM_WIDTH),
        col(MLSTM_WIDTH),
        col(NUM_GATES),
    ]
    weights = 2 * (wnat.size + wtr.size)
    tiles = tm * (2 * 4 * D_MODEL + 3 * 4 * (NAT_WIDTH + TR_ROWS) + 2 * 2 * (NAT_WIDTH + TR_ROWS))
    return pl.pallas_call(
        _in_proj_kernel,
        grid=(steps,),
        in_specs=in_specs,
        out_specs=out_specs,
        out_shape=out_shape,
        compiler_params=pltpu.CompilerParams(
            dimension_semantics=("arbitrary",),
            vmem_limit_bytes=_vmem_limit(2 * weights + tiles)),
        name="in_proj",
    )(xf, ln1, wnat, wtr, bd, gk, cosn, sinn, gq, cost, sint, gbias)


def _moba_kernel(bounded_ref, qt_ref, k_ref, vt_ref, km_ref, o_ref, sel_ref, s_ref, p_ref):
    nb = k_ref.shape[0]
    lq = MOBA_BLOCK
    nh = MOBA_Q_BLOCKS * MOBA_HEADS
    step = pl.program_id(1)
    row = lax.broadcasted_iota(jnp.int32, (HEAD_PAIR, lq), 0)
    own, qh = [], []
    for s in range(nh):
        qb, h = divmod(s, MOBA_HEADS)
        own.append(step * MOBA_Q_BLOCKS + qb)
        qt = qt_ref[(h // 2) * HEAD_PAIR:(h // 2 + 1) * HEAD_PAIR, qb * lq:(qb + 1) * lq]
        keep = (row < HEAD_DIM) if h % 2 == 0 else (row >= HEAD_DIM)
        qh.append(jnp.where(keep, qt, jnp.zeros_like(qt)))
    last_own = step * MOBA_Q_BLOCKS + (MOBA_Q_BLOCKS - 1)

    def k_group(blk0, count, s):
        g = (s % MOBA_HEADS) // 2
        return k_ref[pl.ds(blk0, count), :, g * HEAD_PAIR:(g + 1) * HEAD_PAIR].reshape(count * lq, HEAD_PAIR)

    def v_aug(j, s):
        h = s % MOBA_HEADS
        return vt_ref[j, h * MOBA_V_ROWS:(h + 1) * MOBA_V_ROWS, :]

    def select_blocks():
        gates = []
        for s in range(nh):
            g = (s % MOBA_HEADS) // 2
            km_hi, km_lo = _split_bf16(km_ref[:, g * HEAD_PAIR:(g + 1) * HEAD_PAIR])
            gates.append(_dot(km_hi, qh[s]) + _dot(km_lo, qh[s]))
        gate = jnp.concatenate(gates, axis=1)
        own_all = jnp.concatenate([jnp.full((1, lq), own[s], jnp.int32) for s in range(nh)], axis=1)
        blk_all = lax.broadcasted_iota(jnp.int32, gate.shape, 0)
        gate = jnp.where(blk_all < own_all, gate, NEG_INF)
        sel = jnp.full(gate.shape, NEG_INF, F32)
        for _ in range(MOBA_TOPK):
            mx = jnp.max(gate, axis=0, keepdims=True)
            idx = jnp.min(jnp.where(gate == mx, blk_all, nb), axis=0, keepdims=True)
            pick = blk_all == jnp.where(idx < own_all, idx, nb)
            sel = jnp.where(pick, 0.0, sel)
            gate = jnp.where(pick, NEG_INF, gate)
        for s in range(nh):
            sel_ref[s] = sel[:, s * lq:(s + 1) * lq]

    kpos = lax.broadcasted_iota(jnp.int32, (lq, lq), 0)
    qpos = lax.broadcasted_iota(jnp.int32, (lq, lq), 1)
    causal = kpos <= qpos

    def weighted_values(st, mu, v_rows):
        pv = _dot(v_rows, jnp.exp2(st - mu).astype(BF16))
        return pv[HEAD_DIM:HEAD_DIM + 1], pv[:HEAD_DIM]

    def finish(nums, dens):
        for s in range(0, nh, 2):
            qb, h = divmod(s, MOBA_HEADS)
            ot = jnp.concatenate([nums[s] / dens[s], nums[s + 1] / dens[s + 1]], axis=0)
            o_ref[qb * lq:(qb + 1) * lq, (h // 2) * HEAD_PAIR:(h // 2 + 1) * HEAD_PAIR] = ot.T.astype(BF16)

    trips = lax.shift_right_logical(last_own + (2 * MOBA_GROUP - 1), MOBA_GROUP.bit_length())
    units = nh * MOBA_GROUP
    own_slot = 2 * units

    @pl.when(bounded_ref[0] != 0)
    def _():
        select_blocks()
        for h in range(nh):
            st = jnp.where(causal, _dot(k_group(own[h], 1, h), qh[h]), NEG_INF)
            p_ref[own_slot + h] = jnp.exp2(st).astype(BF16)

        def stage_p(blk0, slot0):
            blk0 = jnp.minimum(blk0, nb - MOBA_GROUP)
            for h in range(nh):
                st = _dot(k_group(blk0, MOBA_GROUP, h), qh[h])
                sel = jnp.concatenate(
                    [jnp.broadcast_to(sel_ref[h, pl.ds(blk0 + u, 1), :], (lq, lq)) for u in range(MOBA_GROUP)],
                    axis=0)
                slot = slot0 + MOBA_GROUP * h
                p_ref[slot:slot + MOBA_GROUP] = jnp.exp2(st + sel).astype(BF16).reshape(MOBA_GROUP, lq, lq)

        def stage_v(blk0, slot0, acc):
            acc = list(acc)
            for h in range(nh):
                slot = slot0 + MOBA_GROUP * h
                p = p_ref[slot:slot + MOBA_GROUP].reshape(MOBA_GROUP * lq, lq)
                v = jnp.concatenate([v_aug(blk0 + u, h) for u in range(MOBA_GROUP)], axis=1)
                acc[h] = acc[h] + _dot(v, p)
            return acc

        def trip(t, acc, last):
            blk = t * (2 * MOBA_GROUP)
            stage_p(blk + MOBA_GROUP, units)
            acc = stage_v(blk, 0, acc)
            if not last:
                stage_p(blk + 2 * MOBA_GROUP, 0)
            return tuple(stage_v(blk + MOBA_GROUP, units, acc))

        stage_p(0, 0)
        acc = tuple(_dot(v_aug(own[h], h), p_ref[own_slot + h]) for h in range(nh))
        acc = lax.fori_loop(0, jnp.maximum(trips - 1, 0), lambda t, a: trip(t, a, False), acc)
        acc = lax.cond(trips > 0, lambda a: trip(trips - 1, a, True), lambda a: a, acc)
        finish([a[:HEAD_DIM] for a in acc], [a[HEAD_DIM:HEAD_DIM + 1] for a in acc])


    def stage_a(blk0, slot0):
        mus = []
        for u in range(MOBA_GROUP):
            for h in range(nh):
                st = _dot(k_group(jnp.minimum(blk0 + u, nb - 1), 1, h), qh[h])
                s_ref[slot0 + nh * u + h] = st
                mus.append(jnp.max(st, axis=0, keepdims=True))
        return mus

    def stage_b(blk0, slot0, mus, state):
        state = list(state)
        for u in range(MOBA_GROUP):
            j = blk0 + u
            for h in range(nh):
                mu = mus[nh * u + h]
                ls, pv = weighted_values(s_ref[slot0 + nh * u + h], mu, v_aug(j, h))
                m, l, acc = state[3 * h:3 * h + 3]
                mu_sel = mu + sel_ref[h, pl.ds(j, 1), :]
                m_new = jnp.maximum(m, mu_sel)
                f = jnp.exp2(mu_sel - m_new)
                c = jnp.exp2(m - m_new)
                state[3 * h:3 * h + 3] = [m_new, c * l + f * ls, c * acc + f * pv]
        return state

    def body(t, loop_carry):
        mus0, state = loop_carry[:units], loop_carry[units:]
        blk = t * (2 * MOBA_GROUP)
        mus1 = stage_a(blk + MOBA_GROUP, units)
        state = stage_b(blk, 0, mus0, state)
        mus0 = stage_a(blk + 2 * MOBA_GROUP, 0)
        state = stage_b(blk + MOBA_GROUP, units, mus1, state)
        return (*mus0, *state)

    @pl.when(bounded_ref[0] == 0)
    def _():
        state = []
        for h in range(nh):
            st = jnp.where(causal, _dot(k_group(own[h], 1, h), qh[h]), NEG_INF)
            mu = jnp.max(st, axis=0, keepdims=True)
            state += [mu, *weighted_values(st, mu, v_aug(own[h], h))]
        mus0 = stage_a(0, 0)
        select_blocks()
        state = lax.fori_loop(0, trips, body, (*mus0, *state))[units:]
        finish(state[2::3], state[1::3])


def _moba(bounded, mqt, mk, mvt, kmean, batch, seq):
    t = mk.shape[0]
    nb = seq // MOBA_BLOCK
    lq = MOBA_BLOCK
    k3 = mk.reshape(t // MOBA_BLOCK, MOBA_BLOCK, MOBA_WIDTH)
    assert nb % (2 * MOBA_GROUP) == 0 and nb % MOBA_Q_BLOCKS == 0
    streams = MOBA_Q_BLOCKS * MOBA_HEADS
    units = streams * MOBA_GROUP
    steps = nb // MOBA_Q_BLOCKS
    kv_bytes = 2 * nb * MOBA_BLOCK * (MOBA_WIDTH + MOBA_HEADS * MOBA_V_ROWS)
    scratch_bytes = lq * lq * (4 * 2 * units + 2 * (2 * units + streams)) + 4 * streams * nb * lq
    resident = lambda shape, imap: pl.BlockSpec(shape, imap, pipeline_mode=pl.Buffered(1))
    return pl.pallas_call(
        _moba_kernel,
        grid=(batch, steps),
        in_specs=[
            pl.BlockSpec(memory_space=pltpu.SMEM),
            pl.BlockSpec((MOBA_WIDTH, MOBA_Q_BLOCKS * lq), lambda b, i: (0, b * steps + i)),
            resident((nb, MOBA_BLOCK, MOBA_WIDTH), lambda b, i: (b, 0, 0)),
            resident((nb, MOBA_HEADS * MOBA_V_ROWS, MOBA_BLOCK), lambda b, i: (b, 0, 0)),
            resident((nb, MOBA_WIDTH), lambda b, i: (b, 0)),
        ],
        out_specs=pl.BlockSpec((MOBA_Q_BLOCKS * lq, MOBA_WIDTH), lambda b, i: (b * steps + i, 0)),
        out_shape=jax.ShapeDtypeStruct((t, MOBA_WIDTH), BF16),
        scratch_shapes=[pltpu.VMEM((streams, nb, lq), F32),
                        pltpu.VMEM((2 * units, lq, lq), F32),
                        pltpu.VMEM((2 * units + streams, lq, lq), BF16)],
        compiler_params=pltpu.CompilerParams(
            dimension_semantics=("arbitrary", "arbitrary"),
            vmem_limit_bytes=_vmem_limit(kv_bytes + scratch_bytes + VMEM_HEADROOM_BYTES)),
        name="moba",
    )(bounded, mqt, k3, mvt, kmean)


def _swa_kernel(bounded_ref, sink_ref, qt_ref, k_ref, kh_ref, vt_ref, vth_ref, o_ref, p_ref):
    tq = k_ref.shape[0]
    w = SWA_WINDOW
    i = pl.program_id(1)
    ncol = SWA_GROUP * w
    kpos = lax.broadcasted_iota(jnp.int32, (2 * w, ncol), 0)
    col = lax.broadcasted_iota(jnp.int32, (2 * w, ncol), 1)
    diff = w + (col % w) - kpos
    in_window = (diff >= 0) & (diff < w)
    zeros = jnp.zeros((HEAD_DIM, ncol), BF16)
    ones = jnp.ones((MOBA_V_ROWS - HEAD_DIM, 2 * w), BF16)

    def attend(shifted):
        sink_terms = []
        for r in range(tq // w):
            if r == 0:
                kcat = jnp.concatenate([kh_ref[...], k_ref[0:w, :]], axis=0)
                mask = in_window & ((kpos >= w) | (i > 0))
            else:
                kcat = k_ref[(r - 1) * w:(r + 1) * w, :]
                mask = in_window
            for g in range(SWA_KV_HEADS):
                qs = jnp.concatenate(
                    [qt_ref[(SWA_GROUP * g + a) * HEAD_DIM:(SWA_GROUP * g + a + 1) * HEAD_DIM, r * w:(r + 1) * w]
                     for a in range(SWA_GROUP)], axis=1)
                qpad = jnp.concatenate([qs, zeros] if g == 0 else [zeros, qs], axis=0)
                st = jnp.where(mask, _dot(kcat, qpad), NEG_INF)
                sink = jnp.concatenate(
                    [jnp.full((1, w), sink_ref[SWA_GROUP * g + a] * LOG2E, F32) for a in range(SWA_GROUP)],
                    axis=1)
                if shifted:
                    m = jnp.maximum(jnp.max(st, axis=0, keepdims=True), sink)
                    st, sink = st - m, sink - m
                p_ref[r * SWA_KV_HEADS + g] = jnp.exp2(st).astype(BF16)
                sink_terms.append(jnp.exp2(sink))
        for r in range(tq // w):
            if r == 0:
                vcat = jnp.concatenate([vth_ref[...], vt_ref[:, 0:w]], axis=1)
            else:
                vcat = vt_ref[:, (r - 1) * w:(r + 1) * w]
            pieces = []
            for g in range(SWA_KV_HEADS):
                v_aug = jnp.concatenate([vcat[g * HEAD_DIM:(g + 1) * HEAD_DIM, :], ones], axis=0)
                pv = _dot(v_aug, p_ref[r * SWA_KV_HEADS + g])
                ot = pv[:HEAD_DIM] / (pv[HEAD_DIM:HEAD_DIM + 1] + sink_terms[r * SWA_KV_HEADS + g])
                pieces += [ot[:, a * w:(a + 1) * w] for a in range(SWA_GROUP)]
            o_ref[r * w:(r + 1) * w, :] = jnp.concatenate(pieces, axis=0).T.astype(BF16)

    @pl.when(bounded_ref[0] != 0)
    def _():
        attend(False)

    @pl.when(bounded_ref[0] == 0)
    def _():
        attend(True)


def _swa(bounded, sinks, sqt, sk, svt, batch, seq):
    t = sk.shape[0]
    tq = SWA_ROWS
    w = SWA_WINDOW
    steps = seq // tq
    halo = lambda b, i: jnp.maximum((b * seq + i * tq) // w - 1, 0)
    return pl.pallas_call(
        _swa_kernel,
        grid=(batch, steps),
        in_specs=[
            pl.BlockSpec(memory_space=pltpu.SMEM),
            pl.BlockSpec(memory_space=pltpu.SMEM),
            pl.BlockSpec((SWA_Q_WIDTH, tq), lambda b, i: (0, b * steps + i)),
            pl.BlockSpec((tq, SWA_KV_WIDTH), lambda b, i: (b * steps + i, 0)),
            pl.BlockSpec((w, SWA_KV_WIDTH), lambda b, i: (halo(b, i), 0)),
            pl.BlockSpec((SWA_KV_WIDTH, tq), lambda b, i: (0, b * steps + i)),
            pl.BlockSpec((SWA_KV_WIDTH, w), lambda b, i: (0, halo(b, i))),
        ],
        out_specs=pl.BlockSpec((tq, SWA_Q_WIDTH), lambda b, i: (b * steps + i, 0)),
        out_shape=jax.ShapeDtypeStruct((t, SWA_Q_WIDTH), BF16),
        scratch_shapes=[pltpu.VMEM(((tq // w) * SWA_KV_HEADS, 2 * w, SWA_GROUP * w), BF16)],
        compiler_params=pltpu.CompilerParams(dimension_semantics=("arbitrary", "arbitrary")),
        name="swa",
    )(bounded, sinks, sqt, sk, sk, svt, svt)


def _log_sigmoid(v):
    return jnp.minimum(v, 0.0) - jnp.log1p(jnp.exp(-jnp.abs(v)))


def _mlstm_kernel(xqk_ref, halo_ref, xvt_ref, xot_ref, gt_ref, gc_ref, convw_ref, convb_ref,
                  gain_ref, cum_ref, cumt_ref, o_ref, ext_ref, c_ref, n_ref, m_ref,
                  qm_ref, k_ref, pv_ref, row_ref, kv_ref, nk_ref, cprev_ref, nprev_ref, ht_ref):
    tt = xqk_ref.shape[0]
    lc = MLSTM_CHUNK
    i = pl.program_id(1)

    @pl.when(i == 0)
    def _():
        c_ref[...] = jnp.zeros_like(c_ref)
        n_ref[...] = jnp.zeros_like(n_ref)
        m_ref[...] = jnp.zeros_like(m_ref)

    hist = halo_ref[...].astype(F32)
    ext_ref[0:CONV_HALO, :] = jnp.where(i > 0, hist, jnp.zeros_like(hist))
    ext_ref[CONV_HALO:CONV_HALO + tt, :] = xqk_ref[...].astype(F32)
    convw = convw_ref[...]
    conv = jnp.zeros((tt, 2 * MLSTM_WIDTH), F32) + convb_ref[...]
    for j in range(CONV_WIDTH):
        start = CONV_HALO - (CONV_WIDTH - 1) + j
        conv = conv + convw[j:j + 1, :] * ext_ref[start:start + tt, :]
    qk = conv * jax.nn.sigmoid(conv)
    q_all = qk[:, :MLSTM_WIDTH]
    lane_q = lax.broadcasted_iota(jnp.int32, q_all.shape, 1)
    even_head = (lane_q % HEAD_PAIR) < HEAD_DIM
    qm_ref[0] = jnp.where(even_head, q_all, 0.0).astype(BF16)
    qm_ref[1] = jnp.where(even_head, 0.0, q_all).astype(BF16)
    k_ref[...] = (qk[:, MLSTM_WIDTH:] * SM_SCALE).astype(BF16)

    g_row = gt_ref[...]
    g_col = gc_ref[...]
    cum = cum_ref[...]
    cumt = cumt_ref[...]
    b_row = sum(_dot(piece, cum) for piece in _split3_bf16(_log_sigmoid(g_row)))
    b_col = sum(_dot(cumt, piece) for piece in _split3_bf16(_log_sigmoid(g_col)))
    u_col = b_col[:, MLSTM_HEADS:] - g_col[:, :MLSTM_HEADS]

    lane = lax.broadcasted_iota(jnp.int32, (HEAD_DIM, HEAD_PAIR), 1)
    s_idx = lax.broadcasted_iota(jnp.int32, (lc, lc), 0)
    t_idx = lax.broadcasted_iota(jnp.int32, (lc, lc), 1)
    causal = s_idx <= t_idx
    nchunks = tt // lc
    heads = range(MLSTM_HEADS)
    in_head = [(lane < HEAD_DIM), (lane >= HEAD_DIM)]

    def operands(c, h):
        p, hh = divmod(h, 2)
        sl = slice(c * lc, (c + 1) * lc)
        qm = qm_ref[hh, sl, p * HEAD_PAIR:(p + 1) * HEAD_PAIR]
        kp = k_ref[sl, p * HEAD_PAIR:(p + 1) * HEAD_PAIR]
        return sl, hh, qm, kp

    m_state = [m_ref[h][0:1, 0:lc] for h in heads]
    for c in range(nchunks):
        for h in heads:
            u = c * MLSTM_HEADS + h
            sl, hh, qm, kp = operands(c, h)
            vt = xvt_ref[h * HEAD_DIM:(h + 1) * HEAD_DIM, sl]
            br = b_row[MLSTM_HEADS + h:MLSTM_HEADS + h + 1, sl]
            ir = g_row[h:h + 1, sl]
            uc = u_col[sl, h:h + 1]
            a = br[:, lc - 1:lc]
            dt = jnp.where(causal, br - uc, NEG_INF)
            inter = br + m_state[h]
            m_t = jnp.maximum(inter, jnp.max(dt, axis=0, keepdims=True))
            w_inter = jnp.exp(inter - m_t)
            qkt = _dot_nt(kp, qm) * jnp.exp(dt - m_t)
            pv_ref[u] = _dot(vt, qkt.astype(BF16))
            row_ref[u, 0:1, 0:lc] = w_inter
            row_ref[u, 1:2, 0:lc] = jnp.sum(qkt, axis=0, keepdims=True)
            row_ref[u, 2:3, 0:lc] = jnp.exp(-m_t)
            g_end = a - br + ir
            m_new = jnp.maximum(a + m_state[h], jnp.max(g_end, axis=1, keepdims=True))
            w_s = jnp.exp(g_end - m_new)
            decay = jnp.exp(a + m_state[h] - m_new)
            row_ref[u, 3:4, 0:lc] = decay
            vtw = (vt.astype(F32) * w_s).astype(BF16)
            kv_ref[u] = jnp.where(in_head[hh], _dot(vtw, kp), 0.0)
            w_rows = jnp.broadcast_to(w_s, (8, lc)).astype(BF16)
            nk_ref[u] = jnp.where(in_head[hh][0:8], _dot(w_rows, kp), 0.0)
            m_state[h] = m_new

    for h in heads:
        c_state = c_ref[h]
        n_state = n_ref[h]
        for c in range(nchunks):
            u = c * MLSTM_HEADS + h
            cprev_ref[u] = c_state.astype(BF16)
            nprev_ref[u] = n_state
            decay = row_ref[u, 3:4, 0:HEAD_PAIR]
            c_state = decay * c_state + kv_ref[u]
            n_state = decay * n_state + nk_ref[u]
        c_ref[h] = c_state
        n_ref[h] = n_state
        m_ref[h] = jnp.broadcast_to(m_state[h][:, 0:1], m_ref.shape[1:])

    for c in range(nchunks):
        for h in heads:
            u = c * MLSTM_HEADS + h
            sl, hh, qm, _ = operands(c, h)
            w_inter = row_ref[u, 0:1, 0:lc]
            num = w_inter * _dot_nt(cprev_ref[u], qm) + pv_ref[u]
            den = w_inter * _dot_nt(nprev_ref[u].astype(BF16), qm)[0:1, :] + row_ref[u, 1:2, 0:lc]
            ht_ref[h * HEAD_DIM:(h + 1) * HEAD_DIM, sl] = num / jnp.maximum(jnp.abs(den), row_ref[u, 2:3, 0:lc])

    gain = jnp.concatenate([gain_ref[...]] * (tt // LANES_V7X), axis=1)
    outs = []
    for h in heads:
        rows = slice(h * HEAD_DIM, (h + 1) * HEAD_DIM)
        hg = ht_ref[rows, :] * jax.nn.sigmoid(xot_ref[rows, :].astype(F32))
        msq = jnp.mean(hg * hg, axis=0, keepdims=True)
        outs.append(hg * lax.rsqrt(msq + NORM_EPS) * gain[rows])
    o_ref[...] = jnp.concatenate(outs, axis=0).T.astype(BF16)


def _mlstm(xqk, xvt, xot, gt, gc, convw, convb, gain, cum, batch, seq):
    t = xqk.shape[0]
    tt = MLSTM_ROWS
    steps = seq // tt
    units = MLSTM_HEADS * (tt // MLSTM_CHUNK)
    col = lambda r: pl.BlockSpec((r, tt), lambda b, i: (0, b * steps + i))
    return pl.pallas_call(
        _mlstm_kernel,
        grid=(batch, steps),
        in_specs=[
            pl.BlockSpec((tt, 2 * MLSTM_WIDTH), lambda b, i: (b * steps + i, 0)),
            pl.BlockSpec((CONV_HALO, 2 * MLSTM_WIDTH),
                         lambda b, i: (jnp.maximum((b * seq + i * tt) // CONV_HALO - 1, 0), 0)),
            col(MLSTM_WIDTH),
            col(MLSTM_WIDTH),
            col(NUM_GATES),
            pl.BlockSpec((tt, NUM_GATES), lambda b, i: (b * steps + i, 0)),
            _const_spec(convw.shape),
            _const_spec(convb.shape),
            _const_spec(gain.shape),
            _const_spec(cum.shape),
            _const_spec(cum.shape),
        ],
        out_specs=pl.BlockSpec((tt, MLSTM_WIDTH), lambda b, i: (b * steps + i, 0)),
        out_shape=jax.ShapeDtypeStruct((t, MLSTM_WIDTH), BF16),
        scratch_shapes=[
            pltpu.VMEM((CONV_HALO + tt, 2 * MLSTM_WIDTH), F32),
            pltpu.VMEM((MLSTM_HEADS, HEAD_DIM, HEAD_PAIR), F32),
            pltpu.VMEM((MLSTM_HEADS, 8, HEAD_PAIR), F32),
            pltpu.VMEM((MLSTM_HEADS, 8, MLSTM_CHUNK), F32),
            pltpu.VMEM((2, tt, MLSTM_WIDTH), BF16),
            pltpu.VMEM((tt, MLSTM_WIDTH), BF16),
            pltpu.VMEM((units, HEAD_DIM, MLSTM_CHUNK), F32),
            pltpu.VMEM((units, 8, MLSTM_CHUNK), F32),
            pltpu.VMEM((units, HEAD_DIM, HEAD_PAIR), F32),
            pltpu.VMEM((units, 8, HEAD_PAIR), F32),
            pltpu.VMEM((units, HEAD_DIM, HEAD_PAIR), BF16),
            pltpu.VMEM((units, 8, HEAD_PAIR), F32),
            pltpu.VMEM((MLSTM_WIDTH, tt), F32),
        ],
        compiler_params=pltpu.CompilerParams(dimension_semantics=("arbitrary", "arbitrary")),
        name="mlstm",
    )(xqk, xqk, xvt, xot, gt, gc, convw, convb, gain, cum, cum.T)


def _out_mlp_kernel(x_ref, ym_ref, yl_ref, ys_ref, wo_ref, ln2_ref, wup_ref, wdn_ref, o_ref):
    y = jnp.concatenate([ym_ref[...], yl_ref[...], ys_ref[...]], axis=1)
    x1 = x_ref[...] + _dot(y, wo_ref[...])
    ms = jnp.mean(x1 * x1, axis=-1, keepdims=True)
    hn = (x1 * lax.rsqrt(ms + NORM_EPS) * ln2_ref[...]).astype(BF16)
    o_ref[...] = x1
    for c in range(0, D_FF, MLP_FF_CHUNK):
        u = _dot(hn, wup_ref[:, c:c + MLP_FF_CHUNK])
        act = jnp.square(jnp.maximum(u, 0.0)).astype(BF16)
        o_ref[...] += _dot(act, wdn_ref[c:c + MLP_FF_CHUNK, :])


def _out_mlp(xf, ym, yl, ys, wo, ln2, wup, wdn):
    t = xf.shape[0]
    tm = OUT_MLP_ROWS
    row = lambda w: pl.BlockSpec((tm, w), lambda i: (i, 0))
    weights = 2 * (wo.size + wup.size + wdn.size)
    tiles = tm * (4 * 4 * D_MODEL + 2 * 2 * D_MODEL + 4 * 4 * D_MODEL + 6 * MLP_FF_CHUNK)
    return pl.pallas_call(
        _out_mlp_kernel,
        grid=(t // tm,),
        in_specs=[
            row(D_MODEL), row(MOBA_WIDTH), row(MLSTM_WIDTH), row(SWA_Q_WIDTH),
            _const_spec(wo.shape), _const_spec(ln2.shape), _const_spec(wup.shape), _const_spec(wdn.shape),
        ],
        out_specs=row(D_MODEL),
        out_shape=jax.ShapeDtypeStruct((t, D_MODEL), F32),
        compiler_params=pltpu.CompilerParams(
            dimension_semantics=("arbitrary",),
            vmem_limit_bytes=_vmem_limit(weights + tiles)),
        name="out_mlp",
    )(xf, ym, yl, ys, wo, ln2, wup, wdn)


def _rope_tables(seq):
    inv = ROPE_THETA ** (-jnp.arange(0, HEAD_DIM, 2, dtype=F32) / HEAD_DIM)
    ang = jnp.arange(seq, dtype=F32)[:, None] * inv[None, :]
    cos, sin = jnp.cos(ang), jnp.sin(ang)
    cosn = jnp.concatenate([cos, cos, cos, cos], axis=1)
    sinn = jnp.concatenate([-sin, sin, -sin, sin], axis=1)
    return cosn, sinn, cos.T, sin.T


def _layer(xf, tables, consts, batch, seq, ln1, w_in, conv_w, conv_b, igate_b, fgate_b, mlstm_norm,
           moba_q_norm, moba_k_norm, swa_q_norm, swa_k_norm, swa_sinks, w_out, ln2, w_up, w_down):
    cosn, sinn, cost, sint = tables
    bd, cum = consts
    o = 0
    cols = {}
    for name, width in (("mq", MOBA_WIDTH), ("mk", MOBA_WIDTH), ("mv", MOBA_WIDTH), ("sq", SWA_Q_WIDTH),
                        ("sk", SWA_KV_WIDTH), ("sv", SWA_KV_WIDTH), ("xqk", 2 * MLSTM_WIDTH),
                        ("xv", MLSTM_WIDTH), ("xo", MLSTM_WIDTH), ("xi", MLSTM_HEADS), ("xf", MLSTM_HEADS)):
        cols[name] = w_in[:, o:o + width]
        o += width
    wnat = jnp.concatenate([cols["mk"], cols["sk"], cols["xqk"]], axis=1).astype(BF16)
    wtr = jnp.concatenate([cols["mq"], cols["sq"], cols["mv"], cols["sv"], cols["xv"], cols["xo"],
                           cols["xi"], cols["xf"]], axis=1).T.astype(BF16)
    gk = jnp.concatenate([jnp.tile(moba_k_norm, MOBA_HEADS), jnp.tile(swa_k_norm, SWA_KV_HEADS)])[None, :]
    gq = jnp.broadcast_to(
        (jnp.concatenate([moba_q_norm, swa_q_norm]) * (SM_SCALE * LOG2E))[:, None], (2 * HEAD_DIM, LANES_V7X))
    gbias = jnp.broadcast_to(jnp.concatenate([igate_b, fgate_b])[:, None], (NUM_GATES, LANES_V7X))

    (mk, kmean, sk, xqk, mqt, sqt, mvt, svt, xvt, xot, gt) = _in_proj(
        xf, ln1[None, :], wnat, wtr, bd, gk, cosn, sinn, gq, cost, sint, gbias, seq)

    kmean = kmean.reshape(-1, MOBA_WIDTH)
    def bounded(q_gain, k_gain, *extra):
        bound = (HEAD_DIM * SM_SCALE * LOG2E) * jnp.max(jnp.abs(q_gain)) * jnp.max(jnp.abs(k_gain))
        for e in extra:
            bound = jnp.maximum(bound, jnp.max(jnp.abs(e)) * LOG2E)
        return (bound <= MOBA_SAFE_LOG2).astype(jnp.int32).reshape(1)

    ym = _moba(bounded(moba_q_norm, moba_k_norm), mqt, mk, mvt, kmean, batch, seq)
    ys = _swa(bounded(swa_q_norm, swa_k_norm, swa_sinks), swa_sinks, sqt, sk, svt, batch, seq)
    gain = jnp.broadcast_to(mlstm_norm.reshape(MLSTM_WIDTH, 1), (MLSTM_WIDTH, LANES_V7X))
    yl = _mlstm(xqk, xvt, xot, gt, gt.T, conv_w, conv_b[None, :], gain, cum, batch, seq)

    return _out_mlp(xf, ym, yl, ys, w_out.astype(BF16), ln2[None, :], w_up.astype(BF16), w_down.astype(BF16))


def kernel(x, ln1, w_in, conv_w, conv_b, igate_b, fgate_b, mlstm_norm, moba_q_norm, moba_k_norm,
           swa_q_norm, swa_k_norm, swa_sinks, w_out, ln2, w_up, w_down):
    batch, seq, d = x.shape
    assert d == D_MODEL and seq % max(IN_PROJ_ROWS, SWA_ROWS, MLSTM_ROWS, MOBA_BLOCK) == 0
    depth = ln1.shape[0]
    tables = _rope_tables(seq)
    bw = 2 * LANES_V7X
    ids = jnp.arange(bw) // HEAD_DIM
    bd = jnp.where(ids[:, None] == ids[None, :], 1.0 / HEAD_DIM, 0.0).astype(BF16)
    tids = jnp.arange(MLSTM_ROWS)
    cum = ((tids[:, None] // MLSTM_CHUNK == tids[None, :] // MLSTM_CHUNK)
           & (tids[:, None] <= tids[None, :])).astype(BF16)
    xf = x.reshape(batch * seq, d)
    for l in range(depth):
        xf = _layer(xf, tables, (bd, cum), batch, seq, ln1[l], w_in[l], conv_w[l], conv_b[l], igate_b[l],
                    fgate_b[l], mlstm_norm[l], moba_q_norm[l], moba_k_norm[l], swa_q_norm[l],
                    swa_k_norm[l], swa_sinks[l], w_out[l], ln2[l], w_up[l], w_down[l])
    return xf.reshape(batch, seq, d)
```

```python
import jax
import jax.numpy as jnp
from jax import lax
from jax.experimental import pallas as pl
from jax.experimental.pallas import tpu as pltpu

F32 = jnp.float32
BF16 = jnp.bfloat16
NEG_INF = float("-inf")

D_MODEL = 1024
HEAD_DIM = 64
MOBA_HEADS = 6
MLSTM_HEADS = 4
SWA_Q_HEADS = 6
SWA_KV_HEADS = 2
SWA_GROUP = SWA_Q_HEADS // SWA_KV_HEADS
MOBA_WIDTH = MOBA_HEADS * HEAD_DIM
MLSTM_WIDTH = MLSTM_HEADS * HEAD_DIM
SWA_Q_WIDTH = SWA_Q_HEADS * HEAD_DIM
SWA_KV_WIDTH = SWA_KV_HEADS * HEAD_DIM
MOBA_BLOCK = 256
MOBA_TOPK = 3
MLSTM_CHUNK = 256
CONV_WIDTH = 4
SWA_WINDOW = 128
ROPE_THETA = 10000.0
D_FF = 4 * D_MODEL
NORM_EPS = 1e-6
SM_SCALE = HEAD_DIM ** -0.5
LOG2E = 1.4426950408889634
MOBA_SAFE_LOG2 = 60.0

LANES_V7X = 128
BF16_TILE_ROWS_V7X = 16
VMEM_BYTES_V7X = 64 * 1024 * 1024
VMEM_HEADROOM_BYTES = 4 * 1024 * 1024
VMEM_TEMPS_BYTES = 8 * 1024 * 1024
HEAD_PAIR = 2 * HEAD_DIM
assert HEAD_PAIR == LANES_V7X
MOBA_V_ROWS = HEAD_DIM + BF16_TILE_ROWS_V7X

NAT_WIDTH = MOBA_WIDTH + SWA_KV_WIDTH + 2 * MLSTM_WIDTH
KN_WIDTH = MOBA_WIDTH + SWA_KV_WIDTH
QT_ROWS = MOBA_WIDTH + SWA_Q_WIDTH
TR_ROWS = QT_ROWS + MOBA_WIDTH + SWA_KV_WIDTH + 2 * MLSTM_WIDTH + 2 * MLSTM_HEADS
NUM_GATES = 2 * MLSTM_HEADS

IN_PROJ_ROWS = 1024
OUT_MLP_ROWS = 1024
MLP_FF_CHUNK = 1024
SWA_ROWS = 2048
MLSTM_ROWS = 256
MOBA_GROUP = 2
MOBA_Q_BLOCKS = 1
CONV_HALO = BF16_TILE_ROWS_V7X


def _dot(a, b):
    return jnp.dot(a, b, preferred_element_type=F32)


def _dot_nt(a, b):
    return lax.dot_general(a, b, (((1,), (1,)), ((), ())), preferred_element_type=F32)


def _split_bf16(v):
    hi = v.astype(BF16)
    lo = (v - hi.astype(F32)).astype(BF16)
    return hi, lo


def _split3_bf16(v):
    hi = v.astype(BF16)
    rest = v - hi.astype(F32)
    mid = rest.astype(BF16)
    return hi, mid, (rest - mid.astype(F32)).astype(BF16)


def _vmem_limit(nbytes):
    return int(min(nbytes + VMEM_TEMPS_BYTES, VMEM_BYTES_V7X - VMEM_HEADROOM_BYTES))


def _const_spec(shape):
    nd = len(shape)
    return pl.BlockSpec(shape, lambda *_: (0,) * nd, pipeline_mode=pl.Buffered(1))


def _in_proj_kernel(x_ref, ln1_ref, wnat_ref, wtr_ref, bd_ref, gk_ref, cosn_ref, sinn_ref,
                    gq_ref, cost_ref, sint_ref, gbias_ref,
                    mk_ref, kmean_ref, sk_ref, xqk_ref, mqt_ref, sqt_ref, mvt_ref, svt_ref,
                    xvt_ref, xot_ref, gt_ref):
    tm = x_ref.shape[0]
    x = x_ref[...]
    ms = jnp.mean(x * x, axis=-1, keepdims=True)
    hn = (x * lax.rsqrt(ms + NORM_EPS) * ln1_ref[...]).astype(BF16)
    nat = _dot(hn, wnat_ref[...])

    kk = nat[:, :KN_WIDTH]
    hi, lo = _split_bf16(kk * kk)
    bd = bd_ref[...]
    bw = bd.shape[0]
    msk = jnp.concatenate(
        [_dot(hi[:, c:c + bw], bd) + _dot(lo[:, c:c + bw], bd) for c in range(0, KN_WIDTH, bw)],
        axis=1)
    kn = kk * lax.rsqrt(msk + NORM_EPS) * gk_ref[...]
    reps = KN_WIDTH // LANES_V7X
    cosn = jnp.concatenate([cosn_ref[...]] * reps, axis=1)
    sinn = jnp.concatenate([sinn_ref[...]] * reps, axis=1)
    lane = lax.broadcasted_iota(jnp.int32, kn.shape, 1)
    first_half = (lane % HEAD_DIM) < (HEAD_DIM // 2)
    swapped = jnp.where(first_half,
                        pltpu.roll(kn, KN_WIDTH - HEAD_DIM // 2, 1),
                        pltpu.roll(kn, HEAD_DIM // 2, 1))
    kr = kn * cosn + swapped * sinn
    mk = kr[:, :MOBA_WIDTH]
    mk_ref[...] = mk.astype(BF16)
    nblk = tm // MOBA_BLOCK
    kmean_ref[0] = jnp.concatenate(
        [jnp.mean(mk[c * MOBA_BLOCK:(c + 1) * MOBA_BLOCK], axis=0, keepdims=True) for c in range(nblk)],
        axis=0)
    sk_ref[...] = kr[:, MOBA_WIDTH:].astype(BF16)
    xqk_ref[...] = nat[:, KN_WIDTH:].astype(BF16)

    tr = _dot_nt(wtr_ref[...], hn)
    cost = cost_ref[...]
    sint = sint_ref[...]
    gq = jnp.concatenate([gq_ref[...]] * (tm // LANES_V7X), axis=1)
    half = HEAD_DIM // 2
    for h in range(QT_ROWS // HEAD_DIM):
        blk = tr[h * HEAD_DIM:(h + 1) * HEAD_DIM]
        is_swa = h >= MOBA_HEADS
        gain = gq[HEAD_DIM:] if is_swa else gq[:HEAD_DIM]
        msq = jnp.mean(blk * blk, axis=0, keepdims=True)
        qn = blk * lax.rsqrt(msq + NORM_EPS) * gain
        x1, x2 = qn[:half], qn[half:]
        rot = jnp.concatenate([x1 * cost - x2 * sint, x2 * cost + x1 * sint], axis=0).astype(BF16)
        if is_swa:
            r0 = (h - MOBA_HEADS) * HEAD_DIM
            sqt_ref[r0:r0 + HEAD_DIM, :] = rot
        else:
            mqt_ref[h * HEAD_DIM:(h + 1) * HEAD_DIM, :] = rot
    r = QT_ROWS
    mv = tr[r:r + MOBA_WIDTH].astype(BF16)
    ones = jnp.ones((MOBA_V_ROWS - HEAD_DIM, MOBA_BLOCK), BF16)
    for c in range(nblk):
        for h in range(MOBA_HEADS):
            r0 = h * MOBA_V_ROWS
            mvt_ref[c, r0:r0 + HEAD_DIM, :] = mv[h * HEAD_DIM:(h + 1) * HEAD_DIM,
                                                 c * MOBA_BLOCK:(c + 1) * MOBA_BLOCK]
            mvt_ref[c, r0 + HEAD_DIM:r0 + MOBA_V_ROWS, :] = ones
    r += MOBA_WIDTH
    svt_ref[...] = tr[r:r + SWA_KV_WIDTH].astype(BF16)
    r += SWA_KV_WIDTH
    xvt_ref[...] = tr[r:r + MLSTM_WIDTH].astype(BF16)
    r += MLSTM_WIDTH
    xot_ref[...] = tr[r:r + MLSTM_WIDTH].astype(BF16)
    r += MLSTM_WIDTH
    gbias = jnp.concatenate([gbias_ref[...]] * (tm // LANES_V7X), axis=1)
    gt_ref[...] = tr[r:r + NUM_GATES] + gbias


def _in_proj(xf, ln1, wnat, wtr, bd, gk, cosn, sinn, gq, cost, sint, gbias, seq):
    t = xf.shape[0]
    tm = IN_PROJ_ROWS
    steps = t // tm
    seq_steps = seq // tm
    nblk = tm // MOBA_BLOCK
    row = lambda w: pl.BlockSpec((tm, w), lambda i: (i, 0))
    col = lambda r: pl.BlockSpec((r, tm), lambda i: (0, i))
    in_specs = [
        row(D_MODEL),
        _const_spec((1, D_MODEL)),
        _const_spec(wnat.shape),
        _const_spec(wtr.shape),
        _const_spec(bd.shape),
        _const_spec(gk.shape),
        pl.BlockSpec((tm, LANES_V7X), lambda i: (i % seq_steps, 0)),
        pl.BlockSpec((tm, LANES_V7X), lambda i: (i % seq_steps, 0)),
        _const_spec(gq.shape),
        pl.BlockSpec((HEAD_DIM // 2, tm), lambda i: (0, i % seq_steps)),
        pl.BlockSpec((HEAD_DIM // 2, tm), lambda i: (0, i % seq_steps)),
        _const_spec(gbias.shape),
    ]
    out_shape = [
        jax.ShapeDtypeStruct((t, MOBA_WIDTH), BF16),
        jax.ShapeDtypeStruct((steps, nblk, MOBA_WIDTH), F32),
        jax.ShapeDtypeStruct((t, SWA_KV_WIDTH), BF16),
        jax.ShapeDtypeStruct((t, 2 * MLSTM_WIDTH), BF16),
        jax.ShapeDtypeStruct((MOBA_WIDTH, t), BF16),
        jax.ShapeDtypeStruct((SWA_Q_WIDTH, t), BF16),
        jax.ShapeDtypeStruct((t // MOBA_BLOCK, MOBA_HEADS * MOBA_V_ROWS, MOBA_BLOCK), BF16),
        jax.ShapeDtypeStruct((SWA_KV_WIDTH, t), BF16),
        jax.ShapeDtypeStruct((MLSTM_WIDTH, t), BF16),
        jax.ShapeDtypeStruct((MLSTM_WIDTH, t), BF16),
        jax.ShapeDtypeStruct((NUM_GATES, t), F32),
    ]
    out_specs = [
        row(MOBA_WIDTH),
        pl.BlockSpec((1, nblk, MOBA_WIDTH), lambda i: (i, 0, 0)),
        row(SWA_KV_WIDTH),
        row(2 * MLSTM_WIDTH),
        col(MOBA_WIDTH),
        col(SWA_Q_WIDTH),
        pl.BlockSpec((nblk, MOBA_HEADS * MOBA_V_ROWS, MOBA_BLOCK), lambda i: (i, 0, 0)),
        col(SWA_KV_WIDTH),
        col(MLSTM_WIDTH),
        col(MLSTM_WIDTH),
        col(NUM_GATES),
    ]
    weights = 2 * (wnat.size + wtr.size)
    tiles = tm * (2 * 4 * D_MODEL + 3 * 4 * (NAT_WIDTH + TR_ROWS) + 2 * 2 * (NAT_WIDTH + TR_ROWS))
    return pl.pallas_call(
        _in_proj_kernel,
        grid=(steps,),
        in_specs=in_specs,
        out_specs=out_specs,
        out_shape=out_shape,
        compiler_params=pltpu.CompilerParams(
            dimension_semantics=("arbitrary",),
            vmem_limit_bytes=_vmem_limit(2 * weights + tiles)),
        name="in_proj",
    )(xf, ln1, wnat, wtr, bd, gk, cosn, sinn, gq, cost, sint, gbias)


def _moba_kernel(bounded_ref, qt_ref, k_ref, vt_ref, km_ref, o_ref, sel_ref, s_ref, p_ref):
    nb = k_ref.shape[0]
    lq = MOBA_BLOCK
    nh = MOBA_Q_BLOCKS * MOBA_HEADS
    step = pl.program_id(1)
    row = lax.broadcasted_iota(jnp.int32, (HEAD_PAIR, lq), 0)
    own, qh = [], []
    for s in range(nh):
        qb, h = divmod(s, MOBA_HEADS)
        own.append(step * MOBA_Q_BLOCKS + qb)
        qt = qt_ref[(h // 2) * HEAD_PAIR:(h // 2 + 1) * HEAD_PAIR, qb * lq:(qb + 1) * lq]
        keep = (row < HEAD_DIM) if h % 2 == 0 else (row >= HEAD_DIM)
        qh.append(jnp.where(keep, qt, jnp.zeros_like(qt)))
    last_own = step * MOBA_Q_BLOCKS + (MOBA_Q_BLOCKS - 1)

    def k_group(blk0, count, s):
        g = (s % MOBA_HEADS) // 2
        return k_ref[pl.ds(blk0, count), :, g * HEAD_PAIR:(g + 1) * HEAD_PAIR].reshape(count * lq, HEAD_PAIR)

    def v_aug(j, s):
        h = s % MOBA_HEADS
        return vt_ref[j, h * MOBA_V_ROWS:(h + 1) * MOBA_V_ROWS, :]

    def select_blocks():
        gates = []
        for s in range(nh):
            g = (s % MOBA_HEADS) // 2
            km_hi, km_lo = _split_bf16(km_ref[:, g * HEAD_PAIR:(g + 1) * HEAD_PAIR])
            gates.append(_dot(km_hi, qh[s]) + _dot(km_lo, qh[s]))
        gate = jnp.concatenate(gates, axis=1)
        own_all = jnp.concatenate([jnp.full((1, lq), own[s], jnp.int32) for s in range(nh)], axis=1)
        blk_all = lax.broadcasted_iota(jnp.int32, gate.shape, 0)
        gate = jnp.where(blk_all < own_all, gate, NEG_INF)
        sel = jnp.full(gate.shape, NEG_INF, F32)
        for _ in range(MOBA_TOPK):
            mx = jnp.max(gate, axis=0, keepdims=True)
            idx = jnp.min(jnp.where(gate == mx, blk_all, nb), axis=0, keepdims=True)
            pick = blk_all == jnp.where(idx < own_all, idx, nb)
            sel = jnp.where(pick, 0.0, sel)
            gate = jnp.where(pick, NEG_INF, gate)
        for s in range(nh):
            sel_ref[s] = sel[:, s * lq:(s + 1) * lq]

    kpos = lax.broadcasted_iota(jnp.int32, (lq, lq), 0)
    qpos = lax.broadcasted_iota(jnp.int32, (lq, lq), 1)
    causal = kpos <= qpos

    def weighted_values(st, mu, v_rows):
        pv = _dot(v_rows, jnp.exp2(st - mu).astype(BF16))
        return pv[HEAD_DIM:HEAD_DIM + 1], pv[:HEAD_DIM]

    def finish(nums, dens):
        for s in range(0, nh, 2):
            qb, h = divmod(s, MOBA_HEADS)
            ot = jnp.concatenate([nums[s] / dens[s], nums[s + 1] / dens[s + 1]], axis=0)
            o_ref[qb * lq:(qb + 1) * lq, (h // 2) * HEAD_PAIR:(h // 2 + 1) * HEAD_PAIR] = ot.T.astype(BF16)

    trips = lax.shift_right_logical(last_own + (2 * MOBA_GROUP - 1), MOBA_GROUP.bit_length())
    units = nh * MOBA_GROUP
    own_slot = 2 * units

    @pl.when(bounded_ref[0] != 0)
    def _():
        select_blocks()
        for h in range(nh):
            st = jnp.where(causal, _dot(k_group(own[h], 1, h), qh[h]), NEG_INF)
            p_ref[own_slot + h] = jnp.exp2(st).astype(BF16)

        def stage_p(blk0, slot0):
            blk0 = jnp.minimum(blk0, nb - MOBA_GROUP)
            for h in range(nh):
                st = _dot(k_group(blk0, MOBA_GROUP, h), qh[h])
                sel = jnp.concatenate(
                    [jnp.broadcast_to(sel_ref[h, pl.ds(blk0 + u, 1), :], (lq, lq)) for u in range(MOBA_GROUP)],
                    axis=0)
                slot = slot0 + MOBA_GROUP * h
                p_ref[slot:slot + MOBA_GROUP] = jnp.exp2(st + sel).astype(BF16).reshape(MOBA_GROUP, lq, lq)

        def stage_v(blk0, slot0, acc):
            acc = list(acc)
            for h in range(nh):
                slot = slot0 + MOBA_GROUP * h
                p = p_ref[slot:slot + MOBA_GROUP].reshape(MOBA_GROUP * lq, lq)
                v = jnp.concatenate([v_aug(blk0 + u, h) for u in range(MOBA_GROUP)], axis=1)
                acc[h] = acc[h] + _dot(v, p)
            return acc

        def trip(t, acc, last):
            blk = t * (2 * MOBA_GROUP)
            stage_p(blk + MOBA_GROUP, units)
            acc = stage_v(blk, 0, acc)
            if not last:
                stage_p(blk + 2 * MOBA_GROUP, 0)
            return tuple(stage_v(blk + MOBA_GROUP, units, acc))

        stage_p(0, 0)
        acc = tuple(_dot(v_aug(own[h], h), p_ref[own_slot + h]) for h in range(nh))
        acc = lax.fori_loop(0, jnp.maximum(trips - 1, 0), lambda t, a: trip(t, a, False), acc)
        acc = lax.cond(trips > 0, lambda a: trip(trips - 1, a, True), lambda a: a, acc)
        finish([a[:HEAD_DIM] for a in acc], [a[HEAD_DIM:HEAD_DIM + 1] for a in acc])


    def stage_a(blk0, slot0):
        mus = []
        for u in range(MOBA_GROUP):
            for h in range(nh):
                st = _dot(k_group(jnp.minimum(blk0 + u, nb - 1), 1, h), qh[h])
                s_ref[slot0 + nh * u + h] = st
                mus.append(jnp.max(st, axis=0, keepdims=True))
        return mus

    def stage_b(blk0, slot0, mus, state):
        state = list(state)
        for u in range(MOBA_GROUP):
            j = blk0 + u
            for h in range(nh):
                mu = mus[nh * u + h]
                ls, pv = weighted_values(s_ref[slot0 + nh * u + h], mu, v_aug(j, h))
                m, l, acc = state[3 * h:3 * h + 3]
                mu_sel = mu + sel_ref[h, pl.ds(j, 1), :]
                m_new = jnp.maximum(m, mu_sel)
                f = jnp.exp2(mu_sel - m_new)
                c = jnp.exp2(m - m_new)
                state[3 * h:3 * h + 3] = [m_new, c * l + f * ls, c * acc + f * pv]
        return state

    def body(t, loop_carry):
        mus0, state = loop_carry[:units], loop_carry[units:]
        blk = t * (2 * MOBA_GROUP)
        mus1 = stage_a(blk + MOBA_GROUP, units)
        state = stage_b(blk, 0, mus0, state)
        mus0 = stage_a(blk + 2 * MOBA_GROUP, 0)
        state = stage_b(blk + MOBA_GROUP, units, mus1, state)
        return (*mus0, *state)

    @pl.when(bounded_ref[0] == 0)
    def _():
        state = []
        for h in range(nh):
            st = jnp.where(causal, _dot(k_group(own[h], 1, h), qh[h]), NEG_INF)
            mu = jnp.max(st, axis=0, keepdims=True)
            state += [mu, *weighted_values(st, mu, v_aug(own[h], h))]
        mus0 = stage_a(0, 0)
        select_blocks()
        state = lax.fori_loop(0, trips, body, (*mus0, *state))[units:]
        finish(state[2::3], state[1::3])


def _moba(bounded, mqt, mk, mvt, kmean, batch, seq):
    t = mk.shape[0]
    nb = seq // MOBA_BLOCK
    lq = MOBA_BLOCK
    k3 = mk.reshape(t // MOBA_BLOCK, MOBA_BLOCK, MOBA_WIDTH)
    assert nb % (2 * MOBA_GROUP) == 0 and nb % MOBA_Q_BLOCKS == 0
    streams = MOBA_Q_BLOCKS * MOBA_HEADS
    units = streams * MOBA_GROUP
    steps = nb // MOBA_Q_BLOCKS
    kv_bytes = 2 * nb * MOBA_BLOCK * (MOBA_WIDTH + MOBA_HEADS * MOBA_V_ROWS)
    scratch_bytes = lq * lq * (4 * 2 * units + 2 * (2 * units + streams)) + 4 * streams * nb * lq
    resident = lambda shape, imap: pl.BlockSpec(shape, imap, pipeline_mode=pl.Buffered(1))
    return pl.pallas_call(
        _moba_kernel,
        grid=(batch, steps),
        in_specs=[
            pl.BlockSpec(memory_space=pltpu.SMEM),
            pl.BlockSpec((MOBA_WIDTH, MOBA_Q_BLOCKS * lq), lambda b, i: (0, b * steps + i)),
            resident((nb, MOBA_BLOCK, MOBA_WIDTH), lambda b, i: (b, 0, 0)),
            resident((nb, MOBA_HEADS * MOBA_V_ROWS, MOBA_BLOCK), lambda b, i: (b, 0, 0)),
            resident((nb, MOBA_WIDTH), lambda b, i: (b, 0)),
        ],
        out_specs=pl.BlockSpec((MOBA_Q_BLOCKS * lq, MOBA_WIDTH), lambda b, i: (b * steps + i, 0)),
        out_shape=jax.ShapeDtypeStruct((t, MOBA_WIDTH), BF16),
        scratch_shapes=[pltpu.VMEM((streams, nb, lq), F32),
                        pltpu.VMEM((2 * units, lq, lq), F32),
                        pltpu.VMEM((2 * units + streams, lq, lq), BF16)],
        compiler_params=pltpu.CompilerParams(
            dimension_semantics=("arbitrary", "arbitrary"),
            vmem_limit_bytes=_vmem_limit(kv_bytes + scratch_bytes + VMEM_HEADROOM_BYTES)),
        name="moba",
    )(bounded, mqt, k3, mvt, kmean)


def _swa_kernel(bounded_ref, sink_ref, qt_ref, k_ref, kh_ref, vt_ref, vth_ref, o_ref, p_ref):
    tq = k_ref.shape[0]
    w = SWA_WINDOW
    i = pl.program_id(1)
    ncol = SWA_GROUP * w
    kpos = lax.broadcasted_iota(jnp.int32, (2 * w, ncol), 0)
    col = lax.broadcasted_iota(jnp.int32, (2 * w, ncol), 1)
    diff = w + (col % w) - kpos
    in_window = (diff >= 0) & (diff < w)
    zeros = jnp.zeros((HEAD_DIM, ncol), BF16)
    ones = jnp.ones((MOBA_V_ROWS - HEAD_DIM, 2 * w), BF16)

    def attend(shifted):
        sink_terms = []
        for r in range(tq // w):
            if r == 0:
                kcat = jnp.concatenate([kh_ref[...], k_ref[0:w, :]], axis=0)
                mask = in_window & ((kpos >= w) | (i > 0))
            else:
                kcat = k_ref[(r - 1) * w:(r + 1) * w, :]
                mask = in_window
            for g in range(SWA_KV_HEADS):
                qs = jnp.concatenate(
                    [qt_ref[(SWA_GROUP * g + a) * HEAD_DIM:(SWA_GROUP * g + a + 1) * HEAD_DIM, r * w:(r + 1) * w]
                     for a in range(SWA_GROUP)], axis=1)
                qpad = jnp.concatenate([qs, zeros] if g == 0 else [zeros, qs], axis=0)
                st = jnp.where(mask, _dot(kcat, qpad), NEG_INF)
                sink = jnp.concatenate(
                    [jnp.full((1, w), sink_ref[SWA_GROUP * g + a] * LOG2E, F32) for a in range(SWA_GROUP)],
                    axis=1)
                if shifted:
                    m = jnp.maximum(jnp.max(st, axis=0, keepdims=True), sink)
                    st, sink = st - m, sink - m
                p_ref[r * SWA_KV_HEADS + g] = jnp.exp2(st).astype(BF16)
                sink_terms.append(jnp.exp2(sink))
        for r in range(tq // w):
            if r == 0:
                vcat = jnp.concatenate([vth_ref[...], vt_ref[:, 0:w]], axis=1)
            else:
                vcat = vt_ref[:, (r - 1) * w:(r + 1) * w]
            pieces = []
            for g in range(SWA_KV_HEADS):
                v_aug = jnp.concatenate([vcat[g * HEAD_DIM:(g + 1) * HEAD_DIM, :], ones], axis=0)
                pv = _dot(v_aug, p_ref[r * SWA_KV_HEADS + g])
                ot = pv[:HEAD_DIM] / (pv[HEAD_DIM:HEAD_DIM + 1] + sink_terms[r * SWA_KV_HEADS + g])
                pieces += [ot[:, a * w:(a + 1) * w] for a in range(SWA_GROUP)]
            o_ref[r * w:(r + 1) * w, :] = jnp.concatenate(pieces, axis=0).T.astype(BF16)

    @pl.when(bounded_ref[0] != 0)
    def _():
        attend(False)

    @pl.when(bounded_ref[0] == 0)
    def _():
        attend(True)


def _swa(bounded, sinks, sqt, sk, svt, batch, seq):
    t = sk.shape[0]
    tq = SWA_ROWS
    w = SWA_WINDOW
    steps = seq // tq
    halo = lambda b, i: jnp.maximum((b * seq + i * tq) // w - 1, 0)
    return pl.pallas_call(
        _swa_kernel,
        grid=(batch, steps),
        in_specs=[
            pl.BlockSpec(memory_space=pltpu.SMEM),
            pl.BlockSpec(memory_space=pltpu.SMEM),
            pl.BlockSpec((SWA_Q_WIDTH, tq), lambda b, i: (0, b * steps + i)),
            pl.BlockSpec((tq, SWA_KV_WIDTH), lambda b, i: (b * steps + i, 0)),
            pl.BlockSpec((w, SWA_KV_WIDTH), lambda b, i: (halo(b, i), 0)),
            pl.BlockSpec((SWA_KV_WIDTH, tq), lambda b, i: (0, b * steps + i)),
            pl.BlockSpec((SWA_KV_WIDTH, w), lambda b, i: (0, halo(b, i))),
        ],
        out_specs=pl.BlockSpec((tq, SWA_Q_WIDTH), lambda b, i: (b * steps + i, 0)),
        out_shape=jax.ShapeDtypeStruct((t, SWA_Q_WIDTH), BF16),
        scratch_shapes=[pltpu.VMEM(((tq // w) * SWA_KV_HEADS, 2 * w, SWA_GROUP * w), BF16)],
        compiler_params=pltpu.CompilerParams(dimension_semantics=("arbitrary", "arbitrary")),
        name="swa",
    )(bounded, sinks, sqt, sk, sk, svt, svt)


def _log_sigmoid(v):
    return jnp.minimum(v, 0.0) - jnp.log1p(jnp.exp(-jnp.abs(v)))


def _mlstm_kernel(xqk_ref, halo_ref, xvt_ref, xot_ref, gt_ref, gc_ref, convw_ref, convb_ref,
                  gain_ref, cum_ref, cumt_ref, o_ref, ext_ref, c_ref, n_ref, m_ref,
                  qm_ref, k_ref, pv_ref, row_ref, kv_ref, nk_ref, cprev_ref, nprev_ref, ht_ref):
    tt = xqk_ref.shape[0]
    lc = MLSTM_CHUNK
    i = pl.program_id(1)

    @pl.when(i == 0)
    def _():
        c_ref[...] = jnp.zeros_like(c_ref)
        n_ref[...] = jnp.zeros_like(n_ref)
        m_ref[...] = jnp.zeros_like(m_ref)

    hist = halo_ref[...].astype(F32)
    ext_ref[0:CONV_HALO, :] = jnp.where(i > 0, hist, jnp.zeros_like(hist))
    ext_ref[CONV_HALO:CONV_HALO + tt, :] = xqk_ref[...].astype(F32)
    convw = convw_ref[...]
    conv = jnp.zeros((tt, 2 * MLSTM_WIDTH), F32) + convb_ref[...]
    for j in range(CONV_WIDTH):
        start = CONV_HALO - (CONV_WIDTH - 1) + j
        conv = conv + convw[j:j + 1, :] * ext_ref[start:start + tt, :]
    qk = conv * jax.nn.sigmoid(conv)
    q_all = qk[:, :MLSTM_WIDTH]
    lane_q = lax.broadcasted_iota(jnp.int32, q_all.shape, 1)
    even_head = (lane_q % HEAD_PAIR) < HEAD_DIM
    qm_ref[0] = jnp.where(even_head, q_all, 0.0).astype(BF16)
    qm_ref[1] = jnp.where(even_head, 0.0, q_all).astype(BF16)
    k_ref[...] = (qk[:, MLSTM_WIDTH:] * SM_SCALE).astype(BF16)

    g_row = gt_ref[...]
    g_col = gc_ref[...]
    cum = cum_ref[...]
    cumt = cumt_ref[...]
    b_row = sum(_dot(piece, cum) for piece in _split3_bf16(_log_sigmoid(g_row)))
    b_col = sum(_dot(cumt, piece) for piece in _split3_bf16(_log_sigmoid(g_col)))
    u_col = b_col[:, MLSTM_HEADS:] - g_col[:, :MLSTM_HEADS]

    lane = lax.broadcasted_iota(jnp.int32, (HEAD_DIM, HEAD_PAIR), 1)
    s_idx = lax.broadcasted_iota(jnp.int32, (lc, lc), 0)
    t_idx = lax.broadcasted_iota(jnp.int32, (lc, lc), 1)
    causal = s_idx <= t_idx
    nchunks = tt // lc
    heads = range(MLSTM_HEADS)
    in_head = [(lane < HEAD_DIM), (lane >= HEAD_DIM)]

    def operands(c, h):
        p, hh = divmod(h, 2)
        sl = slice(c * lc, (c + 1) * lc)
        qm = qm_ref[hh, sl, p * HEAD_PAIR:(p + 1) * HEAD_PAIR]
        kp = k_ref[sl, p * HEAD_PAIR:(p + 1) * HEAD_PAIR]
        return sl, hh, qm, kp

    m_state = [m_ref[h][0:1, 0:lc] for h in heads]
    for c in range(nchunks):
        for h in heads:
            u = c * MLSTM_HEADS + h
            sl, hh, qm, kp = operands(c, h)
            vt = xvt_ref[h * HEAD_DIM:(h + 1) * HEAD_DIM, sl]
            br = b_row[MLSTM_HEADS + h:MLSTM_HEADS + h + 1, sl]
            ir = g_row[h:h + 1, sl]
            uc = u_col[sl, h:h + 1]
            a = br[:, lc - 1:lc]
            dt = jnp.where(causal, br - uc, NEG_INF)
            inter = br + m_state[h]
            m_t = jnp.maximum(inter, jnp.max(dt, axis=0, keepdims=True))
            w_inter = jnp.exp(inter - m_t)
            qkt = _dot_nt(kp, qm) * jnp.exp(dt - m_t)
            pv_ref[u] = _dot(vt, qkt.astype(BF16))
            row_ref[u, 0:1, 0:lc] = w_inter
            row_ref[u, 1:2, 0:lc] = jnp.sum(qkt, axis=0, keepdims=True)
            row_ref[u, 2:3, 0:lc] = jnp.exp(-m_t)
            g_end = a - br + ir
            m_new = jnp.maximum(a + m_state[h], jnp.max(g_end, axis=1, keepdims=True))
            w_s = jnp.exp(g_end - m_new)
            decay = jnp.exp(a + m_state[h] - m_new)
            row_ref[u, 3:4, 0:lc] = decay
            vtw = (vt.astype(F32) * w_s).astype(BF16)
            kv_ref[u] = jnp.where(in_head[hh], _dot(vtw, kp), 0.0)
            w_rows = jnp.broadcast_to(w_s, (8, lc)).astype(BF16)
            nk_ref[u] = jnp.where(in_head[hh][0:8], _dot(w_rows, kp), 0.0)
            m_state[h] = m_new

    for h in heads:
        c_state = c_ref[h]
        n_state = n_ref[h]
        for c in range(nchunks):
            u = c * MLSTM_HEADS + h
            cprev_ref[u] = c_state.astype(BF16)
            nprev_ref[u] = n_state
            decay = row_ref[u, 3:4, 0:HEAD_PAIR]
            c_state = decay * c_state + kv_ref[u]
            n_state = decay * n_state + nk_ref[u]
        c_ref[h] = c_state
        n_ref[h] = n_state
        m_ref[h] = jnp.broadcast_to(m_state[h][:, 0:1], m_ref.shape[1:])

    for c in range(nchunks):
        for h in heads:
            u = c * MLSTM_HEADS + h
            sl, hh, qm, _ = operands(c, h)
            w_inter = row_ref[u, 0:1, 0:lc]
            num = w_inter * _dot_nt(cprev_ref[u], qm) + pv_ref[u]
            den = w_inter * _dot_nt(nprev_ref[u].astype(BF16), qm)[0:1, :] + row_ref[u, 1:2, 0:lc]
            ht_ref[h * HEAD_DIM:(h + 1) * HEAD_DIM, sl] = num / jnp.maximum(jnp.abs(den), row_ref[u, 2:3, 0:lc])

    gain = jnp.concatenate([gain_ref[...]] * (tt // LANES_V7X), axis=1)
    outs = []
    for h in heads:
        rows = slice(h * HEAD_DIM, (h + 1) * HEAD_DIM)
        hg = ht_ref[rows, :] * jax.nn.sigmoid(xot_ref[rows, :].astype(F32))
        msq = jnp.mean(hg * hg, axis=0, keepdims=True)
        outs.append(hg * lax.rsqrt(msq + NORM_EPS) * gain[rows])
    o_ref[...] = jnp.concatenate(outs, axis=0).T.astype(BF16)


def _mlstm(xqk, xvt, xot, gt, gc, convw, convb, gain, cum, batch, seq):
    t = xqk.shape[0]
    tt = MLSTM_ROWS
    steps = seq // tt
    units = MLSTM_HEADS * (tt // MLSTM_CHUNK)
    col = lambda r: pl.BlockSpec((r, tt), lambda b, i: (0, b * steps + i))
    return pl.pallas_call(
        _mlstm_kernel,
        grid=(batch, steps),
        in_specs=[
            pl.BlockSpec((tt, 2 * MLSTM_WIDTH), lambda b, i: (b * steps + i, 0)),
            pl.BlockSpec((CONV_HALO, 2 * MLSTM_WIDTH),
                         lambda b, i: (jnp.maximum((b * seq + i * tt) // CONV_HALO - 1, 0), 0)),
            col(MLSTM_WIDTH),
            col(MLSTM_WIDTH),
            col(NUM_GATES),
            pl.BlockSpec((tt, NUM_GATES), lambda b, i: (b * steps + i, 0)),
            _const_spec(convw.shape),
            _const_spec(convb.shape),
            _const_spec(gain.shape),
            _const_spec(cum.shape),
            _const_spec(cum.shape),
        ],
        out_specs=pl.BlockSpec((tt, MLSTM_WIDTH), lambda b, i: (b * steps + i, 0)),
        out_shape=jax.ShapeDtypeStruct((t, MLSTM_WIDTH), BF16),
        scratch_shapes=[
            pltpu.VMEM((CONV_HALO + tt, 2 * MLSTM_WIDTH), F32),
            pltpu.VMEM((MLSTM_HEADS, HEAD_DIM, HEAD_PAIR), F32),
            pltpu.VMEM((MLSTM_HEADS, 8, HEAD_PAIR), F32),
            pltpu.VMEM((MLSTM_HEADS, 8, MLSTM_CHUNK), F32),
            pltpu.VMEM((2, tt, MLSTM_WIDTH), BF16),
            pltpu.VMEM((tt, MLSTM_WIDTH), BF16),
            pltpu.VMEM((units, HEAD_DIM, MLSTM_CHUNK), F32),
            pltpu.VMEM((units, 8, MLSTM_CHUNK), F32),
            pltpu.VMEM((units, HEAD_DIM, HEAD_PAIR), F32),
            pltpu.VMEM((units, 8, HEAD_PAIR), F32),
            pltpu.VMEM((units, HEAD_DIM, HEAD_PAIR), BF16),
            pltpu.VMEM((units, 8, HEAD_PAIR), F32),
            pltpu.VMEM((MLSTM_WIDTH, tt), F32),
        ],
        compiler_params=pltpu.CompilerParams(dimension_semantics=("arbitrary", "arbitrary")),
        name="mlstm",
    )(xqk, xqk, xvt, xot, gt, gc, convw, convb, gain, cum, cum.T)


def _out_mlp_kernel(x_ref, ym_ref, yl_ref, ys_ref, wo_ref, ln2_ref, wup_ref, wdn_ref, o_ref):
    y = jnp.concatenate([ym_ref[...], yl_ref[...], ys_ref[...]], axis=1)
    x1 = x_ref[...] + _dot(y, wo_ref[...])
    ms = jnp.mean(x1 * x1, axis=-1, keepdims=True)
    hn = (x1 * lax.rsqrt(ms + NORM_EPS) * ln2_ref[...]).astype(BF16)
    o_ref[...] = x1
    for c in range(0, D_FF, MLP_FF_CHUNK):
        u = _dot(hn, wup_ref[:, c:c + MLP_FF_CHUNK])
        act = jnp.square(jnp.maximum(u, 0.0)).astype(BF16)
        o_ref[...] += _dot(act, wdn_ref[c:c + MLP_FF_CHUNK, :])


def _out_mlp(xf, ym, yl, ys, wo, ln2, wup, wdn):
    t = xf.shape[0]
    tm = OUT_MLP_ROWS
    row = lambda w: pl.BlockSpec((tm, w), lambda i: (i, 0))
    weights = 2 * (wo.size + wup.size + wdn.size)
    tiles = tm * (4 * 4 * D_MODEL + 2 * 2 * D_MODEL + 4 * 4 * D_MODEL + 6 * MLP_FF_CHUNK)
    return pl.pallas_call(
        _out_mlp_kernel,
        grid=(t // tm,),
        in_specs=[
            row(D_MODEL), row(MOBA_WIDTH), row(MLSTM_WIDTH), row(SWA_Q_WIDTH),
            _const_spec(wo.shape), _const_spec(ln2.shape), _const_spec(wup.shape), _const_spec(wdn.shape),
        ],
        out_specs=row(D_MODEL),
        out_shape=jax.ShapeDtypeStruct((t, D_MODEL), F32),
        compiler_params=pltpu.CompilerParams(
            dimension_semantics=("arbitrary",),
            vmem_limit_bytes=_vmem_limit(weights + tiles)),
        name="out_mlp",
    )(xf, ym, yl, ys, wo, ln2, wup, wdn)


def _rope_tables(seq):
    inv = ROPE_THETA ** (-jnp.arange(0, HEAD_DIM, 2, dtype=F32) / HEAD_DIM)
    ang = jnp.arange(seq, dtype=F32)[:, None] * inv[None, :]
    cos, sin = jnp.cos(ang), jnp.sin(ang)
    cosn = jnp.concatenate([cos, cos, cos, cos], axis=1)
    sinn = jnp.concatenate([-sin, sin, -sin, sin], axis=1)
    return cosn, sinn, cos.T, sin.T


def _layer(xf, tables, consts, batch, seq, ln1, w_in, conv_w, conv_b, igate_b, fgate_b, mlstm_norm,
           moba_q_norm, moba_k_norm, swa_q_norm, swa_k_norm, swa_sinks, w_out, ln2, w_up, w_down):
    cosn, sinn, cost, sint = tables
    bd, cum = consts
    o = 0
    cols = {}
    for name, width in (("mq", MOBA_WIDTH), ("mk", MOBA_WIDTH), ("mv", MOBA_WIDTH), ("sq", SWA_Q_WIDTH),
                        ("sk", SWA_KV_WIDTH), ("sv", SWA_KV_WIDTH), ("xqk", 2 * MLSTM_WIDTH),
                        ("xv", MLSTM_WIDTH), ("xo", MLSTM_WIDTH), ("xi", MLSTM_HEADS), ("xf", MLSTM_HEADS)):
        cols[name] = w_in[:, o:o + width]
        o += width
    wnat = jnp.concatenate([cols["mk"], cols["sk"], cols["xqk"]], axis=1).astype(BF16)
    wtr = jnp.concatenate([cols["mq"], cols["sq"], cols["mv"], cols["sv"], cols["xv"], cols["xo"],
                           cols["xi"], cols["xf"]], axis=1).T.astype(BF16)
    gk = jnp.concatenate([jnp.tile(moba_k_norm, MOBA_HEADS), jnp.tile(swa_k_norm, SWA_KV_HEADS)])[None, :]
    gq = jnp.broadcast_to(
        (jnp.concatenate([moba_q_norm, swa_q_norm]) * (SM_SCALE * LOG2E))[:, None], (2 * HEAD_DIM, LANES_V7X))
    gbias = jnp.broadcast_to(jnp.concatenate([igate_b, fgate_b])[:, None], (NUM_GATES, LANES_V7X))

    (mk, kmean, sk, xqk, mqt, sqt, mvt, svt, xvt, xot, gt) = _in_proj(
        xf, ln1[None, :], wnat, wtr, bd, gk, cosn, sinn, gq, cost, sint, gbias, seq)

    kmean = kmean.reshape(-1, MOBA_WIDTH)
    def bounded(q_gain, k_gain, *extra):
        bound = (HEAD_DIM * SM_SCALE * LOG2E) * jnp.max(jnp.abs(q_gain)) * jnp.max(jnp.abs(k_gain))
        for e in extra:
            bound = jnp.maximum(bound, jnp.max(jnp.abs(e)) * LOG2E)
        return (bound <= MOBA_SAFE_LOG2).astype(jnp.int32).reshape(1)

    ym = _moba(bounded(moba_q_norm, moba_k_norm), mqt, mk, mvt, kmean, batch, seq)
    ys = _swa(bounded(swa_q_norm, swa_k_norm, swa_sinks), swa_sinks, sqt, sk, svt, batch, seq)
    gain = jnp.broadcast_to(mlstm_norm.reshape(MLSTM_WIDTH, 1), (MLSTM_WIDTH, LANES_V7X))
    yl = _mlstm(xqk, xvt, xot, gt, gt.T, conv_w, conv_b[None, :], gain, cum, batch, seq)

    return _out_mlp(xf, ym, yl, ys, w_out.astype(BF16), ln2[None, :], w_up.astype(BF16), w_down.astype(BF16))


def kernel(x, ln1, w_in, conv_w, conv_b, igate_b, fgate_b, mlstm_norm, moba_q_norm, moba_k_norm,
           swa_q_norm, swa_k_norm, swa_sinks, w_out, ln2, w_up, w_down):
    batch, seq, d = x.shape
    assert d == D_MODEL and seq % max(IN_PROJ_ROWS, SWA_ROWS, MLSTM_ROWS, MOBA_BLOCK) == 0
    depth = ln1.shape[0]
    tables = _rope_tables(seq)
    bw = 2 * LANES_V7X
    ids = jnp.arange(bw) // HEAD_DIM
    bd = jnp.where(ids[:, None] == ids[None, :], 1.0 / HEAD_DIM, 0.0).astype(BF16)
    tids = jnp.arange(MLSTM_ROWS)
    cum = ((tids[:, None] // MLSTM_CHUNK == tids[None, :] // MLSTM_CHUNK)
           & (tids[:, None] <= tids[None, :])).astype(BF16)
    xf = x.reshape(batch * seq, d)
    for l in range(depth):
        xf = _layer(xf, tables, (bd, cum), batch, seq, ln1[l], w_in[l], conv_w[l], conv_b[l], igate_b[l],
                    fgate_b[l], mlstm_norm[l], moba_q_norm[l], moba_k_norm[l], swa_q_norm[l],
                    swa_k_norm[l], swa_sinks[l], w_out[l], ln2[l], w_up[l], w_down[l])
    return xf.reshape(batch, seq, d)
```
